```python
import jax
import jax.numpy as jnp
from jax import lax
import numpy as np

D_MODEL = 2048
BATCH = 4
SEQ = 2048
DEPTH = 2
DEC_BATCH = 128
DEC_SEQ = 8
PAST_LEN = 16384
PAGE_SIZE = 128

A_HEADS = 8
A_DK = 128
A_DV = 128
A_QK = A_HEADS * A_DK
A_WIDTH = A_HEADS * A_DV
F_TINY = 1e-30
B_HEAD = 64
B_WIDTH = 1024
B_HEADS = B_WIDTH // B_HEAD
B_LORA_W = 64
B_LORA_A = 64
B_LORA_G = 128
RWKV_GN_EPS = 64e-5
C_HEADS = 4
C_DK = 128
C_DV = 256
C_QK = C_HEADS * C_DK
C_WIDTH = C_HEADS * C_DV
ROPE_BASE = 10000.0
A_COLS = 2 * A_QK + 2 * A_WIDTH
B_COLS = 3 * B_WIDTH + B_LORA_W + B_LORA_A + B_LORA_G
C_COLS = 2 * C_QK + 2 * C_WIDTH
N_BRANCH = 3
P_COLS = A_COLS + B_COLS + C_COLS + N_BRANCH * D_MODEL
D_FF = 5632
CONV_W = 3
CHUNK = 64
NORM_EPS = 1e-6

kernel_name = 'hybrid_hgrn2_rwkv7_retention_convglu_step'


def _rms(x):
    xf = x.astype(jnp.float32)
    return xf * lax.rsqrt(jnp.mean(xf * xf, axis=-1, keepdims=True) + NORM_EPS)


def _rmsnorm(x, g):
    return _rms(x) * g.astype(jnp.float32)


def _heads(x, h):
    return x.reshape(*x.shape[:-1], h, x.shape[-1] // h)


def _chunk_len(t):
    return CHUNK if t % CHUNK == 0 else t


def _to_chunks(x, c):
    b, t = x.shape[:2]
    return jnp.moveaxis(x.reshape(b, t // c, c, *x.shape[2:]), 1, 0)


def _from_chunks(x):
    x = jnp.moveaxis(x, 0, 1)
    return x.reshape(x.shape[0], x.shape[1] * x.shape[2], *x.shape[3:])


def _rope(x, pos):
    half = x.shape[-1] // 2
    inv = ROPE_BASE ** (-jnp.arange(half, dtype=jnp.float32) / half)
    ang = pos[:, None] * inv[None, :]
    cos = jnp.cos(ang)[None, :, None, :]
    sin = jnp.sin(ang)[None, :, None, :]
    x1, x2 = x[..., :half], x[..., half:]
    return jnp.concatenate([x1 * cos - x2 * sin, x1 * sin + x2 * cos], axis=-1)


def _hgrn2_chunked(q, log_f, k, v, s0):
    c = _chunk_len(q.shape[1])
    causal = jnp.tril(jnp.ones((c, c), dtype=bool))[None, :, :, None, None]

    def step(s, inp):
        qc, gc, kc, vc = inp
        b = jnp.cumsum(gc, axis=1)
        diff = jnp.where(causal, b[:, :, None] - b[:, None, :], 0.0)
        dec = jnp.where(causal, jnp.exp(diff), 0.0)
        scores = jnp.einsum('btshk,bthk->bhts', dec * kc[:, None], qc)
        o = (jnp.einsum('bthk,bhkv->bthv', qc * jnp.exp(b), s)
             + jnp.einsum('bhts,bshv->bthv', scores, vc))
        b_last = b[:, -1]
        s = (jnp.exp(b_last)[..., None] * s
             + jnp.einsum('bshk,bshv->bhkv', kc * jnp.exp(b_last[:, None] - b), vc))
        return s, o

    xs = (_to_chunks(q, c), _to_chunks(log_f, c), _to_chunks(k, c), _to_chunks(v, c))
    s, o = lax.scan(step, s0, xs)
    return _from_chunks(o), s


def _retention_chunked(q, k, v, s0):
    h = q.shape[2]
    c = _chunk_len(q.shape[1])
    log_g = jnp.log1p(-jnp.exp2(-5.0 - jnp.arange(h, dtype=jnp.float32)))
    idx = jnp.arange(c, dtype=jnp.float32)
    rel = idx[:, None] - idx[None, :]
    dmat = jnp.where(rel[None] >= 0, jnp.exp(log_g[:, None, None] * jnp.maximum(rel, 0.0)[None]), 0.0)
    inner = jnp.exp(log_g[None, :] * (idx[:, None] + 1.0))
    tail = jnp.exp(log_g[None, :] * (c - 1.0 - idx[:, None]))
    total = jnp.exp(log_g * c)

    def step(s, inp):
        qc, kc, vc = inp
        scores = jnp.einsum('bthk,bshk->bhts', qc, kc) * dmat
        o = (jnp.einsum('bhts,bshv->bthv', scores, vc)
             + jnp.einsum('bthk,bhkv->bthv', qc * inner[None, :, :, None], s))
        s = (total[None, :, None, None] * s
             + jnp.einsum('bshk,bshv->bhkv', kc * tail[None, :, :, None], vc))
        return s, o

    xs = (_to_chunks(q, c), _to_chunks(k, c), _to_chunks(v, c))
    s, o = lax.scan(step, s0, xs)
    return _from_chunks(o), s


def _rwkv7_scan(r, w, k, v, kk, kka, s0):
    def step(s, inp):
        rt, wt, kt, vt, kkt, kat = inp
        sa = jnp.einsum('bhvk,bhk->bhv', s, -kkt)
        s = s * wt[:, :, None, :] + sa[..., None] * kat[:, :, None, :] + vt[..., None] * kt[:, :, None, :]
        return s, jnp.einsum('bhvk,bhk->bhv', s, rt)

    xs = tuple(jnp.moveaxis(t, 1, 0) for t in (r, w, k, v, kk, kka))
    s, y = lax.scan(step, s0, xs)
    return jnp.moveaxis(y, 0, 1), s


def _group_norm(y, w, b):
    mu = jnp.mean(y, axis=-1, keepdims=True)
    var = jnp.mean(jnp.square(y - mu), axis=-1, keepdims=True)
    yn = (y - mu) * lax.rsqrt(var + RWKV_GN_EPS)
    return yn * _heads(w.astype(jnp.float32), B_HEADS) + _heads(b.astype(jnp.float32), B_HEADS)


def _layer(x, t0, s_a, s_b, s_shift, s_c, s_conv, lb, p):
    f32 = jnp.float32
    bn, t, _ = x.shape
    h = _rmsnorm(x, p['pre_mix_g'])
    z = h @ p['w_in']
    za, zb, zc, zg = jnp.split(z, [A_COLS, A_COLS + B_COLS, A_COLS + B_COLS + C_COLS], axis=-1)

    qa, fa, ia, ga = jnp.split(za, [A_QK, 2 * A_QK, 2 * A_QK + A_WIDTH], axis=-1)
    qa = jax.nn.silu(qa.astype(f32))
    fa = fa.astype(f32)
    lbf = lb.astype(f32)
    f_gate = lbf + (1.0 - lbf) * jax.nn.sigmoid(fa)
    log_f = jnp.log(jnp.maximum(f_gate, F_TINY))
    ka = (1.0 - lbf) * jax.nn.sigmoid(-fa)
    o_a, s_a = _hgrn2_chunked(_heads(qa, A_HEADS), _heads(log_f, A_HEADS), _heads(ka, A_HEADS),
                              _heads(ia.astype(f32), A_HEADS), s_a.astype(f32))
    o_a = _rmsnorm(o_a, _heads(p['a_norm_g'], A_HEADS)).reshape(bn, t, A_WIDTH)
    o_a = o_a * jax.nn.silu(ga.astype(f32))

    zprev = jnp.concatenate([s_shift[:, None].astype(zb.dtype), zb[:, :-1]], axis=1)
    new_shift = zb[:, -1]
    zm = (zb + p['rwkv_mu'] * (zprev - zb)).astype(f32)
    o1 = 3 * B_WIDTH
    rb, kb, vb, wd, ad, gd = jnp.split(
        zm, [B_WIDTH, 2 * B_WIDTH, o1, o1 + B_LORA_W, o1 + B_LORA_W + B_LORA_A], axis=-1)
    w_raw = -jax.nn.softplus(-(p['rwkv_w0'] + jnp.tanh(wd) @ p['rwkv_w2'])) - 0.5
    decay = jnp.exp(-jnp.exp(w_raw))
    aa = jax.nn.sigmoid(p['rwkv_a0'] + ad @ p['rwkv_a2'])
    gb = jax.nn.sigmoid(gd) @ p['rwkv_g2']
    kk = _heads(kb * p['rwkv_kk'], B_HEADS)
    kk = kk / jnp.maximum(jnp.sqrt(jnp.sum(kk * kk, axis=-1, keepdims=True)), 1e-12)
    kb = kb * (1.0 + (aa - 1.0) * p['rwkv_ka'])
    rh, kh, vh = _heads(rb, B_HEADS), _heads(kb, B_HEADS), _heads(vb, B_HEADS)
    ah = _heads(aa, B_HEADS)
    y_b, s_b = _rwkv7_scan(rh, _heads(decay, B_HEADS), kh, vh, kk, kk * ah, s_b.astype(f32))
    y_b = _group_norm(y_b, p['rwkv_gn_w'], p['rwkv_gn_b'])
    bonus = jnp.sum(rh * kh * _heads(p['rwkv_rk'].astype(f32), B_HEADS), axis=-1, keepdims=True)
    y_b = y_b + bonus * vh
    o_b = y_b.reshape(bn, t, B_WIDTH) * gb

    qc, kc, vc, gc = jnp.split(zc, [C_QK, 2 * C_QK, 2 * C_QK + C_WIDTH], axis=-1)
    pos = t0 + jnp.arange(t, dtype=f32)
    qh = _rope(_heads(qc.astype(f32), C_HEADS), pos)
    kh_c = _rope(_heads(kc.astype(f32), C_HEADS), pos) * (C_DK ** -0.5)
    o_c, s_c = _retention_chunked(qh, kh_c, _heads(vc.astype(f32), C_HEADS), s_c.astype(f32))
    o_c = _rms(o_c).reshape(bn, t, C_WIDTH) * jax.nn.silu(gc.astype(f32))

    gates = jax.nn.sigmoid(zg.astype(f32)).reshape(bn, t, N_BRANCH, D_MODEL)
    merged = (gates[:, :, 0] * (o_a @ p['w_br_a'])
              + gates[:, :, 1] * (o_b @ p['w_br_b'])
              + gates[:, :, 2] * (o_c @ p['w_br_c']))
    mix = merged @ p['w_out']
    x = x + _rmsnorm(mix, p['post_mix_g']).astype(x.dtype)

    h = _rmsnorm(x, p['pre_ffn_g'])
    u = h @ p['w_up']
    ua, ub = jnp.split(u, 2, axis=-1)
    ext = jnp.concatenate([s_conv.astype(ua.dtype), ua], axis=1)
    conv = p['conv_b'] + sum(p['conv_w'][j] * ext[:, j:j + t] for j in range(CONV_W))
    new_conv = ext[:, t:]
    ffn = (jax.nn.gelu(conv.astype(f32)) * ub.astype(f32)) @ p['w_down']
    x = x + _rmsnorm(ffn, p['post_ffn_g']).astype(x.dtype)
    return x, s_a, s_b, new_shift, s_c, new_conv


def setup_inputs(seed: int = 0) -> dict:
    key = jax.random.key(seed)
    ks = iter(jax.random.split(key, 48))

    def nrm(shape, scale):
        return scale * jax.random.normal(next(ks), shape, jnp.float32)

    def gain(shape):
        return 1.0 + nrm(shape, 0.05)

    L = DEPTH
    return {
        'x_prompt': nrm((BATCH, SEQ, D_MODEL), 1.0),
        'x_sample': nrm((DEC_BATCH, DEC_SEQ, D_MODEL), 1.0),
        'state_hgrn': nrm((L, DEC_BATCH, A_HEADS, A_DK, A_DV), 0.3),
        'state_rwkv': nrm((L, DEC_BATCH, B_HEADS, B_HEAD, B_HEAD), 0.3),
        'state_rwkv_shift': nrm((L, DEC_BATCH, B_COLS), 1.0),
        'state_ret': nrm((L, DEC_BATCH, C_HEADS, C_DK, C_DV), 0.3),
        'state_conv': nrm((L, DEC_BATCH, CONV_W - 1, D_FF), 1.0),
        'lb_logits': nrm((L, A_QK), 0.5),
        'pre_mix_g': gain((L, D_MODEL)),
        'w_in': nrm((L, D_MODEL, P_COLS), D_MODEL ** -0.5),
        'a_norm_g': gain((L, A_WIDTH)),
        'rwkv_mu': jax.random.uniform(next(ks), (L, B_COLS), jnp.float32),
        'rwkv_w0': nrm((L, B_WIDTH), 0.5) - 1.0,
        'rwkv_w2': nrm((L, B_LORA_W, B_WIDTH), 0.1 * B_LORA_W ** -0.5),
        'rwkv_a0': nrm((L, B_WIDTH), 0.1),
        'rwkv_a2': nrm((L, B_LORA_A, B_WIDTH), B_LORA_A ** -0.5),
        'rwkv_g2': nrm((L, B_LORA_G, B_WIDTH), B_LORA_G ** -0.5),
        'rwkv_kk': 0.85 + nrm((L, B_WIDTH), 0.05),
        'rwkv_ka': 1.0 + nrm((L, B_WIDTH), 0.05),
        'rwkv_rk': nrm((L, B_WIDTH), 0.1),
        'rwkv_gn_w': gain((L, B_WIDTH)),
        'rwkv_gn_b': nrm((L, B_WIDTH), 0.02),
        'w_br_a': nrm((L, A_WIDTH, D_MODEL), A_WIDTH ** -0.5),
        'w_br_b': nrm((L, B_WIDTH, D_MODEL), B_WIDTH ** -0.5),
        'w_br_c': nrm((L, C_WIDTH, D_MODEL), C_WIDTH ** -0.5),
        'w_out': nrm((L, D_MODEL, D_MODEL), D_MODEL ** -0.5),
        'post_mix_g': gain((L, D_MODEL)),
        'pre_ffn_g': gain((L, D_MODEL)),
        'w_up': nrm((L, D_MODEL, 2 * D_FF), D_MODEL ** -0.5),
        'conv_w': nrm((L, CONV_W, D_FF), CONV_W ** -0.5),
        'conv_b': nrm((L, D_FF), 0.02),
        'w_down': nrm((L, D_FF, D_MODEL), D_FF ** -0.5),
        'post_ffn_g': gain((L, D_MODEL)),
    }


def reference(x_prompt, x_sample, state_hgrn, state_rwkv, state_rwkv_shift, state_ret, state_conv,
              lb_logits, pre_mix_g, w_in, a_norm_g, rwkv_mu, rwkv_w0, rwkv_w2, rwkv_a0, rwkv_a2,
              rwkv_g2, rwkv_kk, rwkv_ka, rwkv_rk, rwkv_gn_w, rwkv_gn_b, w_br_a, w_br_b, w_br_c,
              w_out, post_mix_g, pre_ffn_g, w_up, conv_w, conv_b, w_down, post_ffn_g):
    f32 = jnp.float32
    lb_soft = jax.nn.softmax(lb_logits.astype(f32), axis=0)
    lbs = jnp.cumsum(lb_soft, axis=0) - lb_soft[0]

    def layer_params(l):
        return {
            'pre_mix_g': pre_mix_g[l], 'w_in': w_in[l], 'a_norm_g': a_norm_g[l],
            'rwkv_mu': rwkv_mu[l], 'rwkv_w0': rwkv_w0[l], 'rwkv_w2': rwkv_w2[l],
            'rwkv_a0': rwkv_a0[l], 'rwkv_a2': rwkv_a2[l], 'rwkv_g2': rwkv_g2[l],
            'rwkv_kk': rwkv_kk[l], 'rwkv_ka': rwkv_ka[l], 'rwkv_rk': rwkv_rk[l],
            'rwkv_gn_w': rwkv_gn_w[l], 'rwkv_gn_b': rwkv_gn_b[l],
            'w_br_a': w_br_a[l], 'w_br_b': w_br_b[l], 'w_br_c': w_br_c[l], 'w_out': w_out[l],
            'post_mix_g': post_mix_g[l], 'pre_ffn_g': pre_ffn_g[l], 'w_up': w_up[l],
            'conv_w': conv_w[l], 'conv_b': conv_b[l], 'w_down': w_down[l],
            'post_ffn_g': post_ffn_g[l],
        }

    y_p, y_s = x_prompt, x_sample
    acc_p = ([], [], [], [], [])
    acc_s = ([], [], [], [], [])
    for l in range(DEPTH):
        lp = layer_params(l)
        y_p, pa, pb, psh, pc, pcv = _layer(
            y_p, 0,
            jnp.zeros((BATCH, A_HEADS, A_DK, A_DV), f32),
            jnp.zeros((BATCH, B_HEADS, B_HEAD, B_HEAD), f32),
            jnp.zeros((BATCH, B_COLS), x_prompt.dtype),
            jnp.zeros((BATCH, C_HEADS, C_DK, C_DV), f32),
            jnp.zeros((BATCH, CONV_W - 1, D_FF), x_prompt.dtype),
            lbs[l], lp)
        for acc, s in zip(acc_p, (pa, pb, psh, pc, pcv)):
            acc.append(s)
        y_s, sa, sb, ssh, sc, scv = _layer(
            y_s, PAST_LEN, state_hgrn[l], state_rwkv[l], state_rwkv_shift[l], state_ret[l],
            state_conv[l], lbs[l], lp)
        for acc, s in zip(acc_s, (sa, sb, ssh, sc, scv)):
            acc.append(s)

    p_hgrn, p_rwkv, p_rwkv_shift, p_ret, p_conv = [jnp.stack(a) for a in acc_p]
    s_hgrn, s_rwkv, s_rwkv_shift, s_ret, s_conv = [jnp.stack(a) for a in acc_s]
    return (y_p, y_s, p_hgrn, p_rwkv, p_rwkv_shift, p_ret, p_conv,
            s_hgrn, s_rwkv, s_rwkv_shift, s_ret, s_conv)
```

```python
import functools
import math

import jax
import jax.numpy as jnp
import numpy as np
from jax import lax
from jax.experimental import pallas as pl
from jax.experimental.pallas import tpu as pltpu

F32 = jnp.float32
BF16 = jnp.bfloat16

D_MODEL = 2048
A_HEADS, A_DK, A_DV = 8, 128, 128
A_QK = A_HEADS * A_DK
A_WIDTH = A_HEADS * A_DV
F_TINY = 1e-30
B_HEAD = 64
B_WIDTH = 1024
B_HEADS = B_WIDTH // B_HEAD
B_LORA_W, B_LORA_A, B_LORA_G = 64, 64, 128
RWKV_GN_EPS = 64e-5
C_HEADS, C_DK, C_DV = 4, 128, 256
C_QK = C_HEADS * C_DK
C_WIDTH = C_HEADS * C_DV
ROPE_BASE = 10000.0
A_COLS = 2 * A_QK + 2 * A_WIDTH
B_COLS = 3 * B_WIDTH + B_LORA_W + B_LORA_A + B_LORA_G
C_COLS = 2 * C_QK + 2 * C_WIDTH
N_BRANCH = 3
P_COLS = A_COLS + B_COLS + C_COLS + N_BRANCH * D_MODEL
B_OFF = A_COLS
C_OFF = A_COLS + B_COLS
G_OFF = A_COLS + B_COLS + C_COLS
D_FF = 5632
CONV_W = 3
NORM_EPS = 1e-6

LANES = 128
SUBLANES = 8
MIX_ROWS = 128
RWKV_ROWS = 64
VMEM_LIMIT = 56 * 1024 * 1024


def _cparams(sem):
    return pltpu.CompilerParams(dimension_semantics=sem, vmem_limit_bytes=VMEM_LIMIT)


def _dot(a, b):
    return jnp.dot(a.astype(BF16), b.astype(BF16), preferred_element_type=F32)


def _dot_nt(a, b):
    return lax.dot_general(a.astype(BF16), b.astype(BF16), (((1,), (1,)), ((), ())),
                           preferred_element_type=F32)


def _dot_tn(a, b):
    return lax.dot_general(a.astype(BF16), b.astype(BF16), (((0,), (0,)), ((), ())),
                           preferred_element_type=F32)


def _split(x):
    hi = x.astype(BF16)
    lo = (x - hi.astype(F32)).astype(BF16)
    return hi, lo


def _sel_dot(m, x):
    hi, lo = _split(x)
    return (jnp.dot(m, hi, preferred_element_type=F32)
            + jnp.dot(m, lo, preferred_element_type=F32))


def _dot_sel(x, m):
    hi, lo = _split(x)
    return (jnp.dot(hi, m, preferred_element_type=F32)
            + jnp.dot(lo, m, preferred_element_type=F32))


def _dot3(a, b):
    ah, al = _split(a)
    bh, bl = _split(b)
    return (jnp.dot(ah, bh, preferred_element_type=F32)
            + jnp.dot(ah, bl, preferred_element_type=F32)
            + jnp.dot(al, bh, preferred_element_type=F32))


def _sigmoid(x):
    return jax.nn.sigmoid(x)


def _silu(x):
    return x * jax.nn.sigmoid(x)


def _pick(n, cands):
    for c in cands:
        if n % c == 0:
            return c
    raise ValueError(f"no tile in {cands} divides {n}")


def _rms_matmul_kernel(x_ref, g_ref, w_ref, o_ref, xn_ref, *, tm, sub):
    @pl.when(pl.program_id(1) == 0)
    def _():
        def body(i, carry):
            r = pl.multiple_of(i * sub, sub)
            x = x_ref[pl.ds(r, sub), :]
            ms = jnp.mean(x * x, axis=-1, keepdims=True)
            xn_ref[pl.ds(r, sub), :] = (x * lax.rsqrt(ms + NORM_EPS) * g_ref[...]).astype(BF16)
            return carry
        lax.fori_loop(0, tm // sub, body, 0)

    o_ref[...] = jnp.dot(xn_ref[...], w_ref[...], preferred_element_type=F32)


def _rms_matmul(x, g, w, tn):
    m, k = x.shape
    n = w.shape[1]
    tm = _pick(m, (1024, 512, 256, 128))
    sub = min(tm, 128)
    return pl.pallas_call(
        functools.partial(_rms_matmul_kernel, tm=tm, sub=sub),
        grid=(m // tm, n // tn),
        in_specs=[
            pl.BlockSpec((tm, k), lambda i, j: (i, 0)),
            pl.BlockSpec((1, k), lambda i, j: (0, 0)),
            pl.BlockSpec((k, tn), lambda i, j: (0, j)),
        ],
        out_specs=pl.BlockSpec((tm, tn), lambda i, j: (i, j)),
        out_shape=jax.ShapeDtypeStruct((m, n), F32),
        scratch_shapes=[pltpu.VMEM((tm, k), BF16)],
        compiler_params=_cparams(("parallel", "arbitrary")),
        name="rms_matmul",
    )(x, g.reshape(1, k), w)


def _mix_tiling(b, t, rows):
    if t >= rows:
        assert t % rows == 0
        return 1, rows, t // rows, b, t
    assert rows % t == 0 and b % (rows // t) == 0
    nseq = rows // t
    return nseq, t, 1, b // nseq, rows


@functools.lru_cache(maxsize=None)
def _hgrn_consts(c):
    n = MIX_ROWS
    nlev = int(math.log2(c))
    t = np.arange(n)
    u = np.arange(n)[None, :]
    blk = t // c
    same = blk[:, None] == blk[None, :]
    mats = [same & (u <= t[:, None]), same]
    masks = [np.eye(n, dtype=bool)]
    for lev in range(nlev):
        h = 1 << lev
        base = (t // (2 * h)) * (2 * h)
        mid = base + h
        upper = t >= mid
        e_up = (u >= mid[:, None]) & (u <= t[:, None])
        e_lo = (u >= t[:, None] + 1) & (u <= mid[:, None] - 1)
        mats.append(np.where(upper[:, None], e_up, e_lo))
        masks.append((base[:, None] == base[None, :]) & upper[:, None] & (~upper)[None, :])
    sel = np.concatenate(mats, 0).astype(np.float32)
    msk = np.stack(masks).astype(np.float32)
    return sel, msk, nlev


def _hgrn_kernel(*refs, chunks, nseq, c, nlev, has_state):
    if has_state:
        zq, zf, zi, zg, lb, gn, sel, msk, s0, o_ref, s_out, s_ref = refs
        s_ref[...] = s0[:, 0]
    else:
        zq, zf, zi, zg, lb, gn, sel, msk, o_ref, s_out, s_ref = refs
        s_ref[...] = jnp.zeros_like(s_ref)
    n = MIX_ROWS
    lbv = lb[...]

    def chunk(ci, carry):
        r = pl.multiple_of(ci * n, n)
        xq = zq[pl.ds(r, n), :]
        fa = zf[pl.ds(r, n), :]
        v = zi[pl.ds(r, n), :]
        xg = zg[pl.ds(r, n), :]
        q = _silu(xq)
        f_gate = lbv + (1.0 - lbv) * _sigmoid(fa)
        g = jnp.log(jnp.maximum(f_gate, F_TINY))
        k = (1.0 - lbv) * _sigmoid(-fa)
        e = _sel_dot(sel[...], g)
        b = e[0:n]
        bl = e[n:2 * n]
        scores = msk[0] * _dot_nt(q, k)
        for lev in range(nlev):
            x = jnp.exp(e[(lev + 2) * n:(lev + 3) * n])
            scores = scores + msk[lev + 1] * _dot_nt(q * x, k * x)
        o = _dot(scores, v)
        qe = q * jnp.exp(b)
        kt = k * jnp.exp(bl - b)
        dt = jnp.exp(bl).T
        outs = []
        for s in range(nseq):
            rows = slice(s * c, (s + 1) * c)
            ss = s_ref[s]
            outs.append(_dot(qe[rows], ss))
            dcol = jnp.broadcast_to(dt[:, s * c:s * c + 1], (A_DK, A_DV))
            s_ref[s] = ss * dcol + _dot_tn(kt[rows], v[rows])
        o = o + (outs[0] if nseq == 1 else jnp.concatenate(outs, axis=0))
        on = o * lax.rsqrt(jnp.mean(o * o, axis=-1, keepdims=True) + NORM_EPS) * gn[...]
        o_ref[pl.ds(r, n), :] = (on * _silu(xg)).astype(BF16)
        return carry

    lax.fori_loop(0, chunks, chunk, 0)
    s_out[:, 0] = s_ref[...]


def _hgrn(z, lb, gn, state, b, t, row_off):
    nseq, c, chunks, steps, rows = _mix_tiling(b, t, MIX_ROWS)
    sel, msk, nlev = _hgrn_consts(c)
    off = row_off // rows
    assert row_off % rows == 0
    qk_blocks = A_QK // LANES

    def zspec(col_blk):
        return pl.BlockSpec((rows, LANES), lambda i, h: (i + off, col_blk + h))

    in_specs = [
        zspec(0), zspec(qk_blocks), zspec(2 * qk_blocks), zspec(3 * qk_blocks),
        pl.BlockSpec((1, LANES), lambda i, h: (0, h)),
        pl.BlockSpec((1, LANES), lambda i, h: (0, h)),
        pl.BlockSpec(sel.shape, lambda i, h: (0, 0)),
        pl.BlockSpec(msk.shape, lambda i, h: (0, 0, 0)),
    ]
    args = [z, z, z, z, lb.reshape(1, A_QK), gn.reshape(1, A_WIDTH),
            jnp.asarray(sel, BF16), jnp.asarray(msk, F32)]
    has_state = state is not None
    if has_state:
        in_specs.append(pl.BlockSpec((nseq, 1, A_DK, A_DV), lambda i, h: (i, h, 0, 0)))
        args.append(state)
    o, s = pl.pallas_call(
        functools.partial(_hgrn_kernel, chunks=chunks, nseq=nseq, c=c, nlev=nlev, has_state=has_state),
        grid=(steps, A_HEADS),
        in_specs=in_specs,
        out_specs=[
            pl.BlockSpec((rows, LANES), lambda i, h: (i, h)),
            pl.BlockSpec((nseq, 1, A_DK, A_DV), lambda i, h: (i, h, 0, 0)),
        ],
        out_shape=[
            jax.ShapeDtypeStruct((b * t, A_WIDTH), BF16),
            jax.ShapeDtypeStruct((b, A_HEADS, A_DK, A_DV), F32),
        ],
        scratch_shapes=[pltpu.VMEM((nseq, A_DK, A_DV), F32)],
        compiler_params=_cparams(("parallel", "parallel")),
        name="hgrn2",
    )(*args)
    return o, s


def _ret_tables(c):
    n = MIX_ROWS
    log_g = jnp.log1p(-jnp.exp2(-5.0 - jnp.arange(C_HEADS, dtype=F32)))
    t = np.arange(n)
    tt = (t % c).astype(np.float32)
    blk = t // c
    rel = tt[:, None] - tt[None, :]
    same = (blk[:, None] == blk[None, :]) & (rel >= 0)
    dmat = jnp.where(same[None], jnp.exp(log_g[:, None, None] * np.maximum(rel, 0.0)[None]), 0.0)
    inner = jnp.exp(log_g[:, None] * (tt[None, :] + 1.0))
    tail = jnp.exp(log_g[:, None] * (c - 1.0 - tt[None, :]))
    total = jnp.exp(log_g * c)
    shape = (C_HEADS, n, n)
    tab = jnp.stack([dmat, jnp.broadcast_to(inner[:, :, None], shape),
                     jnp.broadcast_to(tail[:, :, None], shape)], axis=1)
    tot = jnp.broadcast_to(total[:, None, None], (C_HEADS, 1, C_DV))
    return tab.astype(F32), tot.astype(F32)


def _rope_tables(t0, t, reps):
    half = C_DK // 2
    inv = ROPE_BASE ** (-jnp.arange(half, dtype=F32) / half)
    pos = t0 + jnp.arange(t, dtype=F32)
    ang = pos[:, None] * inv[None, :]
    cos, sin = jnp.cos(ang), jnp.sin(ang)
    cosf = jnp.concatenate([cos, cos], axis=-1)
    sinf = jnp.concatenate([-sin, sin], axis=-1)
    return jnp.tile(cosf, (reps, 1)), jnp.tile(sinf, (reps, 1))


def _ret_kernel(*refs, chunks, nseq, c, has_state):
    if has_state:
        zq, zk, zv, zg, cos, sin, tab, tot, s0, o_ref, s_out, s_ref = refs
        s_ref[...] = s0[:, 0]
    else:
        zq, zk, zv, zg, cos, sin, tab, tot, o_ref, s_out, s_ref = refs
        s_ref[...] = jnp.zeros_like(s_ref)
    n = MIX_ROWS
    half = C_DK // 2

    def chunk(ci, carry):
        r = pl.multiple_of(ci * n, n)
        cs = cos[pl.ds(r, n), :]
        sn = sin[pl.ds(r, n), :]
        xq = zq[pl.ds(r, n), :]
        xk = zk[pl.ds(r, n), :]
        v = zv[pl.ds(r, n), :]
        xg = zg[pl.ds(r, n), :]
        q = xq * cs + pltpu.roll(xq, half, axis=1) * sn
        k = (xk * cs + pltpu.roll(xk, half, axis=1) * sn) * (C_DK ** -0.5)
        scores = _dot_nt(q, k) * tab[0, 0]
        o = _dot(scores, v)
        qi = q * tab[0, 1]
        ktl = k * tab[0, 2]
        outs = []
        for s in range(nseq):
            rows = slice(s * c, (s + 1) * c)
            ss = s_ref[s]
            outs.append(_dot(qi[rows], ss))
            s_ref[s] = tot[0] * ss + _dot_tn(ktl[rows], v[rows])
        o = o + (outs[0] if nseq == 1 else jnp.concatenate(outs, axis=0))
        on = o * lax.rsqrt(jnp.mean(o * o, axis=-1, keepdims=True) + NORM_EPS)
        o_ref[pl.ds(r, n), :] = (on * _silu(xg)).astype(BF16)
        return carry

    lax.fori_loop(0, chunks, chunk, 0)
    s_out[:, 0] = s_ref[...]


def _retention(z, state, b, t, t0, row_off):
    nseq, c, chunks, steps, rows = _mix_tiling(b, t, MIX_ROWS)
    off = row_off // rows
    assert row_off % rows == 0
    tab, tot = _ret_tables(c)
    cosf, sinf = _rope_tables(t0, t, rows // t)
    qb = C_OFF // C_DK
    vb = (C_OFF + 2 * C_QK) // C_DV
    gb = (C_OFF + 2 * C_QK + C_WIDTH) // C_DV
    in_specs = [
        pl.BlockSpec((rows, C_DK), lambda i, h: (i + off, qb + h)),
        pl.BlockSpec((rows, C_DK), lambda i, h: (i + off, qb + C_HEADS + h)),
        pl.BlockSpec((rows, C_DV), lambda i, h: (i + off, vb + h)),
        pl.BlockSpec((rows, C_DV), lambda i, h: (i + off, gb + h)),
        pl.BlockSpec((rows, C_DK), lambda i, h: (0, 0)),
        pl.BlockSpec((rows, C_DK), lambda i, h: (0, 0)),
        pl.BlockSpec((1, 3, MIX_ROWS, MIX_ROWS), lambda i, h: (h, 0, 0, 0)),
        pl.BlockSpec((1, 1, C_DV), lambda i, h: (h, 0, 0)),
    ]
    args = [z, z, z, z, cosf, sinf, tab, tot]
    has_state = state is not None
    if has_state:
        in_specs.append(pl.BlockSpec((nseq, 1, C_DK, C_DV), lambda i, h: (i, h, 0, 0)))
        args.append(state)
    o, s = pl.pallas_call(
        functools.partial(_ret_kernel, chunks=chunks, nseq=nseq, c=c, has_state=has_state),
        grid=(steps, C_HEADS),
        in_specs=in_specs,
        out_specs=[
            pl.BlockSpec((rows, C_DV), lambda i, h: (i, h)),
            pl.BlockSpec((nseq, 1, C_DK, C_DV), lambda i, h: (i, h, 0, 0)),
        ],
        out_shape=[
            jax.ShapeDtypeStruct((b * t, C_WIDTH), BF16),
            jax.ShapeDtypeStruct((b, C_HEADS, C_DK, C_DV), F32),
        ],
        scratch_shapes=[pltpu.VMEM((nseq, C_DK, C_DV), F32)],
        compiler_params=_cparams(("parallel", "parallel")),
        name="retention",
    )(*args)
    return o, s


@functools.lru_cache(maxsize=None)
def _rwkv_consts(c):
    w = RWKV_ROWS
    t = np.arange(w)
    blk = t // c
    same = blk[:, None] == blk[None, :]
    tri = same & (t[None, :] <= t[:, None])
    cum = np.concatenate([tri, same], 0).astype(np.float32)
    rr = np.arange(2 * w)
    grp = rr // c
    tt = rr % c
    sameg = grp[:, None] == grp[None, :]
    strict = sameg & (tt[None, :] < tt[:, None])
    incl = sameg & (tt[None, :] <= tt[:, None])
    masks = np.stack([strict, incl]).astype(np.float32)
    hh = rr // B_HEAD
    gmat = (hh[:, None] == hh[None, :]).astype(np.float32)
    return cum, masks, gmat


def _rwkv_kernel(*refs, chunks, nseq, c, has_state):
    (zr, zk, zv, zl, mu_r, mu_k, mu_v, mu_l, w0, a0, kkp, kap, rkp, gnw, gnb,
     w2, a2, g2, cum, msk, gmat) = refs[:21]
    if has_state:
        sh_r, sh_k, sh_v, sh_l, s0, o_ref, s_out, s_ref = refs[21:]
    else:
        o_ref, s_out, s_ref = refs[21:]
    w = RWKV_ROWS
    n = 2 * w
    nsq = int(math.log2(c)) - 1
    lane = lax.broadcasted_iota(jnp.int32, (w, LANES), 1)
    head0 = lane < B_HEAD
    row = lax.broadcasted_iota(jnp.int32, (w, 1), 0)
    first = (row % c) == 0
    gm = gmat[...]
    gm_f = gm.astype(F32)
    eye = (lax.broadcasted_iota(jnp.int32, (n, n), 0)
           == lax.broadcasted_iota(jnp.int32, (n, n), 1)).astype(F32)

    zero_blk = jnp.zeros((B_HEAD, B_HEAD), F32)
    for s in range(nseq):
        if has_state:
            top = jnp.concatenate([s0[s, 0], zero_blk], axis=1)
            bot = jnp.concatenate([zero_blk, s0[s, 1]], axis=1)
            s_ref[s] = jnp.concatenate([top, bot], axis=0)
        else:
            s_ref[s] = jnp.zeros((n, n), F32)

    def stack(x):
        return jnp.concatenate([jnp.where(head0, x, 0.0), jnp.where(head0, 0.0, x)], axis=0)

    def fold(x):
        return x[0:w] + x[w:n]

    def gsum(x):
        return _dot_sel(x, gm)

    def shifted(ref, sh, mu, ci, r):
        x = ref[pl.ds(r, w), :]
        width = x.shape[1]
        if has_state:
            src = jnp.concatenate(
                [jnp.broadcast_to(sh[s:s + 1, :], (c, width)) for s in range(nseq)], axis=0)
        else:
            rp = pl.multiple_of(jnp.maximum(r - SUBLANES, 0), SUBLANES)
            last = ref[pl.ds(rp, SUBLANES), :][SUBLANES - 1:SUBLANES, :]
            last = jnp.where(ci == 0, 0.0, last)
            src = jnp.broadcast_to(last, (w, width))
        prev = jnp.where(first, src, pltpu.roll(x, 1, axis=0))
        return x + mu[...] * (prev - x)

    def chunk(ci, carry):
        r = pl.multiple_of(ci * w, w)
        xr = shifted(zr, sh_r if has_state else None, mu_r, ci, r)
        xk = shifted(zk, sh_k if has_state else None, mu_k, ci, r)
        xv = shifted(zv, sh_v if has_state else None, mu_v, ci, r)
        xl = shifted(zl, sh_l if has_state else None, mu_l, ci, r)
        wd = xl[:, 0:B_LORA_W]
        ad = xl[:, B_LORA_W:B_LORA_W + B_LORA_A]
        gd = xl[:, B_LORA_W + B_LORA_A:]
        wx = -(w0[...] + _dot(jnp.tanh(wd), w2[...]))
        w_raw = -(jnp.maximum(wx, 0.0) + jnp.log1p(jnp.exp(-jnp.abs(wx)))) - 0.5
        lw = -jnp.exp(w_raw)
        aa = _sigmoid(a0[...] + _dot(ad, a2[...]))
        gb = _dot(_sigmoid(gd), g2[...])
        kk = xk * kkp[...]
        kk = kk / jnp.maximum(jnp.sqrt(gsum(kk * kk)), 1e-12)
        k2 = xk * (1.0 + (aa - 1.0) * kap[...])
        a = -kk
        b = kk * aa
        e = _sel_dot(cum[...], lw)
        lwc = e[0:w]
        lwl = e[w:n]
        dec_in = jnp.exp(lwc)
        dec_ex = jnp.exp(lwc - lw)
        inv = jnp.exp(-lwc)
        rest = jnp.exp(lwl - lwc)
        a_t = a * dec_ex
        r_t = xr * dec_in
        b_t = b * inv
        k_t = k2 * inv
        gram = _dot_nt(jnp.concatenate([stack(a_t), stack(r_t)], axis=0),
                       jnp.concatenate([stack(b_t), stack(k_t)], axis=0))
        m_ab = gram[0:n, 0:n] * msk[0]
        m_ak = gram[0:n, n:2 * n] * msk[0]
        m_rb = gram[n:2 * n, 0:n] * msk[1]
        m_rk = gram[n:2 * n, n:2 * n] * msk[1]
        p = m_ab
        tinv = eye + p
        for _ in range(nsq):
            p = _dot3(p, p)
            tinv = tinv + _dot3(tinv, p)
        p0a, p0r = [], []
        for s in range(nseq):
            rows = slice(s * c, (s + 1) * c)
            pr = _dot_nt(jnp.concatenate([a_t[rows], r_t[rows]], axis=0), s_ref[s])
            p0a.append(pr[0:c])
            p0r.append(pr[c:2 * c])
        p0a = p0a[0] if nseq == 1 else jnp.concatenate(p0a, axis=0)
        p0r = p0r[0] if nseq == 1 else jnp.concatenate(p0r, axis=0)
        vs = stack(xv)
        us = _dot(tinv, stack(p0a) + _dot(m_ak, vs))
        ys = stack(p0r) + _dot(m_rb, us) + _dot(m_rk, vs)
        y = fold(ys)
        u_w = fold(us)
        b_g = b * rest
        k_g = k2 * rest
        dec_l = jnp.exp(lwl)
        for s in range(nseq):
            rows = slice(s * c, (s + 1) * c)
            upd = _dot_tn(jnp.concatenate([u_w[rows], xv[rows]], axis=0),
                          jnp.concatenate([b_g[rows], k_g[rows]], axis=0))
            s_ref[s] = s_ref[s] * dec_l[s * c:s * c + 1, :] + gm_f * upd
        mean = gsum(y) * (1.0 / B_HEAD)
        d = y - mean
        var = gsum(d * d) * (1.0 / B_HEAD)
        yn = d * lax.rsqrt(var + RWKV_GN_EPS) * gnw[...] + gnb[...]
        bonus = gsum(xr * k2 * rkp[...])
        o_ref[pl.ds(r, w), :] = ((yn + bonus * xv) * gb).astype(BF16)
        return carry

    lax.fori_loop(0, chunks, chunk, 0)
    for s in range(nseq):
        ss = s_ref[s]
        s_out[s, 0] = ss[0:B_HEAD, 0:B_HEAD]
        s_out[s, 1] = ss[B_HEAD:n, B_HEAD:n]


def _rwkv(z, p, state, shift, b, t, row_off):
    nseq, c, chunks, steps, rows = _mix_tiling(b, t, RWKV_ROWS)
    off = row_off // rows
    assert row_off % rows == 0
    cum, msk, gmat = _rwkv_consts(c)
    pairs = B_HEADS // 2
    cb = B_OFF // LANES
    wb = B_WIDTH // LANES
    lora_w = B_LORA_W + B_LORA_A + B_LORA_G
    lb_z = (B_OFF + 3 * B_WIDTH) // lora_w
    lb_s = (3 * B_WIDTH) // lora_w

    def vec(x):
        return x.reshape(1, -1)

    def pspec(rows_, col_off=0):
        return pl.BlockSpec((rows_, LANES), lambda i, h: (0, col_off + h))

    in_specs = [
        pl.BlockSpec((rows, LANES), lambda i, h: (i + off, cb + h)),
        pl.BlockSpec((rows, LANES), lambda i, h: (i + off, cb + wb + h)),
        pl.BlockSpec((rows, LANES), lambda i, h: (i + off, cb + 2 * wb + h)),
        pl.BlockSpec((rows, lora_w), lambda i, h: (i + off, lb_z)),
        pspec(1), pspec(1, wb), pspec(1, 2 * wb),
        pl.BlockSpec((1, lora_w), lambda i, h: (0, lb_s)),
        pspec(1), pspec(1), pspec(1), pspec(1), pspec(1), pspec(1), pspec(1),
        pspec(B_LORA_W), pspec(B_LORA_A), pspec(B_LORA_G),
        pl.BlockSpec(cum.shape, lambda i, h: (0, 0)),
        pl.BlockSpec(msk.shape, lambda i, h: (0, 0, 0)),
        pl.BlockSpec(gmat.shape, lambda i, h: (0, 0)),
    ]
    mu = vec(p['rwkv_mu'])
    args = [z, z, z, z, mu, mu, mu, mu,
            vec(p['rwkv_w0']), vec(p['rwkv_a0']), vec(p['rwkv_kk']), vec(p['rwkv_ka']),
            vec(p['rwkv_rk']), vec(p['rwkv_gn_w']), vec(p['rwkv_gn_b']),
            p['rwkv_w2'].astype(BF16), p['rwkv_a2'].astype(BF16), p['rwkv_g2'].astype(BF16),
            jnp.asarray(cum, BF16), jnp.asarray(msk, F32), jnp.asarray(gmat, BF16)]
    has_state = state is not None
    if has_state:
        in_specs += [
            pl.BlockSpec((nseq, LANES), lambda i, h: (i, h)),
            pl.BlockSpec((nseq, LANES), lambda i, h: (i, wb + h)),
            pl.BlockSpec((nseq, LANES), lambda i, h: (i, 2 * wb + h)),
            pl.BlockSpec((nseq, lora_w), lambda i, h: (i, lb_s)),
            pl.BlockSpec((nseq, 2, B_HEAD, B_HEAD), lambda i, h: (i, h, 0, 0)),
        ]
        args += [shift, shift, shift, shift, state]
    o, s = pl.pallas_call(
        functools.partial(_rwkv_kernel, chunks=chunks, nseq=nseq, c=c, has_state=has_state),
        grid=(steps, pairs),
        in_specs=in_specs,
        out_specs=[
            pl.BlockSpec((rows, LANES), lambda i, h: (i, h)),
            pl.BlockSpec((nseq, 2, B_HEAD, B_HEAD), lambda i, h: (i, h, 0, 0)),
        ],
        out_shape=[
            jax.ShapeDtypeStruct((b * t, B_WIDTH), BF16),
            jax.ShapeDtypeStruct((b, B_HEADS, B_HEAD, B_HEAD), F32),
        ],
        scratch_shapes=[pltpu.VMEM((nseq, 2 * B_HEAD, 2 * B_HEAD), F32)],
        compiler_params=_cparams(("parallel", "parallel")),
        name="rwkv7",
    )(*args)
    return o, s


def _merge_kernel(oa, ob, oc, wa, wb, wc, ga, gb, gc, o_ref):
    acc = _sigmoid(ga[...]) * jnp.dot(oa[...], wa[...], preferred_element_type=F32)
    acc = acc + _sigmoid(gb[...]) * jnp.dot(ob[...], wb[...], preferred_element_type=F32)
    acc = acc + _sigmoid(gc[...]) * jnp.dot(oc[...], wc[...], preferred_element_type=F32)
    o_ref[...] = acc.astype(BF16)


def _merge(z, oa, ob, oc, wa, wb, wc):
    m = oa.shape[0]
    tm = _pick(m, (1024, 512, 256, 128))
    tn = 256
    gblk = G_OFF // tn
    nb = D_MODEL // tn
    assert G_OFF % tn == 0

    def ospec():
        return pl.BlockSpec((tm, oa.shape[1]), lambda i, j: (i, 0))

    def wspec():
        return pl.BlockSpec((wa.shape[0], tn), lambda i, j: (0, j))

    def gspec(br):
        return pl.BlockSpec((tm, tn), lambda i, j: (i, gblk + br * nb + j))

    return pl.pallas_call(
        _merge_kernel,
        grid=(m // tm, nb),
        in_specs=[ospec(), ospec(), ospec(), wspec(), wspec(), wspec(), gspec(0), gspec(1), gspec(2)],
        out_specs=pl.BlockSpec((tm, tn), lambda i, j: (i, j)),
        out_shape=jax.ShapeDtypeStruct((m, D_MODEL), BF16),
        compiler_params=_cparams(("parallel", "arbitrary")),
        name="merge",
    )(oa, ob, oc, wa, wb, wc, z, z, z)


def _proj_res_kernel(m_ref, w_ref, g_ref, x_ref, o_ref):
    y = jnp.dot(m_ref[...], w_ref[...], preferred_element_type=F32)
    yn = y * lax.rsqrt(jnp.mean(y * y, axis=-1, keepdims=True) + NORM_EPS) * g_ref[...]
    o_ref[...] = x_ref[...] + yn


def _proj_res(mrg, w, g, x):
    m = x.shape[0]
    tm = _pick(m, (512, 256, 128))
    return pl.pallas_call(
        _proj_res_kernel,
        grid=(m // tm,),
        in_specs=[
            pl.BlockSpec((tm, D_MODEL), lambda i: (i, 0)),
            pl.BlockSpec((D_MODEL, D_MODEL), lambda i: (0, 0)),
            pl.BlockSpec((1, D_MODEL), lambda i: (0, 0)),
            pl.BlockSpec((tm, D_MODEL), lambda i: (i, 0)),
        ],
        out_specs=pl.BlockSpec((tm, D_MODEL), lambda i: (i, 0)),
        out_shape=jax.ShapeDtypeStruct((m, D_MODEL), F32),
        compiler_params=_cparams(("parallel",)),
        name="proj_res",
    )(mrg, w, g.reshape(1, D_MODEL), x)


def _gelu(x):
    return 0.5 * x * (1.0 + jnp.tanh(math.sqrt(2.0 / math.pi) * (x + 0.044715 * (x * x * x))))


def _ffn_down_kernel(*refs, tm, t, blocks_per_seq, has_state):
    if has_state:
        ua, ub, cw, cb, wd, g_ref, x_ref, st, o_ref, acc = refs
    else:
        ua, ub, halo, cw, cb, wd, g_ref, x_ref, o_ref, acc = refs
    kstep = pl.program_id(1)

    @pl.when(kstep == 0)
    def _():
        acc[...] = jnp.zeros_like(acc)

    x = ua[...]
    tk = x.shape[1]
    if has_state:
        ns = tm // t
        x3 = x.reshape(ns, t, tk)
        tt = lax.broadcasted_iota(jnp.int32, (ns, t, tk), 1)
        s_old = st[:, 0:1, :]
        s_new = st[:, 1:2, :]
        prev1 = jnp.where(tt >= 1, pltpu.roll(x3, 1, axis=1), s_new)
        prev2 = jnp.where(tt >= 2, pltpu.roll(x3, 2, axis=1), jnp.where(tt == 1, s_new, s_old))
        prev1 = prev1.reshape(tm, tk)
        prev2 = prev2.reshape(tm, tk)
    else:
        seq_start = (pl.program_id(0) % blocks_per_seq) == 0
        h = jnp.where(seq_start, 0.0, halo[...])
        rr = lax.broadcasted_iota(jnp.int32, (tm, tk), 0)
        h1 = h[SUBLANES - 1:SUBLANES, :]
        h2 = h[SUBLANES - 2:SUBLANES - 1, :]
        prev1 = jnp.where(rr == 0, h1, pltpu.roll(x, 1, axis=0))
        prev2 = jnp.where(rr == 0, h2, jnp.where(rr == 1, h1, pltpu.roll(x, 2, axis=0)))
    conv = cb[...] + cw[0:1, :] * prev2 + cw[1:2, :] * prev1 + cw[2:3, :] * x
    act = _gelu(conv) * ub[...]
    acc[...] += jnp.dot(act.astype(BF16), wd[...], preferred_element_type=F32)

    @pl.when(kstep == pl.num_programs(1) - 1)
    def _():
        y = acc[...]
        yn = y * lax.rsqrt(jnp.mean(y * y, axis=-1, keepdims=True) + NORM_EPS) * g_ref[...]
        o_ref[...] = x_ref[...] + yn


def _ffn_down(u, x1, cw, cb, wd, g, state, b, t, row_off):
    m = b * t
    has_state = state is not None
    tk = 512
    nk = D_FF // tk
    if has_state:
        tm = _pick(m, (512, 256, 128))
        assert tm % t == 0 and t == SUBLANES
        blocks_per_seq = 1
    else:
        tm = _pick(t, (512, 256, 128))
        blocks_per_seq = t // tm
    off = row_off // tm
    assert row_off % tm == 0
    in_specs = [
        pl.BlockSpec((tm, tk), lambda i, k: (i + off, k)),
        pl.BlockSpec((tm, tk), lambda i, k: (i + off, nk + k)),
    ]
    args = [u, u]
    if not has_state:
        hb = tm // SUBLANES
        hoff = row_off // SUBLANES
        in_specs.append(pl.BlockSpec((SUBLANES, tk), lambda i, k: (jnp.maximum(i * hb + hoff - 1, 0), k)))
        args.append(u)
    in_specs += [
        pl.BlockSpec((CONV_W, tk), lambda i, k: (0, k)),
        pl.BlockSpec((1, tk), lambda i, k: (0, k)),
        pl.BlockSpec((tk, D_MODEL), lambda i, k: (k, 0)),
        pl.BlockSpec((1, D_MODEL), lambda i, k: (0, 0)),
        pl.BlockSpec((tm, D_MODEL), lambda i, k: (i + off, 0)),
    ]
    args += [cw, cb.reshape(1, D_FF), wd, g.reshape(1, D_MODEL), x1]
    if has_state:
        in_specs.append(pl.BlockSpec((tm // t, CONV_W - 1, tk), lambda i, k: (i, 0, k)))
        args.append(state)
    return pl.pallas_call(
        functools.partial(_ffn_down_kernel, tm=tm, t=t, blocks_per_seq=blocks_per_seq, has_state=has_state),
        grid=(m // tm, nk),
        in_specs=in_specs,
        out_specs=pl.BlockSpec((tm, D_MODEL), lambda i, k: (i, 0)),
        out_shape=jax.ShapeDtypeStruct((m, D_MODEL), F32),
        scratch_shapes=[pltpu.VMEM((tm, D_MODEL), F32)],
        compiler_params=_cparams(("parallel", "arbitrary")),
        name="ffn_down",
    )(*args)


def _layer(x, groups, lb, p):
    z = _rms_matmul(x, p['pre_mix_g'], p['w_in'], tn=1280)
    oa, ob, oc, states = [], [], [], []
    row = 0
    for (b, t, t0, st) in groups:
        s_a, s_b, s_sh, s_c, s_cv = st if st is not None else (None,) * 5
        o_a, n_a = _hgrn(z, lb, p['a_norm_g'], s_a, b, t, row)
        o_b, n_b = _rwkv(z, p, s_b, s_sh, b, t, row)
        o_c, n_c = _retention(z, s_c, b, t, t0, row)
        n_sh = z[row:row + b * t].reshape(b, t, P_COLS)[:, t - 1, B_OFF:B_OFF + B_COLS]
        oa.append(o_a)
        ob.append(o_b)
        oc.append(o_c)
        states.append([n_a, n_b, n_sh, n_c])
        row += b * t
    oa, ob, oc = (jnp.concatenate(v, axis=0) for v in (oa, ob, oc))
    mrg = _merge(z, oa, ob, oc, p['w_br_a'], p['w_br_b'], p['w_br_c'])
    x1 = _proj_res(mrg, p['w_out'], p['post_mix_g'], x)
    u = _rms_matmul(x1, p['pre_ffn_g'], p['w_up'], tn=1024)
    outs = []
    row = 0
    for gi, (b, t, t0, st) in enumerate(groups):
        s_cv = st[4] if st is not None else None
        outs.append(_ffn_down(u, x1, p['conv_w'], p['conv_b'], p['w_down'], p['post_ffn_g'],
                              s_cv, b, t, row))
        n_cv = u[row:row + b * t].reshape(b, t, 2 * D_FF)[:, t - (CONV_W - 1):, :D_FF]
        states[gi].append(n_cv)
        row += b * t
    return outs, states


def kernel(x_prompt, x_sample, state_hgrn, state_rwkv, state_rwkv_shift, state_ret, state_conv,
           lb_logits, pre_mix_g, w_in, a_norm_g, rwkv_mu, rwkv_w0, rwkv_w2, rwkv_a0, rwkv_a2,
           rwkv_g2, rwkv_kk, rwkv_ka, rwkv_rk, rwkv_gn_w, rwkv_gn_b, w_br_a, w_br_b, w_br_c,
           w_out, post_mix_g, pre_ffn_g, w_up, conv_w, conv_b, w_down, post_ffn_g):
    depth = w_in.shape[0]
    bp, tp, _ = x_prompt.shape
    bs, ts, _ = x_sample.shape
    past_len = 16384
    lb_soft = jax.nn.softmax(lb_logits.astype(F32), axis=0)
    lbs = jnp.cumsum(lb_soft, axis=0) - lb_soft[0]
    xs = [x_prompt.reshape(bp * tp, D_MODEL), x_sample.reshape(bs * ts, D_MODEL)]
    acc_p = [[] for _ in range(5)]
    acc_s = [[] for _ in range(5)]
    for l in range(depth):
        p = {
            'pre_mix_g': pre_mix_g[l], 'w_in': w_in[l].astype(BF16), 'a_norm_g': a_norm_g[l],
            'rwkv_mu': rwkv_mu[l], 'rwkv_w0': rwkv_w0[l], 'rwkv_w2': rwkv_w2[l],
            'rwkv_a0': rwkv_a0[l], 'rwkv_a2': rwkv_a2[l], 'rwkv_g2': rwkv_g2[l],
            'rwkv_kk': rwkv_kk[l], 'rwkv_ka': rwkv_ka[l], 'rwkv_rk': rwkv_rk[l],
            'rwkv_gn_w': rwkv_gn_w[l], 'rwkv_gn_b': rwkv_gn_b[l],
            'w_br_a': w_br_a[l].astype(BF16), 'w_br_b': w_br_b[l].astype(BF16),
            'w_br_c': w_br_c[l].astype(BF16), 'w_out': w_out[l].astype(BF16),
            'post_mix_g': post_mix_g[l], 'pre_ffn_g': pre_ffn_g[l], 'w_up': w_up[l].astype(BF16),
            'conv_w': conv_w[l], 'conv_b': conv_b[l], 'w_down': w_down[l].astype(BF16),
            'post_ffn_g': post_ffn_g[l],
        }
        groups = [
            (bp, tp, 0, None),
            (bs, ts, past_len, (state_hgrn[l], state_rwkv[l], state_rwkv_shift[l], state_ret[l],
                                state_conv[l])),
        ]
        x = jnp.concatenate(xs, axis=0)
        xs, states = _layer(x, groups, lbs[l], p)
        for acc, s in zip(acc_p, states[0]):
            acc.append(s)
        for acc, s in zip(acc_s, states[1]):
            acc.append(s)
    y_p = xs[0].reshape(bp, tp, D_MODEL)
    y_s = xs[1].reshape(bs, ts, D_MODEL)
    outs_p = [jnp.stack(a) for a in acc_p]
    outs_s = [jnp.stack(a) for a in acc_s]
    return (y_p, y_s, *outs_p, *outs_s)
```

```python
import functools
import itertools
import math

import jax
import jax.numpy as jnp
import numpy as np
from jax import lax
from jax.experimental import pallas as pl
from jax.experimental.pallas import tpu as pltpu

F32 = jnp.float32
BF16 = jnp.bfloat16

D_MODEL = 2048
A_HEADS, A_DK, A_DV = 8, 128, 128
A_QK = A_HEADS * A_DK
A_WIDTH = A_HEADS * A_DV
F_TINY = 1e-30
B_HEAD = 64
B_WIDTH = 1024
B_HEADS = B_WIDTH // B_HEAD
B_LORA_W, B_LORA_A, B_LORA_G = 64, 64, 128
RWKV_GN_EPS = 64e-5
C_HEADS, C_DK, C_DV = 4, 128, 256
C_QK = C_HEADS * C_DK
C_WIDTH = C_HEADS * C_DV
ROPE_BASE = 10000.0
A_COLS = 2 * A_QK + 2 * A_WIDTH
B_COLS = 3 * B_WIDTH + B_LORA_W + B_LORA_A + B_LORA_G
C_COLS = 2 * C_QK + 2 * C_WIDTH
N_BRANCH = 3
P_COLS = A_COLS + B_COLS + C_COLS + N_BRANCH * D_MODEL
B_OFF = A_COLS
C_OFF = A_COLS + B_COLS
G_OFF = A_COLS + B_COLS + C_COLS
D_FF = 5632
CONV_W = 3
NORM_EPS = 1e-6

LANES = 128
SUBLANES = 8
MIX_ROWS = 128
RWKV_ROWS = 64
MIX_TIME_BLOCK = 512
MIX_STREAMS = 4
VMEM_LIMIT = 56 * 1024 * 1024


def _cparams(sem):
    return pltpu.CompilerParams(dimension_semantics=sem, vmem_limit_bytes=VMEM_LIMIT)


def _dot(a, b):
    return jnp.dot(a.astype(BF16), b.astype(BF16), preferred_element_type=F32)


def _dot_nt(a, b):
    return lax.dot_general(a.astype(BF16), b.astype(BF16), (((1,), (1,)), ((), ())),
                           preferred_element_type=F32)


def _dot_tn(a, b):
    return lax.dot_general(a.astype(BF16), b.astype(BF16), (((0,), (0,)), ((), ())),
                           preferred_element_type=F32)


def _split(x):
    hi = x.astype(BF16)
    lo = (x - hi.astype(F32)).astype(BF16)
    return hi, lo


def _sel_dot(m, x):
    hi, lo = _split(x)
    return (jnp.dot(m, hi, preferred_element_type=F32)
            + jnp.dot(m, lo, preferred_element_type=F32))


def _dot_sel(x, m):
    hi, lo = _split(x)
    return (jnp.dot(hi, m, preferred_element_type=F32)
            + jnp.dot(lo, m, preferred_element_type=F32))


def _dot3(a, b):
    ah, al = _split(a)
    bh, bl = _split(b)
    return (jnp.dot(ah, bh, preferred_element_type=F32)
            + jnp.dot(ah, bl, preferred_element_type=F32)
            + jnp.dot(al, bh, preferred_element_type=F32))


def _sigmoid(x):
    return jax.nn.sigmoid(x)


def _silu(x):
    return x * jax.nn.sigmoid(x)


def _round_robin(gens):
    for _ in itertools.zip_longest(*gens):
        pass


def _pick(n, cands):
    for c in cands:
        if n % c == 0:
            return c
    raise ValueError(f"no tile in {cands} divides {n}")


def _rms_matmul_kernel(x_ref, g_ref, w_ref, o_ref, xn_ref, *, tm, sub):
    @pl.when(pl.program_id(1) == 0)
    def _():
        def body(i, carry):
            r = pl.multiple_of(i * sub, sub)
            x = x_ref[pl.ds(r, sub), :]
            ms = jnp.mean(x * x, axis=-1, keepdims=True)
            xn_ref[pl.ds(r, sub), :] = (x * lax.rsqrt(ms + NORM_EPS) * g_ref[...]).astype(BF16)
            return carry
        lax.fori_loop(0, tm // sub, body, 0)

    o_ref[...] = jnp.dot(xn_ref[...], w_ref[...], preferred_element_type=F32)


def _rms_matmul(x, g, w, tn):
    m, k = x.shape
    n = w.shape[1]
    tm = _pick(m, (1024, 512, 256, 128))
    sub = min(tm, 128)
    return pl.pallas_call(
        functools.partial(_rms_matmul_kernel, tm=tm, sub=sub),
        grid=(m // tm, n // tn),
        in_specs=[
            pl.BlockSpec((tm, k), lambda i, j: (i, 0)),
            pl.BlockSpec((1, k), lambda i, j: (0, 0)),
            pl.BlockSpec((k, tn), lambda i, j: (0, j)),
        ],
        out_specs=pl.BlockSpec((tm, tn), lambda i, j: (i, j)),
        out_shape=jax.ShapeDtypeStruct((m, n), F32),
        scratch_shapes=[pltpu.VMEM((tm, k), BF16)],
        compiler_params=_cparams(("parallel", "arbitrary")),
        name="rms_matmul",
    )(x, g.reshape(1, k), w)


class _Tiling:
    def __init__(self, b, t, rows, row_off):
        if t >= rows:
            self.nseq, self.c = 1, rows
            self.blk = _pick(t, (MIX_TIME_BLOCK, rows))
            self.nt = t // self.blk
            n_streams = b
        else:
            assert rows % t == 0 and b % (rows // t) == 0
            self.nseq, self.c = rows // t, t
            self.blk, self.nt = rows, 1
            n_streams = b // self.nseq
        self.g = _pick(n_streams, (MIX_STREAMS, 2, 1))
        self.steps = n_streams // self.g
        self.n_streams = n_streams
        self.chunks = self.blk // rows
        assert row_off % self.blk == 0
        self.off = row_off // self.blk

    def zspec(self, stream, width, col_blk, per_head=1):
        g, nt, off = self.g, self.nt, self.off
        return pl.BlockSpec(
            (self.blk, width),
            lambda i, h, tb: (off + (i * g + stream) * nt + tb, col_blk + per_head * h))

    def ospec(self, width):
        return pl.BlockSpec((self.g, self.blk, width), lambda i, h, tb: (i, tb, h))

    def oshape(self, width):
        return jax.ShapeDtypeStruct((self.n_streams, self.nt * self.blk, width), BF16)

    def sspec(self, layer, heads_per_step, d0, d1):
        return pl.BlockSpec((1, self.g * self.nseq, heads_per_step, d0, d1),
                            lambda i, h, tb: (layer, i, h, 0, 0))


def _const_spec(shape):
    nd = len(shape)
    return pl.BlockSpec(shape, lambda i, h, tb: (0,) * nd)


def _state_io(til, state, prev_out, layer, depth, b, heads, heads_per_step, d0, d1,
              in_specs, args):
    has_state = state is not None
    if has_state:
        in_specs.append(til.sspec(layer, heads_per_step, d0, d1))
        args.append(state)
    aliases = {}
    if prev_out is not None:
        aliases = {len(args): 1}
        in_specs.append(pl.BlockSpec(memory_space=pl.ANY))
        args.append(prev_out)
    out_spec = til.sspec(layer, heads_per_step, d0, d1)
    out_shape = jax.ShapeDtypeStruct((depth, b, heads, d0, d1), F32)
    return has_state, prev_out is not None, aliases, out_spec, out_shape


@functools.lru_cache(maxsize=None)
def _hgrn_consts(c):
    n = MIX_ROWS
    nlev = int(math.log2(c))
    t = np.arange(n)
    u = np.arange(n)[None, :]
    blk = t // c
    same = blk[:, None] == blk[None, :]
    mats = [same & (u <= t[:, None]), same]
    masks = [np.eye(n, dtype=bool)]
    for lev in range(nlev):
        h = 1 << lev
        base = (t // (2 * h)) * (2 * h)
        mid = base + h
        upper = t >= mid
        e_up = (u >= mid[:, None]) & (u <= t[:, None])
        e_lo = (u >= t[:, None] + 1) & (u <= mid[:, None] - 1)
        mats.append(np.where(upper[:, None], e_up, e_lo))
        masks.append((base[:, None] == base[None, :]) & upper[:, None] & (~upper)[None, :])
    sel = np.concatenate(mats, 0).astype(np.float32)
    msk = np.stack(masks).astype(np.float32)
    return sel, msk, nlev


def _hgrn_kernel(*refs, g_n, chunks, nseq, c, nlev, has_state, has_alias):
    it = iter(refs)
    zq, zf, zi, zg = ([next(it) for _ in range(g_n)] for _ in range(4))
    lb, gn, sel, msk = (next(it) for _ in range(4))
    s0 = next(it) if has_state else None
    if has_alias:
        next(it)
    o_ref, s_out, s_ref = next(it), next(it), next(it)
    n = MIX_ROWS
    tb = pl.program_id(2)

    @pl.when(tb == 0)
    def _():
        if has_state:
            s_ref[...] = s0[0, :, 0]
        else:
            s_ref[...] = jnp.zeros_like(s_ref)

    lbv = lb[...]

    def tile(g, r):
        xq = zq[g][pl.ds(r, n), :]
        fa = zf[g][pl.ds(r, n), :]
        v = zi[g][pl.ds(r, n), :]
        xg = zg[g][pl.ds(r, n), :]
        q = _silu(xq)
        f_gate = lbv + (1.0 - lbv) * _sigmoid(fa)
        gl = jnp.log(jnp.maximum(f_gate, F_TINY))
        k = (1.0 - lbv) * _sigmoid(-fa)
        e = _sel_dot(sel[...], gl)
        yield
        b = e[0:n]
        bl = e[n:2 * n]
        scores = msk[0] * _dot_nt(q, k)
        for lev in range(nlev):
            x = jnp.exp(e[(lev + 2) * n:(lev + 3) * n])
            scores = scores + msk[lev + 1] * _dot_nt(q * x, k * x)
        yield
        o = _dot(scores, v)
        yield
        qe = q * jnp.exp(b)
        kt = k * jnp.exp(bl - b)
        dt = jnp.exp(bl).T
        outs = []
        for s in range(nseq):
            rows = slice(s * c, (s + 1) * c)
            ss = s_ref[g * nseq + s]
            outs.append(_dot(qe[rows], ss))
            dcol = jnp.broadcast_to(dt[:, s * c:s * c + 1], (A_DK, A_DV))
            s_ref[g * nseq + s] = ss * dcol + _dot_tn(kt[rows], v[rows])
        o = o + (outs[0] if nseq == 1 else jnp.concatenate(outs, axis=0))
        on = o * lax.rsqrt(jnp.mean(o * o, axis=-1, keepdims=True) + NORM_EPS) * gn[...]
        o_ref[g, pl.ds(r, n), :] = (on * _silu(xg)).astype(BF16)

    def chunk(ci, carry):
        r = pl.multiple_of(ci * n, n)
        _round_robin([tile(g, r) for g in range(g_n)])
        return carry

    lax.fori_loop(0, chunks, chunk, 0)

    @pl.when(tb == pl.num_programs(2) - 1)
    def _():
        s_out[0, :, 0] = s_ref[...]


def _hgrn(z, lb, gn, state, prev_out, layer, depth, b, t, row_off):
    til = _Tiling(b, t, MIX_ROWS, row_off)
    sel, msk, nlev = _hgrn_consts(til.c)
    qk_blocks = A_QK // LANES
    in_specs, args = [], []
    for col in range(4):
        for g in range(til.g):
            in_specs.append(til.zspec(g, LANES, col * qk_blocks))
            args.append(z)
    in_specs += [
        pl.BlockSpec((1, LANES), lambda i, h, tb: (0, h)),
        pl.BlockSpec((1, LANES), lambda i, h, tb: (0, h)),
        _const_spec(sel.shape), _const_spec(msk.shape),
    ]
    args += [lb.reshape(1, A_QK), gn.reshape(1, A_WIDTH), jnp.asarray(sel, BF16), jnp.asarray(msk, F32)]
    has_state, has_alias, aliases, s_spec, s_shape = _state_io(
        til, state, prev_out, layer, depth, b, A_HEADS, 1, A_DK, A_DV, in_specs, args)
    o, s = pl.pallas_call(
        functools.partial(_hgrn_kernel, g_n=til.g, chunks=til.chunks, nseq=til.nseq, c=til.c,
                          nlev=nlev, has_state=has_state, has_alias=has_alias),
        grid=(til.steps, A_HEADS, til.nt),
        in_specs=in_specs,
        out_specs=[til.ospec(LANES), s_spec],
        out_shape=[til.oshape(A_WIDTH), s_shape],
        scratch_shapes=[pltpu.VMEM((til.g * til.nseq, A_DK, A_DV), F32)],
        input_output_aliases=aliases,
        compiler_params=_cparams(("parallel", "parallel", "arbitrary")),
        name="hgrn2",
    )(*args)
    return o.reshape(b * t, A_WIDTH), s


def _ret_tables(c):
    n = MIX_ROWS
    log_g = jnp.log1p(-jnp.exp2(-5.0 - jnp.arange(C_HEADS, dtype=F32)))
    t = np.arange(n)
    tt = (t % c).astype(np.float32)
    blk = t // c
    rel = tt[:, None] - tt[None, :]
    same = (blk[:, None] == blk[None, :]) & (rel >= 0)
    dmat = jnp.where(same[None], jnp.exp(log_g[:, None, None] * np.maximum(rel, 0.0)[None]), 0.0)
    inner = jnp.exp(log_g[:, None] * (tt[None, :] + 1.0))
    tail = jnp.exp(log_g[:, None] * (c - 1.0 - tt[None, :]))
    total = jnp.exp(log_g * c)
    shape = (C_HEADS, n, n)
    tab = jnp.stack([dmat, jnp.broadcast_to(inner[:, :, None], shape),
                     jnp.broadcast_to(tail[:, :, None], shape)], axis=1)
    tot = jnp.broadcast_to(total[:, None, None], (C_HEADS, 1, C_DV))
    return tab.astype(F32), tot.astype(F32)


def _rope_tables(t0, t, reps):
    half = C_DK // 2
    inv = ROPE_BASE ** (-jnp.arange(half, dtype=F32) / half)
    pos = t0 + jnp.arange(t, dtype=F32)
    ang = pos[:, None] * inv[None, :]
    cos, sin = jnp.cos(ang), jnp.sin(ang)
    cosf = jnp.concatenate([cos, cos], axis=-1)
    sinf = jnp.concatenate([-sin, sin], axis=-1)
    return jnp.tile(cosf, (reps, 1)), jnp.tile(sinf, (reps, 1))


def _ret_kernel(*refs, g_n, chunks, nseq, c, has_state, has_alias):
    it = iter(refs)
    zq, zk, zv, zg = ([next(it) for _ in range(g_n)] for _ in range(4))
    cos, sin, tab, tot = (next(it) for _ in range(4))
    s0 = next(it) if has_state else None
    if has_alias:
        next(it)
    o_ref, s_out, s_ref = next(it), next(it), next(it)
    n = MIX_ROWS
    half = C_DK // 2
    tb = pl.program_id(2)

    @pl.when(tb == 0)
    def _():
        if has_state:
            s_ref[...] = s0[0, :, 0]
        else:
            s_ref[...] = jnp.zeros_like(s_ref)

    def tile(g, r, cs, sn):
        xq = zq[g][pl.ds(r, n), :]
        xk = zk[g][pl.ds(r, n), :]
        v = zv[g][pl.ds(r, n), :]
        xg = zg[g][pl.ds(r, n), :]
        q = xq * cs + pltpu.roll(xq, half, axis=1) * sn
        k = (xk * cs + pltpu.roll(xk, half, axis=1) * sn) * (C_DK ** -0.5)
        scores = _dot_nt(q, k) * tab[0, 0]
        yield
        o = _dot(scores, v)
        yield
        qi = q * tab[0, 1]
        ktl = k * tab[0, 2]
        outs = []
        for s in range(nseq):
            rows = slice(s * c, (s + 1) * c)
            ss = s_ref[g * nseq + s]
            outs.append(_dot(qi[rows], ss))
            s_ref[g * nseq + s] = tot[0] * ss + _dot_tn(ktl[rows], v[rows])
        o = o + (outs[0] if nseq == 1 else jnp.concatenate(outs, axis=0))
        on = o * lax.rsqrt(jnp.mean(o * o, axis=-1, keepdims=True) + NORM_EPS)
        o_ref[g, pl.ds(r, n), :] = (on * _silu(xg)).astype(BF16)

    def chunk(ci, carry):
        r = pl.multiple_of(ci * n, n)
        cs = cos[pl.ds(r, n), :]
        sn = sin[pl.ds(r, n), :]
        _round_robin([tile(g, r, cs, sn) for g in range(g_n)])
        return carry

    lax.fori_loop(0, chunks, chunk, 0)

    @pl.when(tb == pl.num_programs(2) - 1)
    def _():
        s_out[0, :, 0] = s_ref[...]


def _retention(z, state, prev_out, layer, depth, b, t, t0, row_off):
    til = _Tiling(b, t, MIX_ROWS, row_off)
    tab, tot = _ret_tables(til.c)
    cosf, sinf = _rope_tables(t0, t, max(MIX_ROWS // t, 1))
    qb = C_OFF // C_DK
    vb = (C_OFF + 2 * C_QK) // C_DV
    gb = (C_OFF + 2 * C_QK + C_WIDTH) // C_DV
    in_specs, args = [], []
    for width, col in ((C_DK, qb), (C_DK, qb + C_HEADS), (C_DV, vb), (C_DV, gb)):
        for g in range(til.g):
            in_specs.append(til.zspec(g, width, col))
            args.append(z)
    in_specs += [
        pl.BlockSpec((til.blk, C_DK), lambda i, h, tb: (tb, 0)),
        pl.BlockSpec((til.blk, C_DK), lambda i, h, tb: (tb, 0)),
        pl.BlockSpec((1, 3, MIX_ROWS, MIX_ROWS), lambda i, h, tb: (h, 0, 0, 0)),
        pl.BlockSpec((1, 1, C_DV), lambda i, h, tb: (h, 0, 0)),
    ]
    args += [cosf, sinf, tab, tot]
    has_state, has_alias, aliases, s_spec, s_shape = _state_io(
        til, state, prev_out, layer, depth, b, C_HEADS, 1, C_DK, C_DV, in_specs, args)
    o, s = pl.pallas_call(
        functools.partial(_ret_kernel, g_n=til.g, chunks=til.chunks, nseq=til.nseq, c=til.c,
                          has_state=has_state, has_alias=has_alias),
        grid=(til.steps, C_HEADS, til.nt),
        in_specs=in_specs,
        out_specs=[til.ospec(C_DV), s_spec],
        out_shape=[til.oshape(C_WIDTH), s_shape],
        scratch_shapes=[pltpu.VMEM((til.g * til.nseq, C_DK, C_DV), F32)],
        input_output_aliases=aliases,
        compiler_params=_cparams(("parallel", "parallel", "arbitrary")),
        name="retention",
    )(*args)
    return o.reshape(b * t, C_WIDTH), s


@functools.lru_cache(maxsize=None)
def _rwkv_consts(c):
    w = RWKV_ROWS
    t = np.arange(w)
    blk = t // c
    same = blk[:, None] == blk[None, :]
    tri = same & (t[None, :] <= t[:, None])
    cum = np.concatenate([tri, same], 0).astype(np.float32)
    rr = np.arange(2 * w)
    grp = rr // c
    tt = rr % c
    sameg = grp[:, None] == grp[None, :]
    strict = sameg & (tt[None, :] < tt[:, None])
    incl = sameg & (tt[None, :] <= tt[:, None])
    masks = np.stack([strict, incl]).astype(np.float32)
    hh = rr // B_HEAD
    gmat = (hh[:, None] == hh[None, :]).astype(np.float32)
    return cum, masks, gmat


def _rwkv_kernel(*refs, g_n, chunks, nseq, c, has_state, has_alias):
    it = iter(refs)
    zr, zk, zv, zl = ([next(it) for _ in range(g_n)] for _ in range(4))
    (mu_r, mu_k, mu_v, mu_l, w0, a0, kkp, kap, rkp, gnw, gnb,
     w2, a2, g2, cum, msk, gmat) = (next(it) for _ in range(17))
    if has_state:
        sh_r, sh_k, sh_v, sh_l, s0 = (next(it) for _ in range(5))
    if has_alias:
        next(it)
    o_ref, s_out, s_ref = next(it), next(it), next(it)
    carries = None if has_state else [next(it) for _ in range(4)]
    w = RWKV_ROWS
    n = 2 * w
    blk = chunks * w
    nsq = int(math.log2(c)) - 1
    tb = pl.program_id(2)
    lane = lax.broadcasted_iota(jnp.int32, (w, LANES), 1)
    head0 = lane < B_HEAD
    row = lax.broadcasted_iota(jnp.int32, (w, 1), 0)
    first = (row % c) == 0
    gm = gmat[...]
    gm_f = gm.astype(F32)
    eye = (lax.broadcasted_iota(jnp.int32, (n, n), 0)
           == lax.broadcasted_iota(jnp.int32, (n, n), 1)).astype(F32)

    @pl.when(tb == 0)
    def _():
        if has_state:
            zero_blk = jnp.zeros((B_HEAD, B_HEAD), F32)
            for s in range(g_n * nseq):
                top = jnp.concatenate([s0[0, s, 0], zero_blk], axis=1)
                bot = jnp.concatenate([zero_blk, s0[0, s, 1]], axis=1)
                s_ref[s] = jnp.concatenate([top, bot], axis=0)
        else:
            s_ref[...] = jnp.zeros_like(s_ref)
            for cr in carries:
                cr[...] = jnp.zeros_like(cr)

    def stack(x):
        return jnp.concatenate([jnp.where(head0, x, 0.0), jnp.where(head0, 0.0, x)], axis=0)

    def fold(x):
        return x[0:w] + x[w:n]

    def gsum(x):
        return _dot_sel(x, gm)

    def shifted(refs_g, sh, cr, mu, g, ci, r):
        ref = refs_g[g]
        x = ref[pl.ds(r, w), :]
        width = x.shape[1]
        if has_state:
            src = jnp.concatenate(
                [jnp.broadcast_to(sh[g * nseq + s:g * nseq + s + 1, :], (c, width))
                 for s in range(nseq)], axis=0)
        else:
            rp = pl.multiple_of(jnp.maximum(r - SUBLANES, 0), SUBLANES)
            prev8 = jnp.where(ci == 0, cr[g], ref[pl.ds(rp, SUBLANES), :])
            src = jnp.broadcast_to(prev8[SUBLANES - 1:SUBLANES, :], (w, width))
        prev = jnp.where(first, src, pltpu.roll(x, 1, axis=0))
        return x + mu[...] * (prev - x)

    def tile(g, ci, r):
        cr = carries if carries is not None else [None] * 4
        xr = shifted(zr, sh_r if has_state else None, cr[0], mu_r, g, ci, r)
        xk = shifted(zk, sh_k if has_state else None, cr[1], mu_k, g, ci, r)
        xv = shifted(zv, sh_v if has_state else None, cr[2], mu_v, g, ci, r)
        xl = shifted(zl, sh_l if has_state else None, cr[3], mu_l, g, ci, r)
        wd = xl[:, 0:B_LORA_W]
        ad = xl[:, B_LORA_W:B_LORA_W + B_LORA_A]
        gd = xl[:, B_LORA_W + B_LORA_A:]
        wx = -(w0[...] + _dot(jnp.tanh(wd), w2[...]))
        w_raw = -(jnp.maximum(wx, 0.0) + jnp.log1p(jnp.exp(-jnp.abs(wx)))) - 0.5
        lw = -jnp.exp(w_raw)
        aa = _sigmoid(a0[...] + _dot(ad, a2[...]))
        gb = _dot(_sigmoid(gd), g2[...])
        yield
        kk = xk * kkp[...]
        kk = kk / jnp.maximum(jnp.sqrt(gsum(kk * kk)), 1e-12)
        k2 = xk * (1.0 + (aa - 1.0) * kap[...])
        a = -kk
        b = kk * aa
        yield
        e = _sel_dot(cum[...], lw)
        yield
        lwc = e[0:w]
        lwl = e[w:n]
        dec_in = jnp.exp(lwc)
        dec_ex = jnp.exp(lwc - lw)
        inv = jnp.exp(-lwc)
        rest = jnp.exp(lwl - lwc)
        a_t = a * dec_ex
        r_t = xr * dec_in
        b_t = b * inv
        k_t = k2 * inv
        gram = _dot_nt(jnp.concatenate([stack(a_t), stack(r_t)], axis=0),
                       jnp.concatenate([stack(b_t), stack(k_t)], axis=0))
        yield
        m_ab = gram[0:n, 0:n] * msk[0]
        m_ak = gram[0:n, n:2 * n] * msk[0]
        m_rb = gram[n:2 * n, 0:n] * msk[1]
        m_rk = gram[n:2 * n, n:2 * n] * msk[1]
        p = m_ab
        tinv = eye + p
        for _ in range(nsq):
            p = _dot3(p, p)
            yield
            tinv = tinv + _dot3(tinv, p)
            yield
        p0a, p0r = [], []
        for s in range(nseq):
            rows = slice(s * c, (s + 1) * c)
            pr = _dot_nt(jnp.concatenate([a_t[rows], r_t[rows]], axis=0), s_ref[g * nseq + s])
            p0a.append(pr[0:c])
            p0r.append(pr[c:2 * c])
        p0a = p0a[0] if nseq == 1 else jnp.concatenate(p0a, axis=0)
        p0r = p0r[0] if nseq == 1 else jnp.concatenate(p0r, axis=0)
        yield
        vs = stack(xv)
        rhs = stack(p0a) + _dot(m_ak, vs)
        yield
        us = _dot(tinv, rhs)
        yield
        ys = stack(p0r) + _dot(m_rb, us) + _dot(m_rk, vs)
        yield
        y = fold(ys)
        u_w = fold(us)
        b_g = b * rest
        k_g = k2 * rest
        dec_l = jnp.exp(lwl)
        for s in range(nseq):
            rows = slice(s * c, (s + 1) * c)
            upd = _dot_tn(jnp.concatenate([u_w[rows], xv[rows]], axis=0),
                          jnp.concatenate([b_g[rows], k_g[rows]], axis=0))
            s_ref[g * nseq + s] = s_ref[g * nseq + s] * dec_l[s * c:s * c + 1, :] + gm_f * upd
        yield
        mean = gsum(y) * (1.0 / B_HEAD)
        yield
        d = y - mean
        var = gsum(d * d) * (1.0 / B_HEAD)
        yn = d * lax.rsqrt(var + RWKV_GN_EPS) * gnw[...] + gnb[...]
        yield
        bonus = gsum(xr * k2 * rkp[...])
        o_ref[g, pl.ds(r, w), :] = ((yn + bonus * xv) * gb).astype(BF16)

    def chunk(ci, carry):
        r = pl.multiple_of(ci * w, w)
        _round_robin([tile(g, ci, r) for g in range(g_n)])
        return carry

    lax.fori_loop(0, chunks, chunk, 0)

    if carries is not None:
        for refs_g, cr in zip((zr, zk, zv, zl), carries):
            for g in range(g_n):
                cr[g] = refs_g[g][blk - SUBLANES:blk, :]

    @pl.when(tb == pl.num_programs(2) - 1)
    def _():
        for s in range(g_n * nseq):
            ss = s_ref[s]
            s_out[0, s, 0] = ss[0:B_HEAD, 0:B_HEAD]
            s_out[0, s, 1] = ss[B_HEAD:n, B_HEAD:n]


def _rwkv(z, p, state, shift, prev_out, layer, depth, b, t, row_off):
    til = _Tiling(b, t, RWKV_ROWS, row_off)
    cum, msk, gmat = _rwkv_consts(til.c)
    pairs = B_HEADS // 2
    cb = B_OFF // LANES
    wb = B_WIDTH // LANES
    lora_w = B_LORA_W + B_LORA_A + B_LORA_G
    lb_z = (B_OFF + 3 * B_WIDTH) // lora_w
    lb_s = (3 * B_WIDTH) // lora_w

    def vec(x):
        return x.reshape(1, -1)

    def pspec(rows_, col_off=0):
        return pl.BlockSpec((rows_, LANES), lambda i, h, tb: (0, col_off + h))

    in_specs, args = [], []
    for width, col, per_head in ((LANES, cb, 1), (LANES, cb + wb, 1), (LANES, cb + 2 * wb, 1),
                                 (lora_w, lb_z, 0)):
        for g in range(til.g):
            in_specs.append(til.zspec(g, width, col, per_head))
            args.append(z)
    in_specs += [
        pspec(1), pspec(1, wb), pspec(1, 2 * wb),
        pl.BlockSpec((1, lora_w), lambda i, h, tb: (0, lb_s)),
        pspec(1), pspec(1), pspec(1), pspec(1), pspec(1), pspec(1), pspec(1),
        pspec(B_LORA_W), pspec(B_LORA_A), pspec(B_LORA_G),
        _const_spec(cum.shape), _const_spec(msk.shape), _const_spec(gmat.shape),
    ]
    mu = vec(p['rwkv_mu'])
    args += [mu, mu, mu, mu,
             vec(p['rwkv_w0']), vec(p['rwkv_a0']), vec(p['rwkv_kk']), vec(p['rwkv_ka']),
             vec(p['rwkv_rk']), vec(p['rwkv_gn_w']), vec(p['rwkv_gn_b']),
             p['rwkv_w2'].astype(BF16), p['rwkv_a2'].astype(BF16), p['rwkv_g2'].astype(BF16),
             jnp.asarray(cum, BF16), jnp.asarray(msk, F32), jnp.asarray(gmat, BF16)]
    if state is not None:
        ns = til.g * til.nseq
        in_specs += [
            pl.BlockSpec((ns, LANES), lambda i, h, tb: (i, h)),
            pl.BlockSpec((ns, LANES), lambda i, h, tb: (i, wb + h)),
            pl.BlockSpec((ns, LANES), lambda i, h, tb: (i, 2 * wb + h)),
            pl.BlockSpec((ns, lora_w), lambda i, h, tb: (i, lb_s)),
        ]
        args += [shift, shift, shift, shift]
    has_state, has_alias, aliases, s_spec, s_shape = _state_io(
        til, state, prev_out, layer, depth, b, B_HEADS, 2, B_HEAD, B_HEAD, in_specs, args)
    scratch = [pltpu.VMEM((til.g * til.nseq, 2 * B_HEAD, 2 * B_HEAD), F32)]
    if not has_state:
        scratch += [pltpu.VMEM((til.g, SUBLANES, wd), F32) for wd in (LANES, LANES, LANES, lora_w)]
    o, s = pl.pallas_call(
        functools.partial(_rwkv_kernel, g_n=til.g, chunks=til.chunks, nseq=til.nseq, c=til.c,
                          has_state=has_state, has_alias=has_alias),
        grid=(til.steps, pairs, til.nt),
        in_specs=in_specs,
        out_specs=[til.ospec(LANES), s_spec],
        out_shape=[til.oshape(B_WIDTH), s_shape],
        scratch_shapes=scratch,
        input_output_aliases=aliases,
        compiler_params=_cparams(("parallel", "parallel", "arbitrary")),
        name="rwkv7",
    )(*args)
    return o.reshape(b * t, B_WIDTH), s


def _merge_kernel(oa, ob, oc, wa, wb, wc, ga, gb, gc, o_ref):
    acc = _sigmoid(ga[...]) * jnp.dot(oa[...], wa[...], preferred_element_type=F32)
    acc = acc + _sigmoid(gb[...]) * jnp.dot(ob[...], wb[...], preferred_element_type=F32)
    acc = acc + _sigmoid(gc[...]) * jnp.dot(oc[...], wc[...], preferred_element_type=F32)
    o_ref[...] = acc.astype(BF16)


def _merge(z, oa, ob, oc, wa, wb, wc):
    m = oa.shape[0]
    tm = _pick(m, (1024, 512, 256, 128))
    tn = 256
    gblk = G_OFF // tn
    nb = D_MODEL // tn
    assert G_OFF % tn == 0

    def ospec():
        return pl.BlockSpec((tm, oa.shape[1]), lambda i, j: (i, 0))

    def wspec():
        return pl.BlockSpec((wa.shape[0], tn), lambda i, j: (0, j))

    def gspec(br):
        return pl.BlockSpec((tm, tn), lambda i, j: (i, gblk + br * nb + j))

    return pl.pallas_call(
        _merge_kernel,
        grid=(m // tm, nb),
        in_specs=[ospec(), ospec(), ospec(), wspec(), wspec(), wspec(), gspec(0), gspec(1), gspec(2)],
        out_specs=pl.BlockSpec((tm, tn), lambda i, j: (i, j)),
        out_shape=jax.ShapeDtypeStruct((m, D_MODEL), BF16),
        compiler_params=_cparams(("parallel", "arbitrary")),
        name="merge",
    )(oa, ob, oc, wa, wb, wc, z, z, z)


def _proj_res_kernel(m_ref, w_ref, g_ref, x_ref, o_ref):
    y = jnp.dot(m_ref[...], w_ref[...], preferred_element_type=F32)
    yn = y * lax.rsqrt(jnp.mean(y * y, axis=-1, keepdims=True) + NORM_EPS) * g_ref[...]
    o_ref[...] = x_ref[...] + yn


def _proj_res(mrg, w, g, x):
    m = x.shape[0]
    tm = _pick(m, (512, 256, 128))
    return pl.pallas_call(
        _proj_res_kernel,
        grid=(m // tm,),
        in_specs=[
            pl.BlockSpec((tm, D_MODEL), lambda i: (i, 0)),
            pl.BlockSpec((D_MODEL, D_MODEL), lambda i: (0, 0)),
            pl.BlockSpec((1, D_MODEL), lambda i: (0, 0)),
            pl.BlockSpec((tm, D_MODEL), lambda i: (i, 0)),
        ],
        out_specs=pl.BlockSpec((tm, D_MODEL), lambda i: (i, 0)),
        out_shape=jax.ShapeDtypeStruct((m, D_MODEL), F32),
        compiler_params=_cparams(("parallel",)),
        name="proj_res",
    )(mrg, w, g.reshape(1, D_MODEL), x)


def _gelu(x):
    return 0.5 * x * (1.0 + jnp.tanh(math.sqrt(2.0 / math.pi) * (x + 0.044715 * (x * x * x))))


def _ffn_down_kernel(*refs, tm, t, blocks_per_seq, has_state):
    if has_state:
        ua, ub, cw, cb, wd, g_ref, x_ref, st, o_ref, acc = refs
    else:
        ua, ub, halo, cw, cb, wd, g_ref, x_ref, o_ref, acc = refs
    kstep = pl.program_id(1)

    @pl.when(kstep == 0)
    def _():
        acc[...] = jnp.zeros_like(acc)

    x = ua[...]
    tk = x.shape[1]
    if has_state:
        ns = tm // t
        x3 = x.reshape(ns, t, tk)
        tt = lax.broadcasted_iota(jnp.int32, (ns, t, tk), 1)
        s_old = st[:, 0:1, :]
        s_new = st[:, 1:2, :]
        prev1 = jnp.where(tt >= 1, pltpu.roll(x3, 1, axis=1), s_new)
        prev2 = jnp.where(tt >= 2, pltpu.roll(x3, 2, axis=1), jnp.where(tt == 1, s_new, s_old))
        prev1 = prev1.reshape(tm, tk)
        prev2 = prev2.reshape(tm, tk)
    else:
        seq_start = (pl.program_id(0) % blocks_per_seq) == 0
        h = jnp.where(seq_start, 0.0, halo[...])
        rr = lax.broadcasted_iota(jnp.int32, (tm, tk), 0)
        h1 = h[SUBLANES - 1:SUBLANES, :]
        h2 = h[SUBLANES - 2:SUBLANES - 1, :]
        prev1 = jnp.where(rr == 0, h1, pltpu.roll(x, 1, axis=0))
        prev2 = jnp.where(rr == 0, h2, jnp.where(rr == 1, h1, pltpu.roll(x, 2, axis=0)))
    conv = cb[...] + cw[0:1, :] * prev2 + cw[1:2, :] * prev1 + cw[2:3, :] * x
    act = _gelu(conv) * ub[...]
    acc[...] += jnp.dot(act.astype(BF16), wd[...], preferred_element_type=F32)

    @pl.when(kstep == pl.num_programs(1) - 1)
    def _():
        y = acc[...]
        yn = y * lax.rsqrt(jnp.mean(y * y, axis=-1, keepdims=True) + NORM_EPS) * g_ref[...]
        o_ref[...] = x_ref[...] + yn


def _ffn_down(u, x1, cw, cb, wd, g, state, b, t, row_off):
    m = b * t
    has_state = state is not None
    tk = 512
    nk = D_FF // tk
    if has_state:
        tm = _pick(m, (512, 256, 128))
        assert tm % t == 0 and t == SUBLANES
        blocks_per_seq = 1
    else:
        tm = _pick(t, (512, 256, 128))
        blocks_per_seq = t // tm
    off = row_off // tm
    assert row_off % tm == 0
    in_specs = [
        pl.BlockSpec((tm, tk), lambda i, k: (i + off, k)),
        pl.BlockSpec((tm, tk), lambda i, k: (i + off, nk + k)),
    ]
    args = [u, u]
    if not has_state:
        hb = tm // SUBLANES
        hoff = row_off // SUBLANES
        in_specs.append(pl.BlockSpec((SUBLANES, tk), lambda i, k: (jnp.maximum(i * hb + hoff - 1, 0), k)))
        args.append(u)
    in_specs += [
        pl.BlockSpec((CONV_W, tk), lambda i, k: (0, k)),
        pl.BlockSpec((1, tk), lambda i, k: (0, k)),
        pl.BlockSpec((tk, D_MODEL), lambda i, k: (k, 0)),
        pl.BlockSpec((1, D_MODEL), lambda i, k: (0, 0)),
        pl.BlockSpec((tm, D_MODEL), lambda i, k: (i + off, 0)),
    ]
    args += [cw, cb.reshape(1, D_FF), wd, g.reshape(1, D_MODEL), x1]
    if has_state:
        in_specs.append(pl.BlockSpec((tm // t, CONV_W - 1, tk), lambda i, k: (i, 0, k)))
        args.append(state)
    return pl.pallas_call(
        functools.partial(_ffn_down_kernel, tm=tm, t=t, blocks_per_seq=blocks_per_seq, has_state=has_state),
        grid=(m // tm, nk),
        in_specs=in_specs,
        out_specs=pl.BlockSpec((tm, D_MODEL), lambda i, k: (i, 0)),
        out_shape=jax.ShapeDtypeStruct((m, D_MODEL), F32),
        scratch_shapes=[pltpu.VMEM((tm, D_MODEL), F32)],
        compiler_params=_cparams(("parallel", "arbitrary")),
        name="ffn_down",
    )(*args)


def _layer(x, groups, lb, p, layer, depth, prev):
    z = _rms_matmul(x, p['pre_mix_g'], p['w_in'], tn=1280)
    oa, ob, oc, states = [], [], [], []
    row = 0
    for gi, (b, t, t0, st) in enumerate(groups):
        s_a, s_b, s_sh, s_c, _ = st if st is not None else (None,) * 5
        pv = prev[gi] if prev is not None else (None,) * 3
        o_a, n_a = _hgrn(z, lb, p['a_norm_g'], s_a, pv[0], layer, depth, b, t, row)
        o_b, n_b = _rwkv(z, p, s_b, s_sh, pv[1], layer, depth, b, t, row)
        o_c, n_c = _retention(z, s_c, pv[2], layer, depth, b, t, t0, row)
        n_sh = lax.slice(z, (row + t - 1, B_OFF), (row + b * t, B_OFF + B_COLS), (t, 1))
        oa.append(o_a)
        ob.append(o_b)
        oc.append(o_c)
        states.append([n_a, n_b, n_sh, n_c])
        row += b * t
    oa, ob, oc = (jnp.concatenate(v, axis=0) for v in (oa, ob, oc))
    mrg = _merge(z, oa, ob, oc, p['w_br_a'], p['w_br_b'], p['w_br_c'])
    x1 = _proj_res(mrg, p['w_out'], p['post_mix_g'], x)
    u = _rms_matmul(x1, p['pre_ffn_g'], p['w_up'], tn=1024)
    outs = []
    row = 0
    for gi, (b, t, t0, st) in enumerate(groups):
        s_cv = st[4] if st is not None else None
        outs.append(_ffn_down(u, x1, p['conv_w'], p['conv_b'], p['w_down'], p['post_ffn_g'],
                              s_cv, b, t, row))
        rows = [lax.slice(u, (row + t - (CONV_W - 1) + j, 0), (row + b * t - (CONV_W - 2) + j, D_FF), (t, 1))
                for j in range(CONV_W - 1)]
        states[gi].append(jnp.stack(rows, axis=1))
        row += b * t
    return outs, states


def kernel(x_prompt, x_sample, state_hgrn, state_rwkv, state_rwkv_shift, state_ret, state_conv,
           lb_logits, pre_mix_g, w_in, a_norm_g, rwkv_mu, rwkv_w0, rwkv_w2, rwkv_a0, rwkv_a2,
           rwkv_g2, rwkv_kk, rwkv_ka, rwkv_rk, rwkv_gn_w, rwkv_gn_b, w_br_a, w_br_b, w_br_c,
           w_out, post_mix_g, pre_ffn_g, w_up, conv_w, conv_b, w_down, post_ffn_g):
    depth = w_in.shape[0]
    bp, tp, _ = x_prompt.shape
    bs, ts, _ = x_sample.shape
    past_len = 16384
    lb_soft = jax.nn.softmax(lb_logits.astype(F32), axis=0)
    lbs = jnp.cumsum(lb_soft, axis=0) - lb_soft[0]
    xs = [x_prompt.reshape(bp * tp, D_MODEL), x_sample.reshape(bs * ts, D_MODEL)]
    small = [[[], []], [[], []]]
    prev = None
    for l in range(depth):
        p = {
            'pre_mix_g': pre_mix_g[l], 'w_in': w_in[l].astype(BF16), 'a_norm_g': a_norm_g[l],
            'rwkv_mu': rwkv_mu[l], 'rwkv_w0': rwkv_w0[l], 'rwkv_w2': rwkv_w2[l],
            'rwkv_a0': rwkv_a0[l], 'rwkv_a2': rwkv_a2[l], 'rwkv_g2': rwkv_g2[l],
            'rwkv_kk': rwkv_kk[l], 'rwkv_ka': rwkv_ka[l], 'rwkv_rk': rwkv_rk[l],
            'rwkv_gn_w': rwkv_gn_w[l], 'rwkv_gn_b': rwkv_gn_b[l],
            'w_br_a': w_br_a[l].astype(BF16), 'w_br_b': w_br_b[l].astype(BF16),
            'w_br_c': w_br_c[l].astype(BF16), 'w_out': w_out[l].astype(BF16),
            'post_mix_g': post_mix_g[l], 'pre_ffn_g': pre_ffn_g[l], 'w_up': w_up[l].astype(BF16),
            'conv_w': conv_w[l], 'conv_b': conv_b[l], 'w_down': w_down[l].astype(BF16),
            'post_ffn_g': post_ffn_g[l],
        }
        groups = [
            (bp, tp, 0, None),
            (bs, ts, past_len, (state_hgrn, state_rwkv, state_rwkv_shift[l], state_ret, state_conv[l])),
        ]
        x = jnp.concatenate(xs, axis=0)
        xs, states = _layer(x, groups, lbs[l], p, l, depth, prev)
        prev = [(st[0], st[1], st[3]) for st in states]
        for gi, st in enumerate(states):
            small[gi][0].append(st[2])
            small[gi][1].append(st[4])
    y_p = xs[0].reshape(bp, tp, D_MODEL)
    y_s = xs[1].reshape(bs, ts, D_MODEL)
    outs = []
    for gi in range(2):
        outs += [prev[gi][0], prev[gi][1], jnp.stack(small[gi][0]), prev[gi][2], jnp.stack(small[gi][1])]
    return (y_p, y_s, *outs)
```

```python
import functools
import itertools
import math

import jax
import jax.numpy as jnp
import numpy as np
from jax import lax
from jax.experimental import pallas as pl
from jax.experimental.pallas import tpu as pltpu

F32 = jnp.float32
BF16 = jnp.bfloat16

D_MODEL = 2048
A_HEADS, A_DK, A_DV = 8, 128, 128
A_QK = A_HEADS * A_DK
A_WIDTH = A_HEADS * A_DV
F_TINY = 1e-30
B_HEAD = 64
B_WIDTH = 1024
B_HEADS = B_WIDTH // B_HEAD
B_LORA_W, B_LORA_A, B_LORA_G = 64, 64, 128
RWKV_GN_EPS = 64e-5
C_HEADS, C_DK, C_DV = 4, 128, 256
C_QK = C_HEADS * C_DK
C_WIDTH = C_HEADS * C_DV
ROPE_BASE = 10000.0
A_COLS = 2 * A_QK + 2 * A_WIDTH
B_COLS = 3 * B_WIDTH + B_LORA_W + B_LORA_A + B_LORA_G
C_COLS = 2 * C_QK + 2 * C_WIDTH
N_BRANCH = 3
P_COLS = A_COLS + B_COLS + C_COLS + N_BRANCH * D_MODEL
B_OFF = A_COLS
C_OFF = A_COLS + B_COLS
G_OFF = A_COLS + B_COLS + C_COLS
D_FF = 5632
CONV_W = 3
NORM_EPS = 1e-6

LANES = 128
SUBLANES = 8
MIX_ROWS = 128
RWKV_ROWS = 64
MIX_TIME_BLOCK = 512
MIX_STREAMS = 4
FFN_TK = 1408
VMEM_LIMIT = 56 * 1024 * 1024


def _cparams(sem):
    return pltpu.CompilerParams(dimension_semantics=sem, vmem_limit_bytes=VMEM_LIMIT)


def _dot(a, b):
    return jnp.dot(a.astype(BF16), b.astype(BF16), preferred_element_type=F32)


def _dot_nt(a, b):
    return lax.dot_general(a.astype(BF16), b.astype(BF16), (((1,), (1,)), ((), ())),
                           preferred_element_type=F32)


def _dot_tn(a, b):
    return lax.dot_general(a.astype(BF16), b.astype(BF16), (((0,), (0,)), ((), ())),
                           preferred_element_type=F32)


def _split(x):
    hi = x.astype(BF16)
    lo = (x - hi.astype(F32)).astype(BF16)
    return hi, lo


def _sel_dot(m, x):
    hi, lo = _split(x)
    return (jnp.dot(m, hi, preferred_element_type=F32)
            + jnp.dot(m, lo, preferred_element_type=F32))


def _dot_sel(x, m):
    hi, lo = _split(x)
    return (jnp.dot(hi, m, preferred_element_type=F32)
            + jnp.dot(lo, m, preferred_element_type=F32))


def _sigmoid(x):
    return jax.nn.sigmoid(x)


def _silu(x):
    return x * jax.nn.sigmoid(x)


def _round_robin(gens):
    for _ in itertools.zip_longest(*gens):
        pass


def _pick(n, cands):
    for c in cands:
        if n % c == 0:
            return c
    raise ValueError(f"no tile in {cands} divides {n}")


def _rms_matmul_kernel(x_ref, g_ref, w_ref, o_ref, xn_ref, *, tm, sub):
    @pl.when(pl.program_id(1) == 0)
    def _():
        def body(i, carry):
            r = pl.multiple_of(i * sub, sub)
            x = x_ref[pl.ds(r, sub), :]
            ms = jnp.mean(x * x, axis=-1, keepdims=True)
            xn_ref[pl.ds(r, sub), :] = (x * lax.rsqrt(ms + NORM_EPS) * g_ref[...]).astype(BF16)
            return carry
        lax.fori_loop(0, tm // sub, body, 0)

    o_ref[...] = jnp.dot(xn_ref[...], w_ref[...], preferred_element_type=F32)


def _rms_matmul(x, g, w, layer, tn):
    m, k = x.shape
    n = w.shape[2]
    tm = _pick(m, (1024, 512, 256, 128))
    sub = min(tm, 128)
    return pl.pallas_call(
        functools.partial(_rms_matmul_kernel, tm=tm, sub=sub),
        grid=(m // tm, n // tn),
        in_specs=[
            pl.BlockSpec((tm, k), lambda i, j: (i, 0)),
            pl.BlockSpec((1, k), lambda i, j: (0, 0)),
            pl.BlockSpec((None, k, tn), lambda i, j: (layer, 0, j)),
        ],
        out_specs=pl.BlockSpec((tm, tn), lambda i, j: (i, j)),
        out_shape=jax.ShapeDtypeStruct((m, n), F32),
        scratch_shapes=[pltpu.VMEM((tm, k), BF16)],
        compiler_params=_cparams(("parallel", "arbitrary")),
        name="rms_matmul",
    )(x, g.reshape(1, k), w)


class _Tiling:
    def __init__(self, b, t, rows, row_off):
        if t >= rows:
            self.nseq, self.c = 1, rows
            self.blk = _pick(t, (MIX_TIME_BLOCK, rows))
            self.nt = t // self.blk
            n_streams = b
        else:
            assert rows % t == 0 and b % (rows // t) == 0
            self.nseq, self.c = rows // t, t
            self.blk, self.nt = rows, 1
            n_streams = b // self.nseq
        self.g = _pick(n_streams, (MIX_STREAMS, 2, 1))
        self.steps = n_streams // self.g
        self.n_streams = n_streams
        self.chunks = self.blk // rows
        assert row_off % self.blk == 0
        self.off = row_off // self.blk

    def zspec(self, stream, width, col_blk, per_head=1):
        g, nt, off = self.g, self.nt, self.off
        return pl.BlockSpec(
            (self.blk, width),
            lambda i, h, tb: (off + (i * g + stream) * nt + tb, col_blk + per_head * h))

    def ospec(self, width):
        return pl.BlockSpec((self.g, self.blk, width), lambda i, h, tb: (i, tb, h))

    def oshape(self, width):
        return jax.ShapeDtypeStruct((self.n_streams, self.nt * self.blk, width), BF16)

    def sspec(self, layer, heads_per_step, d0, d1):
        return pl.BlockSpec((1, self.g * self.nseq, heads_per_step, d0, d1),
                            lambda i, h, tb: (layer, i, h, 0, 0))


def _const_spec(shape):
    nd = len(shape)
    return pl.BlockSpec(shape, lambda i, h, tb: (0,) * nd)


def _state_io(til, state, prev_out, layer, depth, b, heads, heads_per_step, d0, d1,
              in_specs, args):
    has_state = state is not None
    if has_state:
        in_specs.append(til.sspec(layer, heads_per_step, d0, d1))
        args.append(state)
    aliases = {}
    if prev_out is not None:
        aliases = {len(args): 1}
        in_specs.append(pl.BlockSpec(memory_space=pl.ANY))
        args.append(prev_out)
    out_spec = til.sspec(layer, heads_per_step, d0, d1)
    out_shape = jax.ShapeDtypeStruct((depth, b, heads, d0, d1), F32)
    return has_state, prev_out is not None, aliases, out_spec, out_shape


@functools.lru_cache(maxsize=None)
def _hgrn_consts(c):
    n = MIX_ROWS
    nlev = int(math.log2(c))
    t = np.arange(n)
    u = np.arange(n)[None, :]
    blk = t // c
    same = blk[:, None] == blk[None, :]
    mats = [same & (u <= t[:, None]), same]
    masks = [np.eye(n, dtype=bool)]
    for lev in range(nlev):
        h = 1 << lev
        base = (t // (2 * h)) * (2 * h)
        mid = base + h
        upper = t >= mid
        e_up = (u >= mid[:, None]) & (u <= t[:, None])
        e_lo = (u >= t[:, None] + 1) & (u <= mid[:, None] - 1)
        mats.append(np.where(upper[:, None], e_up, e_lo))
        masks.append((base[:, None] == base[None, :]) & upper[:, None] & (~upper)[None, :])
    sel = np.concatenate(mats, 0).astype(np.float32)
    msk = np.stack(masks).astype(np.float32)
    return sel, msk, nlev


def _hgrn_kernel(*refs, g_n, chunks, nseq, c, nlev, has_state, has_alias):
    it = iter(refs)
    zq, zf, zi, zg = ([next(it) for _ in range(g_n)] for _ in range(4))
    lb, gn, sel, msk = (next(it) for _ in range(4))
    s0 = next(it) if has_state else None
    if has_alias:
        next(it)
    o_ref, s_out, s_ref = next(it), next(it), next(it)
    n = MIX_ROWS
    tb = pl.program_id(2)

    @pl.when(tb == 0)
    def _():
        if has_state:
            s_ref[...] = s0[0, :, 0]
        else:
            s_ref[...] = jnp.zeros_like(s_ref)

    lbv = lb[...]

    def tile(g, r):
        xq = zq[g][pl.ds(r, n), :]
        fa = zf[g][pl.ds(r, n), :]
        v = zi[g][pl.ds(r, n), :]
        xg = zg[g][pl.ds(r, n), :]
        q = _silu(xq)
        f_gate = lbv + (1.0 - lbv) * _sigmoid(fa)
        gl = jnp.log(jnp.maximum(f_gate, F_TINY))
        k = (1.0 - lbv) * _sigmoid(-fa)
        e = _sel_dot(sel[...], gl)
        yield
        b = e[0:n]
        bl = e[n:2 * n]
        scores = msk[0] * _dot_nt(q, k)
        for lev in range(nlev):
            x = jnp.exp(e[(lev + 2) * n:(lev + 3) * n])
            scores = scores + msk[lev + 1] * _dot_nt(q * x, k * x)
        yield
        o = _dot(scores, v)
        yield
        qe = q * jnp.exp(b)
        kt = k * jnp.exp(bl - b)
        dt = jnp.exp(bl).T
        outs = []
        for s in range(nseq):
            rows = slice(s * c, (s + 1) * c)
            ss = s_ref[g * nseq + s]
            outs.append(_dot(qe[rows], ss))
            dcol = jnp.broadcast_to(dt[:, s * c:s * c + 1], (A_DK, A_DV))
            s_ref[g * nseq + s] = ss * dcol + _dot_tn(kt[rows], v[rows])
        o = o + (outs[0] if nseq == 1 else jnp.concatenate(outs, axis=0))
        on = o * lax.rsqrt(jnp.mean(o * o, axis=-1, keepdims=True) + NORM_EPS) * gn[...]
        o_ref[g, pl.ds(r, n), :] = (on * _silu(xg)).astype(BF16)

    def chunk(ci, carry):
        r = pl.multiple_of(ci * n, n)
        _round_robin([tile(g, r) for g in range(g_n)])
        return carry

    lax.fori_loop(0, chunks, chunk, 0)

    @pl.when(tb == pl.num_programs(2) - 1)
    def _():
        s_out[0, :, 0] = s_ref[...]


def _hgrn(z, lb, gn, state, prev_out, layer, depth, b, t, row_off):
    til = _Tiling(b, t, MIX_ROWS, row_off)
    sel, msk, nlev = _hgrn_consts(til.c)
    qk_blocks = A_QK // LANES
    in_specs, args = [], []
    for col in range(4):
        for g in range(til.g):
            in_specs.append(til.zspec(g, LANES, col * qk_blocks))
            args.append(z)
    in_specs += [
        pl.BlockSpec((1, LANES), lambda i, h, tb: (0, h)),
        pl.BlockSpec((1, LANES), lambda i, h, tb: (0, h)),
        _const_spec(sel.shape), _const_spec(msk.shape),
    ]
    args += [lb.reshape(1, A_QK), gn.reshape(1, A_WIDTH), jnp.asarray(sel, BF16), jnp.asarray(msk, F32)]
    has_state, has_alias, aliases, s_spec, s_shape = _state_io(
        til, state, prev_out, layer, depth, b, A_HEADS, 1, A_DK, A_DV, in_specs, args)
    o, s = pl.pallas_call(
        functools.partial(_hgrn_kernel, g_n=til.g, chunks=til.chunks, nseq=til.nseq, c=til.c,
                          nlev=nlev, has_state=has_state, has_alias=has_alias),
        grid=(til.steps, A_HEADS, til.nt),
        in_specs=in_specs,
        out_specs=[til.ospec(LANES), s_spec],
        out_shape=[til.oshape(A_WIDTH), s_shape],
        scratch_shapes=[pltpu.VMEM((til.g * til.nseq, A_DK, A_DV), F32)],
        input_output_aliases=aliases,
        compiler_params=_cparams(("parallel", "parallel", "arbitrary")),
        name="hgrn2",
    )(*args)
    return o.reshape(b * t, A_WIDTH), s


def _ret_tables(c):
    n = MIX_ROWS
    log_g = jnp.log1p(-jnp.exp2(-5.0 - jnp.arange(C_HEADS, dtype=F32)))
    t = np.arange(n)
    tt = (t % c).astype(np.float32)
    blk = t // c
    rel = tt[:, None] - tt[None, :]
    same = (blk[:, None] == blk[None, :]) & (rel >= 0)
    dmat = jnp.where(same[None], jnp.exp(log_g[:, None, None] * np.maximum(rel, 0.0)[None]), 0.0)
    inner = jnp.exp(log_g[:, None] * (tt[None, :] + 1.0))
    tail = jnp.exp(log_g[:, None] * (c - 1.0 - tt[None, :]))
    total = jnp.exp(log_g * c)
    shape = (C_HEADS, n, n)
    tab = jnp.stack([dmat, jnp.broadcast_to(inner[:, :, None], shape),
                     jnp.broadcast_to(tail[:, :, None], shape)], axis=1)
    tot = jnp.broadcast_to(total[:, None, None], (C_HEADS, 1, C_DV))
    return tab.astype(F32), tot.astype(F32)


def _rope_tables(t0, t, reps):
    half = C_DK // 2
    inv = ROPE_BASE ** (-jnp.arange(half, dtype=F32) / half)
    pos = t0 + jnp.arange(t, dtype=F32)
    ang = pos[:, None] * inv[None, :]
    cos, sin = jnp.cos(ang), jnp.sin(ang)
    cosf = jnp.concatenate([cos, cos], axis=-1)
    sinf = jnp.concatenate([-sin, sin], axis=-1)
    return jnp.tile(cosf, (reps, 1)), jnp.tile(sinf, (reps, 1))


def _ret_kernel(*refs, g_n, chunks, nseq, c, has_state, has_alias):
    it = iter(refs)
    zq, zk, zv, zg = ([next(it) for _ in range(g_n)] for _ in range(4))
    cos, sin, tab, tot = (next(it) for _ in range(4))
    s0 = next(it) if has_state else None
    if has_alias:
        next(it)
    o_ref, s_out, s_ref = next(it), next(it), next(it)
    n = MIX_ROWS
    half = C_DK // 2
    tb = pl.program_id(2)

    @pl.when(tb == 0)
    def _():
        if has_state:
            s_ref[...] = s0[0, :, 0]
        else:
            s_ref[...] = jnp.zeros_like(s_ref)

    def tile(g, r, cs, sn):
        xq = zq[g][pl.ds(r, n), :]
        xk = zk[g][pl.ds(r, n), :]
        v = zv[g][pl.ds(r, n), :]
        xg = zg[g][pl.ds(r, n), :]
        q = xq * cs + pltpu.roll(xq, half, axis=1) * sn
        k = (xk * cs + pltpu.roll(xk, half, axis=1) * sn) * (C_DK ** -0.5)
        scores = _dot_nt(q, k) * tab[0, 0]
        yield
        o = _dot(scores, v)
        yield
        qi = q * tab[0, 1]
        ktl = k * tab[0, 2]
        outs = []
        for s in range(nseq):
            rows = slice(s * c, (s + 1) * c)
            ss = s_ref[g * nseq + s]
            outs.append(_dot(qi[rows], ss))
            s_ref[g * nseq + s] = tot[0] * ss + _dot_tn(ktl[rows], v[rows])
        o = o + (outs[0] if nseq == 1 else jnp.concatenate(outs, axis=0))
        on = o * lax.rsqrt(jnp.mean(o * o, axis=-1, keepdims=True) + NORM_EPS)
        o_ref[g, pl.ds(r, n), :] = (on * _silu(xg)).astype(BF16)

    def chunk(ci, carry):
        r = pl.multiple_of(ci * n, n)
        cs = cos[pl.ds(r, n), :]
        sn = sin[pl.ds(r, n), :]
        _round_robin([tile(g, r, cs, sn) for g in range(g_n)])
        return carry

    lax.fori_loop(0, chunks, chunk, 0)

    @pl.when(tb == pl.num_programs(2) - 1)
    def _():
        s_out[0, :, 0] = s_ref[...]


def _retention(z, state, prev_out, layer, depth, b, t, t0, row_off):
    til = _Tiling(b, t, MIX_ROWS, row_off)
    tab, tot = _ret_tables(til.c)
    cosf, sinf = _rope_tables(t0, t, max(MIX_ROWS // t, 1))
    qb = C_OFF // C_DK
    vb = (C_OFF + 2 * C_QK) // C_DV
    gb = (C_OFF + 2 * C_QK + C_WIDTH) // C_DV
    in_specs, args = [], []
    for width, col in ((C_DK, qb), (C_DK, qb + C_HEADS), (C_DV, vb), (C_DV, gb)):
        for g in range(til.g):
            in_specs.append(til.zspec(g, width, col))
            args.append(z)
    in_specs += [
        pl.BlockSpec((til.blk, C_DK), lambda i, h, tb: (tb, 0)),
        pl.BlockSpec((til.blk, C_DK), lambda i, h, tb: (tb, 0)),
        pl.BlockSpec((1, 3, MIX_ROWS, MIX_ROWS), lambda i, h, tb: (h, 0, 0, 0)),
        pl.BlockSpec((1, 1, C_DV), lambda i, h, tb: (h, 0, 0)),
    ]
    args += [cosf, sinf, tab, tot]
    has_state, has_alias, aliases, s_spec, s_shape = _state_io(
        til, state, prev_out, layer, depth, b, C_HEADS, 1, C_DK, C_DV, in_specs, args)
    o, s = pl.pallas_call(
        functools.partial(_ret_kernel, g_n=til.g, chunks=til.chunks, nseq=til.nseq, c=til.c,
                          has_state=has_state, has_alias=has_alias),
        grid=(til.steps, C_HEADS, til.nt),
        in_specs=in_specs,
        out_specs=[til.ospec(C_DV), s_spec],
        out_shape=[til.oshape(C_WIDTH), s_shape],
        scratch_shapes=[pltpu.VMEM((til.g * til.nseq, C_DK, C_DV), F32)],
        input_output_aliases=aliases,
        compiler_params=_cparams(("parallel", "parallel", "arbitrary")),
        name="retention",
    )(*args)
    return o.reshape(b * t, C_WIDTH), s


@functools.lru_cache(maxsize=None)
def _rwkv_consts(c):
    w = RWKV_ROWS
    t = np.arange(w)
    blk = t // c
    same = blk[:, None] == blk[None, :]
    tri = same & (t[None, :] <= t[:, None])
    cum = np.concatenate([tri, same], 0).astype(np.float32)
    rr = np.arange(2 * w)
    grp = rr // c
    tt = rr % c
    sameg = grp[:, None] == grp[None, :]
    strict = sameg & (tt[None, :] < tt[:, None])
    incl = sameg & (tt[None, :] <= tt[:, None])
    masks = np.stack([strict, incl]).astype(np.float32)
    hh = rr // B_HEAD
    gmat = (hh[:, None] == hh[None, :]).astype(np.float32)
    return cum, masks, gmat


def _rwkv_kernel(*refs, g_n, chunks, nseq, c, has_state, has_alias):
    it = iter(refs)
    zr, zk, zv, zl = ([next(it) for _ in range(g_n)] for _ in range(4))
    (mu_r, mu_k, mu_v, mu_l, w0, a0, kkp, kap, rkp, gnw, gnb,
     w2, a2, g2, cum, msk, gmat) = (next(it) for _ in range(17))
    if has_state:
        sh_r, sh_k, sh_v, sh_l, s0 = (next(it) for _ in range(5))
    if has_alias:
        next(it)
    o_ref, s_out = next(it), next(it)
    shift_outs = [next(it) for _ in range(4)]
    s_ref = next(it)
    carries = None if has_state else [next(it) for _ in range(4)]
    w = RWKV_ROWS
    n = 2 * w
    blk = chunks * w
    nsq = int(math.log2(c)) - 1
    tb = pl.program_id(2)
    lane = lax.broadcasted_iota(jnp.int32, (w, LANES), 1)
    head0 = lane < B_HEAD
    row = lax.broadcasted_iota(jnp.int32, (w, 1), 0)
    first = (row % c) == 0
    gm = gmat[...]
    gm_f = gm.astype(F32)
    eye = (lax.broadcasted_iota(jnp.int32, (n, n), 0)
           == lax.broadcasted_iota(jnp.int32, (n, n), 1)).astype(F32)

    @pl.when(tb == 0)
    def _():
        if has_state:
            zero_blk = jnp.zeros((B_HEAD, B_HEAD), F32)
            for s in range(g_n * nseq):
                top = jnp.concatenate([s0[0, s, 0], zero_blk], axis=1)
                bot = jnp.concatenate([zero_blk, s0[0, s, 1]], axis=1)
                s_ref[s] = jnp.concatenate([top, bot], axis=0)
        else:
            s_ref[...] = jnp.zeros_like(s_ref)
            for cr in carries:
                cr[...] = jnp.zeros_like(cr)

    def stack(x):
        return jnp.concatenate([jnp.where(head0, x, 0.0), jnp.where(head0, 0.0, x)], axis=0)

    def fold(x):
        return x[0:w] + x[w:n]

    def gsum(x):
        return _dot_sel(x, gm)

    def shifted(refs_g, sh, cr, mu, g, ci, r):
        ref = refs_g[g]
        x = ref[pl.ds(r, w), :]
        width = x.shape[1]
        if has_state:
            src = jnp.concatenate(
                [jnp.broadcast_to(sh[g * nseq + s:g * nseq + s + 1, :], (c, width))
                 for s in range(nseq)], axis=0)
        else:
            rp = pl.multiple_of(jnp.maximum(r - SUBLANES, 0), SUBLANES)
            prev8 = jnp.where(ci == 0, cr[g], ref[pl.ds(rp, SUBLANES), :])
            src = jnp.broadcast_to(prev8[SUBLANES - 1:SUBLANES, :], (w, width))
        prev = jnp.where(first, src, pltpu.roll(x, 1, axis=0))
        return x + mu[...] * (prev - x)

    def tile(g, ci, r):
        cr = carries if carries is not None else [None] * 4
        xr = shifted(zr, sh_r if has_state else None, cr[0], mu_r, g, ci, r)
        xk = shifted(zk, sh_k if has_state else None, cr[1], mu_k, g, ci, r)
        xv = shifted(zv, sh_v if has_state else None, cr[2], mu_v, g, ci, r)
        xl = shifted(zl, sh_l if has_state else None, cr[3], mu_l, g, ci, r)
        wd = xl[:, 0:B_LORA_W]
        ad = xl[:, B_LORA_W:B_LORA_W + B_LORA_A]
        gd = xl[:, B_LORA_W + B_LORA_A:]
        wx = -(w0[...] + _dot(jnp.tanh(wd), w2[...]))
        w_raw = -(jnp.maximum(wx, 0.0) + jnp.log1p(jnp.exp(-jnp.abs(wx)))) - 0.5
        lw = -jnp.exp(w_raw)
        aa = _sigmoid(a0[...] + _dot(ad, a2[...]))
        gb = _dot(_sigmoid(gd), g2[...])
        yield
        kk = xk * kkp[...]
        kk = kk / jnp.maximum(jnp.sqrt(gsum(kk * kk)), 1e-12)
        k2 = xk * (1.0 + (aa - 1.0) * kap[...])
        a = -kk
        b = kk * aa
        yield
        e = _sel_dot(cum[...], lw)
        yield
        lwc = e[0:w]
        lwl = e[w:n]
        dec_in = jnp.exp(lwc)
        dec_ex = jnp.exp(lwc - lw)
        inv = jnp.exp(-lwc)
        rest = jnp.exp(lwl - lwc)
        a_t = a * dec_ex
        r_t = xr * dec_in
        b_t = b * inv
        k_t = k2 * inv
        gram = _dot_nt(jnp.concatenate([stack(a_t), stack(r_t)], axis=0),
                       jnp.concatenate([stack(b_t), stack(k_t)], axis=0))
        yield
        m_ab = gram[0:n, 0:n] * msk[0]
        m_ak = gram[0:n, n:2 * n] * msk[0]
        m_rb = gram[n:2 * n, 0:n] * msk[1]
        m_rk = gram[n:2 * n, n:2 * n] * msk[1]
        p = m_ab
        tinv = eye + p
        for _ in range(nsq):
            p = _dot(p, p)
            yield
            tinv = tinv + _dot(tinv, p)
            yield
        p0a, p0r = [], []
        for s in range(nseq):
            rows = slice(s * c, (s + 1) * c)
            pr = _dot_nt(jnp.concatenate([a_t[rows], r_t[rows]], axis=0), s_ref[g * nseq + s])
            p0a.append(pr[0:c])
            p0r.append(pr[c:2 * c])
        p0a = p0a[0] if nseq == 1 else jnp.concatenate(p0a, axis=0)
        p0r = p0r[0] if nseq == 1 else jnp.concatenate(p0r, axis=0)
        yield
        vs = stack(xv)
        rhs = stack(p0a) + _dot(m_ak, vs)
        yield
        us = _dot(tinv, rhs)
        yield
        ys = stack(p0r) + _dot(m_rb, us) + _dot(m_rk, vs)
        yield
        y = fold(ys)
        u_w = fold(us)
        b_g = b * rest
        k_g = k2 * rest
        dec_l = jnp.exp(lwl)
        for s in range(nseq):
            rows = slice(s * c, (s + 1) * c)
            upd = _dot_tn(jnp.concatenate([u_w[rows], xv[rows]], axis=0),
                          jnp.concatenate([b_g[rows], k_g[rows]], axis=0))
            s_ref[g * nseq + s] = s_ref[g * nseq + s] * dec_l[s * c:s * c + 1, :] + gm_f * upd
        yield
        mean = gsum(y) * (1.0 / B_HEAD)
        yield
        d = y - mean
        var = gsum(d * d) * (1.0 / B_HEAD)
        yn = d * lax.rsqrt(var + RWKV_GN_EPS) * gnw[...] + gnb[...]
        yield
        bonus = gsum(xr * k2 * rkp[...])
        o_ref[g, pl.ds(r, w), :] = ((yn + bonus * xv) * gb).astype(BF16)

    def chunk(ci, carry):
        r = pl.multiple_of(ci * w, w)
        _round_robin([tile(g, ci, r) for g in range(g_n)])
        return carry

    lax.fori_loop(0, chunks, chunk, 0)

    if carries is not None:
        for refs_g, cr in zip((zr, zk, zv, zl), carries):
            for g in range(g_n):
                cr[g] = refs_g[g][blk - SUBLANES:blk, :]

    base = blk - w
    for refs_g, out in zip((zr, zk, zv, zl), shift_outs):
        for g in range(g_n):
            for s in range(nseq):
                last = base + (s + 1) * c - 1
                out[g * nseq + s:g * nseq + s + 1, :] = refs_g[g][last:last + 1, :]

    @pl.when(tb == pl.num_programs(2) - 1)
    def _():
        for s in range(g_n * nseq):
            ss = s_ref[s]
            s_out[0, s, 0] = ss[0:B_HEAD, 0:B_HEAD]
            s_out[0, s, 1] = ss[B_HEAD:n, B_HEAD:n]


def _rwkv(z, p, state, shift, prev_out, layer, depth, b, t, row_off):
    til = _Tiling(b, t, RWKV_ROWS, row_off)
    cum, msk, gmat = _rwkv_consts(til.c)
    pairs = B_HEADS // 2
    cb = B_OFF // LANES
    wb = B_WIDTH // LANES
    lora_w = B_LORA_W + B_LORA_A + B_LORA_G
    lb_z = (B_OFF + 3 * B_WIDTH) // lora_w
    lb_s = (3 * B_WIDTH) // lora_w

    def vec(x):
        return x.reshape(1, -1)

    def pspec(rows_, col_off=0):
        return pl.BlockSpec((rows_, LANES), lambda i, h, tb: (0, col_off + h))

    in_specs, args = [], []
    for width, col, per_head in ((LANES, cb, 1), (LANES, cb + wb, 1), (LANES, cb + 2 * wb, 1),
                                 (lora_w, lb_z, 0)):
        for g in range(til.g):
            in_specs.append(til.zspec(g, width, col, per_head))
            args.append(z)
    in_specs += [
        pspec(1), pspec(1, wb), pspec(1, 2 * wb),
        pl.BlockSpec((1, lora_w), lambda i, h, tb: (0, lb_s)),
        pspec(1), pspec(1), pspec(1), pspec(1), pspec(1), pspec(1), pspec(1),
        pspec(B_LORA_W), pspec(B_LORA_A), pspec(B_LORA_G),
        _const_spec(cum.shape), _const_spec(msk.shape), _const_spec(gmat.shape),
    ]
    mu = vec(p['rwkv_mu'])
    args += [mu, mu, mu, mu,
             vec(p['rwkv_w0']), vec(p['rwkv_a0']), vec(p['rwkv_kk']), vec(p['rwkv_ka']),
             vec(p['rwkv_rk']), vec(p['rwkv_gn_w']), vec(p['rwkv_gn_b']),
             p['rwkv_w2'].astype(BF16), p['rwkv_a2'].astype(BF16), p['rwkv_g2'].astype(BF16),
             jnp.asarray(cum, BF16), jnp.asarray(msk, F32), jnp.asarray(gmat, BF16)]
    ns = til.g * til.nseq
    if state is not None:
        in_specs += [
            pl.BlockSpec((ns, LANES), lambda i, h, tb: (i, h)),
            pl.BlockSpec((ns, LANES), lambda i, h, tb: (i, wb + h)),
            pl.BlockSpec((ns, LANES), lambda i, h, tb: (i, 2 * wb + h)),
            pl.BlockSpec((ns, lora_w), lambda i, h, tb: (i, lb_s)),
        ]
        args += [shift, shift, shift, shift]
    has_state, has_alias, aliases, s_spec, s_shape = _state_io(
        til, state, prev_out, layer, depth, b, B_HEADS, 2, B_HEAD, B_HEAD, in_specs, args)
    scratch = [pltpu.VMEM((til.g * til.nseq, 2 * B_HEAD, 2 * B_HEAD), F32)]
    if not has_state:
        scratch += [pltpu.VMEM((til.g, SUBLANES, wd), F32) for wd in (LANES, LANES, LANES, lora_w)]
    o, s, sh_r, sh_k, sh_v, sh_l = pl.pallas_call(
        functools.partial(_rwkv_kernel, g_n=til.g, chunks=til.chunks, nseq=til.nseq, c=til.c,
                          has_state=has_state, has_alias=has_alias),
        grid=(til.steps, pairs, til.nt),
        in_specs=in_specs,
        out_specs=[til.ospec(LANES), s_spec,
                   pl.BlockSpec((ns, LANES), lambda i, h, tb: (i, h)),
                   pl.BlockSpec((ns, LANES), lambda i, h, tb: (i, h)),
                   pl.BlockSpec((ns, LANES), lambda i, h, tb: (i, h)),
                   pl.BlockSpec((ns, lora_w), lambda i, h, tb: (i, 0))],
        out_shape=[til.oshape(B_WIDTH), s_shape] + [
            jax.ShapeDtypeStruct((b, wd), F32) for wd in (B_WIDTH, B_WIDTH, B_WIDTH, lora_w)],
        scratch_shapes=scratch,
        input_output_aliases=aliases,
        compiler_params=_cparams(("parallel", "parallel", "arbitrary")),
        name="rwkv7",
    )(*args)
    return o.reshape(b * t, B_WIDTH), s, jnp.concatenate([sh_r, sh_k, sh_v, sh_l], axis=1)


def _merge_kernel(oa, ob, oc, wa, wb, wc, ga, gb, gc, o_ref):
    acc = _sigmoid(ga[...]) * jnp.dot(oa[...], wa[...], preferred_element_type=F32)
    acc = acc + _sigmoid(gb[...]) * jnp.dot(ob[...], wb[...], preferred_element_type=F32)
    acc = acc + _sigmoid(gc[...]) * jnp.dot(oc[...], wc[...], preferred_element_type=F32)
    o_ref[...] = acc.astype(BF16)


def _merge(z, oa, ob, oc, wa, wb, wc, layer):
    m = oa.shape[0]
    tm = _pick(m, (1024, 512, 256, 128))
    tn = 256
    gblk = G_OFF // tn
    nb = D_MODEL // tn
    assert G_OFF % tn == 0

    def ospec():
        return pl.BlockSpec((tm, oa.shape[1]), lambda i, j: (i, 0))

    def wspec():
        return pl.BlockSpec((None, wa.shape[1], tn), lambda i, j: (layer, 0, j))

    def gspec(br):
        return pl.BlockSpec((tm, tn), lambda i, j: (i, gblk + br * nb + j))

    return pl.pallas_call(
        _merge_kernel,
        grid=(m // tm, nb),
        in_specs=[ospec(), ospec(), ospec(), wspec(), wspec(), wspec(), gspec(0), gspec(1), gspec(2)],
        out_specs=pl.BlockSpec((tm, tn), lambda i, j: (i, j)),
        out_shape=jax.ShapeDtypeStruct((m, D_MODEL), BF16),
        compiler_params=_cparams(("parallel", "arbitrary")),
        name="merge",
    )(oa, ob, oc, wa, wb, wc, z, z, z)


def _proj_res_kernel(m_ref, w_ref, g_ref, x_ref, o_ref):
    y = jnp.dot(m_ref[...], w_ref[...], preferred_element_type=F32)
    yn = y * lax.rsqrt(jnp.mean(y * y, axis=-1, keepdims=True) + NORM_EPS) * g_ref[...]
    o_ref[...] = x_ref[...] + yn


def _proj_res(mrg, w, g, x, layer):
    m = x.shape[0]
    tm = _pick(m, (512, 256, 128))
    return pl.pallas_call(
        _proj_res_kernel,
        grid=(m // tm,),
        in_specs=[
            pl.BlockSpec((tm, D_MODEL), lambda i: (i, 0)),
            pl.BlockSpec((None, D_MODEL, D_MODEL), lambda i: (layer, 0, 0)),
            pl.BlockSpec((1, D_MODEL), lambda i: (0, 0)),
            pl.BlockSpec((tm, D_MODEL), lambda i: (i, 0)),
        ],
        out_specs=pl.BlockSpec((tm, D_MODEL), lambda i: (i, 0)),
        out_shape=jax.ShapeDtypeStruct((m, D_MODEL), F32),
        compiler_params=_cparams(("parallel",)),
        name="proj_res",
    )(mrg, w, g.reshape(1, D_MODEL), x)


def _gelu(x):
    return 0.5 * x * (1.0 + jnp.tanh(math.sqrt(2.0 / math.pi) * (x + 0.044715 * (x * x * x))))


def _ffn_down_kernel(*refs, tm, t, blocks_per_seq, has_state, has_alias):
    it = iter(refs)
    ua, ub = next(it), next(it)
    halo = None if has_state else next(it)
    cw, cb, wd, g_ref, x_ref = (next(it) for _ in range(5))
    st = next(it) if has_state else None
    if has_alias:
        next(it)
    o_ref, nc_ref, acc, act = (next(it) for _ in range(4))
    kstep = pl.program_id(1)
    tk = ua.shape[1]

    @pl.when(kstep == 0)
    def _():
        acc[...] = jnp.zeros_like(acc)

    if has_state:
        ns = tm // t
        tt = lax.broadcasted_iota(jnp.int32, (ns, t, LANES), 1)
    else:
        seq_start = (pl.program_id(0) % blocks_per_seq) == 0
        rr = lax.broadcasted_iota(jnp.int32, (tm, LANES), 0)
    for j in range(tk // LANES):
        cols = slice(j * LANES, (j + 1) * LANES)
        x = ua[:, cols]
        if has_state:
            x3 = x.reshape(ns, t, LANES)
            s_old = st[:, 0:1, cols]
            s_new = st[:, 1:2, cols]
            prev1 = jnp.where(tt >= 1, pltpu.roll(x3, 1, axis=1), s_new)
            prev2 = jnp.where(tt >= 2, pltpu.roll(x3, 2, axis=1), jnp.where(tt == 1, s_new, s_old))
            prev1 = prev1.reshape(tm, LANES)
            prev2 = prev2.reshape(tm, LANES)
            nc_ref[:, :, cols] = x3[:, t - (CONV_W - 1):, :]
        else:
            h = jnp.where(seq_start, 0.0, halo[:, cols])
            h1 = h[SUBLANES - 1:SUBLANES, :]
            h2 = h[SUBLANES - 2:SUBLANES - 1, :]
            prev1 = jnp.where(rr == 0, h1, pltpu.roll(x, 1, axis=0))
            prev2 = jnp.where(rr == 0, h2, jnp.where(rr == 1, h1, pltpu.roll(x, 2, axis=0)))
            nc_ref[0, :, cols] = x[tm - (CONV_W - 1):, :]
        conv = cb[:, cols] + cw[0:1, cols] * prev2 + cw[1:2, cols] * prev1 + cw[2:3, cols] * x
        act[:, cols] = (_gelu(conv) * ub[:, cols]).astype(BF16)
    acc[...] += jnp.dot(act[...], wd[...], preferred_element_type=F32)

    @pl.when(kstep == pl.num_programs(1) - 1)
    def _():
        y = acc[...]
        yn = y * lax.rsqrt(jnp.mean(y * y, axis=-1, keepdims=True) + NORM_EPS) * g_ref[...]
        o_ref[...] = x_ref[...] + yn


def _ffn_down(u, x1, cw, cb, wd, g, state, prev_out, layer, b, t, row_off):
    m = b * t
    total = x1.shape[0]
    has_state = state is not None
    has_alias = prev_out is not None
    tk = FFN_TK
    nk = D_FF // tk
    if has_state:
        tm = _pick(m, (512, 256, 128))
        assert tm % t == 0 and t == SUBLANES
        blocks_per_seq = 1
        nc_spec = pl.BlockSpec((tm // t, CONV_W - 1, tk), lambda i, k: (i, 0, k))
    else:
        tm = _pick(t, (512, 256, 128))
        blocks_per_seq = t // tm
        nc_spec = pl.BlockSpec((1, CONV_W - 1, tk), lambda i, k: (i // blocks_per_seq, 0, k))
    off = row_off // tm
    assert row_off % tm == 0
    in_specs = [
        pl.BlockSpec((tm, tk), lambda i, k: (i + off, k)),
        pl.BlockSpec((tm, tk), lambda i, k: (i + off, nk + k)),
    ]
    args = [u, u]
    if not has_state:
        hb = tm // SUBLANES
        hoff = row_off // SUBLANES
        in_specs.append(pl.BlockSpec((SUBLANES, tk), lambda i, k: (jnp.maximum(i * hb + hoff - 1, 0), k)))
        args.append(u)
    in_specs += [
        pl.BlockSpec((CONV_W, tk), lambda i, k: (0, k)),
        pl.BlockSpec((1, tk), lambda i, k: (0, k)),
        pl.BlockSpec((None, tk, D_MODEL), lambda i, k: (layer, k, 0)),
        pl.BlockSpec((1, D_MODEL), lambda i, k: (0, 0)),
        pl.BlockSpec((tm, D_MODEL), lambda i, k: (i + off, 0)),
    ]
    args += [cw, cb.reshape(1, D_FF), wd, g.reshape(1, D_MODEL), x1]
    if has_state:
        in_specs.append(pl.BlockSpec((tm // t, CONV_W - 1, tk), lambda i, k: (i, 0, k)))
        args.append(state)
    aliases = {}
    if has_alias:
        aliases = {len(args): 0}
        in_specs.append(pl.BlockSpec(memory_space=pl.ANY))
        args.append(prev_out)
    return pl.pallas_call(
        functools.partial(_ffn_down_kernel, tm=tm, t=t, blocks_per_seq=blocks_per_seq,
                          has_state=has_state, has_alias=has_alias),
        grid=(m // tm, nk),
        in_specs=in_specs,
        out_specs=[pl.BlockSpec((tm, D_MODEL), lambda i, k: (i + off, 0)), nc_spec],
        out_shape=[jax.ShapeDtypeStruct((total, D_MODEL), F32),
                   jax.ShapeDtypeStruct((b, CONV_W - 1, D_FF), F32)],
        scratch_shapes=[pltpu.VMEM((tm, D_MODEL), F32), pltpu.VMEM((tm, tk), BF16)],
        input_output_aliases=aliases,
        compiler_params=_cparams(("arbitrary", "arbitrary")),
        name="ffn_down",
    )(*args)


def _layer(x, groups, lb, p, layer, depth, prev):
    z = _rms_matmul(x, p['pre_mix_g'], p['w_in'], layer, tn=1280)
    oa, ob, oc, states = [], [], [], []
    row = 0
    for gi, (b, t, t0, st) in enumerate(groups):
        s_a, s_b, s_sh, s_c, _ = st if st is not None else (None,) * 5
        pv = prev[gi] if prev is not None else (None,) * 3
        o_a, n_a = _hgrn(z, lb, p['a_norm_g'], s_a, pv[0], layer, depth, b, t, row)
        o_b, n_b, n_sh = _rwkv(z, p, s_b, s_sh, pv[1], layer, depth, b, t, row)
        o_c, n_c = _retention(z, s_c, pv[2], layer, depth, b, t, t0, row)
        oa.append(o_a)
        ob.append(o_b)
        oc.append(o_c)
        states.append([n_a, n_b, n_sh, n_c])
        row += b * t
    oa, ob, oc = (jnp.concatenate(v, axis=0) for v in (oa, ob, oc))
    mrg = _merge(z, oa, ob, oc, p['w_br_a'], p['w_br_b'], p['w_br_c'], layer)
    x1 = _proj_res(mrg, p['w_out'], p['post_mix_g'], x, layer)
    u = _rms_matmul(x1, p['pre_ffn_g'], p['w_up'], layer, tn=1024)
    x2 = None
    row = 0
    for gi, (b, t, t0, st) in enumerate(groups):
        s_cv = st[4] if st is not None else None
        x2, n_cv = _ffn_down(u, x1, p['conv_w'], p['conv_b'], p['w_down'], p['post_ffn_g'],
                             s_cv, x2, layer, b, t, row)
        states[gi].append(n_cv)
        row += b * t
    return x2, states


def kernel(x_prompt, x_sample, state_hgrn, state_rwkv, state_rwkv_shift, state_ret, state_conv,
           lb_logits, pre_mix_g, w_in, a_norm_g, rwkv_mu, rwkv_w0, rwkv_w2, rwkv_a0, rwkv_a2,
           rwkv_g2, rwkv_kk, rwkv_ka, rwkv_rk, rwkv_gn_w, rwkv_gn_b, w_br_a, w_br_b, w_br_c,
           w_out, post_mix_g, pre_ffn_g, w_up, conv_w, conv_b, w_down, post_ffn_g):
    depth = w_in.shape[0]
    bp, tp, _ = x_prompt.shape
    bs, ts, _ = x_sample.shape
    past_len = 16384
    lb_soft = jax.nn.softmax(lb_logits.astype(F32), axis=0)
    lbs = jnp.cumsum(lb_soft, axis=0) - lb_soft[0]
    big = {'w_in': w_in.astype(BF16), 'w_br_a': w_br_a.astype(BF16), 'w_br_b': w_br_b.astype(BF16),
           'w_br_c': w_br_c.astype(BF16), 'w_out': w_out.astype(BF16), 'w_up': w_up.astype(BF16),
           'w_down': w_down.astype(BF16)}
    x = jnp.concatenate([x_prompt.reshape(bp * tp, D_MODEL), x_sample.reshape(bs * ts, D_MODEL)], axis=0)
    small = [[[], []], [[], []]]
    prev = None
    for l in range(depth):
        p = dict(big)
        p.update({
            'pre_mix_g': pre_mix_g[l], 'a_norm_g': a_norm_g[l],
            'rwkv_mu': rwkv_mu[l], 'rwkv_w0': rwkv_w0[l], 'rwkv_w2': rwkv_w2[l],
            'rwkv_a0': rwkv_a0[l], 'rwkv_a2': rwkv_a2[l], 'rwkv_g2': rwkv_g2[l],
            'rwkv_kk': rwkv_kk[l], 'rwkv_ka': rwkv_ka[l], 'rwkv_rk': rwkv_rk[l],
            'rwkv_gn_w': rwkv_gn_w[l], 'rwkv_gn_b': rwkv_gn_b[l],
            'post_mix_g': post_mix_g[l], 'pre_ffn_g': pre_ffn_g[l],
            'conv_w': conv_w[l], 'conv_b': conv_b[l], 'post_ffn_g': post_ffn_g[l],
        })
        groups = [
            (bp, tp, 0, None),
            (bs, ts, past_len, (state_hgrn, state_rwkv, state_rwkv_shift[l], state_ret, state_conv[l])),
        ]
        x, states = _layer(x, groups, lbs[l], p, l, depth, prev)
        prev = [(st[0], st[1], st[3]) for st in states]
        for gi, st in enumerate(states):
            small[gi][0].append(st[2])
            small[gi][1].append(st[4])
    y_p = x[:bp * tp].reshape(bp, tp, D_MODEL)
    y_s = x[bp * tp:].reshape(bs, ts, D_MODEL)
    outs = []
    for gi in range(2):
        outs += [prev[gi][0], prev[gi][1], jnp.stack(small[gi][0]), prev[gi][2], jnp.stack(small[gi][1])]
    return (y_p, y_s, *outs)
```

```python
import functools
import itertools
import math

import jax
import jax.numpy as jnp
import numpy as np
from jax import lax
from jax.experimental import pallas as pl
from jax.experimental.pallas import tpu as pltpu

F32 = jnp.float32
BF16 = jnp.bfloat16

D_MODEL = 2048
A_HEADS, A_DK, A_DV = 8, 128, 128
A_QK = A_HEADS * A_DK
A_WIDTH = A_HEADS * A_DV
F_TINY = 1e-30
B_HEAD = 64
B_WIDTH = 1024
B_HEADS = B_WIDTH // B_HEAD
B_LORA_W, B_LORA_A, B_LORA_G = 64, 64, 128
RWKV_GN_EPS = 64e-5
C_HEADS, C_DK, C_DV = 4, 128, 256
C_QK = C_HEADS * C_DK
C_WIDTH = C_HEADS * C_DV
ROPE_BASE = 10000.0
A_COLS = 2 * A_QK + 2 * A_WIDTH
B_COLS = 3 * B_WIDTH + B_LORA_W + B_LORA_A + B_LORA_G
C_COLS = 2 * C_QK + 2 * C_WIDTH
N_BRANCH = 3
P_COLS = A_COLS + B_COLS + C_COLS + N_BRANCH * D_MODEL
B_OFF = A_COLS
C_OFF = A_COLS + B_COLS
G_OFF = A_COLS + B_COLS + C_COLS
D_FF = 5632
CONV_W = 3
NORM_EPS = 1e-6

LANES = 128
SUBLANES = 8
MIX_ROWS = 128
RWKV_ROWS = 64
MIX_TIME_BLOCK = 512
MIX_STREAMS = 4
MIX_STREAMS_TOTAL = 8
STATE_WINDOW_BYTES = 4 * 1024 * 1024
FFN_TK = 1408
VMEM_LIMIT = 56 * 1024 * 1024


def _cparams(sem):
    return pltpu.CompilerParams(dimension_semantics=sem, vmem_limit_bytes=VMEM_LIMIT)


def _dot(a, b):
    return jnp.dot(a.astype(BF16), b.astype(BF16), preferred_element_type=F32)


def _dot_nt(a, b):
    return lax.dot_general(a.astype(BF16), b.astype(BF16), (((1,), (1,)), ((), ())),
                           preferred_element_type=F32)


def _dot_tn(a, b):
    return lax.dot_general(a.astype(BF16), b.astype(BF16), (((0,), (0,)), ((), ())),
                           preferred_element_type=F32)


def _split(x):
    hi = x.astype(BF16)
    lo = (x - hi.astype(F32)).astype(BF16)
    return hi, lo


def _sel_dot(m, x):
    hi, lo = _split(x)
    n = x.shape[1]
    both = jnp.dot(m, jnp.concatenate([hi, lo], axis=1), preferred_element_type=F32)
    return both[:, :n] + both[:, n:]


def _dot_sel(x, m):
    hi, lo = _split(x)
    return (jnp.dot(hi, m, preferred_element_type=F32)
            + jnp.dot(lo, m, preferred_element_type=F32))


def _sigmoid(x):
    return jax.nn.sigmoid(x)


def _silu(x):
    return x * jax.nn.sigmoid(x)


def _round_robin(gens):
    for _ in itertools.zip_longest(*gens):
        pass


def _pick(n, cands):
    for c in cands:
        if n % c == 0:
            return c
    raise ValueError(f"no tile in {cands} divides {n}")


def _rms_matmul_kernel(x_ref, g_ref, w_ref, o_ref, xn_ref, *, tm, sub):
    @pl.when(pl.program_id(1) == 0)
    def _():
        def body(i, carry):
            r = pl.multiple_of(i * sub, sub)
            x = x_ref[pl.ds(r, sub), :]
            ms = jnp.mean(x * x, axis=-1, keepdims=True)
            xn_ref[pl.ds(r, sub), :] = (x * lax.rsqrt(ms + NORM_EPS) * g_ref[...]).astype(BF16)
            return carry
        lax.fori_loop(0, tm // sub, body, 0)

    o_ref[...] = jnp.dot(xn_ref[...], w_ref[...], preferred_element_type=F32)


def _rms_matmul(x, g, w, layer, tn):
    m, k = x.shape
    n = w.shape[2]
    tm = _pick(m, (1024, 512, 256, 128))
    sub = min(tm, 128)
    return pl.pallas_call(
        functools.partial(_rms_matmul_kernel, tm=tm, sub=sub),
        grid=(m // tm, n // tn),
        in_specs=[
            pl.BlockSpec((tm, k), lambda i, j: (i, 0)),
            pl.BlockSpec((1, k), lambda i, j: (0, 0)),
            pl.BlockSpec((None, k, tn), lambda i, j: (layer, 0, j)),
        ],
        out_specs=pl.BlockSpec((tm, tn), lambda i, j: (i, j)),
        out_shape=jax.ShapeDtypeStruct((m, n), F32),
        scratch_shapes=[pltpu.VMEM((tm, k), BF16)],
        compiler_params=_cparams(("parallel", "arbitrary")),
        name="rms_matmul",
    )(x, g.reshape(1, k), w)


class _Tiling:
    def __init__(self, b, t, rows, row_off, heads, state_bytes):
        if t >= rows:
            self.nseq, self.c = 1, rows
            self.blk = _pick(t, (MIX_TIME_BLOCK, rows))
            self.nt = t // self.blk
            n_streams = b
        else:
            assert rows % t == 0 and b % (rows // t) == 0
            self.nseq, self.c = rows // t, t
            self.blk, self.nt = rows, 1
            n_streams = b // self.nseq
        self.g = _pick(n_streams, (MIX_STREAMS, 2, 1))
        self.hs = _pick(heads, (max(MIX_STREAMS_TOTAL // self.g, 1), 1))
        while self.g * self.nseq * self.hs * state_bytes > STATE_WINDOW_BYTES and self.hs > 1:
            self.hs //= 2
        while self.g * self.nseq * self.hs * state_bytes > STATE_WINDOW_BYTES and self.g > 1:
            self.g //= 2
        self.head_steps = heads // self.hs
        self.steps = n_streams // self.g
        self.n_streams = n_streams
        self.chunks = self.blk // rows
        assert row_off % self.blk == 0
        self.off = row_off // self.blk

    def streams(self):
        return [(j, g) for j in range(self.hs) for g in range(self.g)]

    def zspec(self, j, stream, width, col_blk, per_head=1):
        g, nt, off, hs = self.g, self.nt, self.off, self.hs
        return pl.BlockSpec(
            (self.blk, width),
            lambda i, h, tb: (off + (i * g + stream) * nt + tb, col_blk + per_head * (h * hs + j)))

    def hspec(self, rows, width, blk_off=0):
        assert blk_off % self.hs == 0
        off = blk_off // self.hs
        return pl.BlockSpec((rows, self.hs * width), lambda i, h, tb: (0, off + h))

    def ospec(self, width):
        return pl.BlockSpec((self.g, self.blk, self.hs * width), lambda i, h, tb: (i, tb, h))

    def oshape(self, width):
        return jax.ShapeDtypeStruct((self.n_streams, self.nt * self.blk, width), BF16)

    def sspec(self, layer, heads_per_step, d0, d1):
        return pl.BlockSpec((1, self.g * self.nseq, self.hs * heads_per_step, d0, d1),
                            lambda i, h, tb: (layer, i, h, 0, 0))


def _const_spec(shape):
    nd = len(shape)
    return pl.BlockSpec(shape, lambda i, h, tb: (0,) * nd)


def _state_io(til, state, prev_out, layer, depth, b, heads, heads_per_step, d0, d1,
              in_specs, args):
    has_state = state is not None
    if has_state:
        in_specs.append(til.sspec(layer, heads_per_step, d0, d1))
        args.append(state)
    aliases = {}
    if prev_out is not None:
        aliases = {len(args): 1}
        in_specs.append(pl.BlockSpec(memory_space=pl.ANY))
        args.append(prev_out)
    out_spec = til.sspec(layer, heads_per_step, d0, d1)
    out_shape = jax.ShapeDtypeStruct((depth, b, heads, d0, d1), F32)
    return has_state, prev_out is not None, aliases, out_spec, out_shape


@functools.lru_cache(maxsize=None)
def _hgrn_consts(c):
    n = MIX_ROWS
    nlev = int(math.log2(c))
    t = np.arange(n)
    u = np.arange(n)[None, :]
    blk = t // c
    same = blk[:, None] == blk[None, :]
    mats = [same & (u <= t[:, None]), same]
    masks = [np.eye(n, dtype=bool)]
    for lev in range(nlev):
        h = 1 << lev
        base = (t // (2 * h)) * (2 * h)
        mid = base + h
        upper = t >= mid
        e_up = (u >= mid[:, None]) & (u <= t[:, None])
        e_lo = (u >= t[:, None] + 1) & (u <= mid[:, None] - 1)
        mats.append(np.where(upper[:, None], e_up, e_lo))
        masks.append((base[:, None] == base[None, :]) & upper[:, None] & (~upper)[None, :])
    sel = np.concatenate(mats, 0).astype(np.float32)
    msk = np.stack(masks).astype(np.float32)
    return sel, msk, nlev


def _hgrn_kernel(*refs, g_n, hs, chunks, nseq, c, nlev, has_state, has_alias):
    it = iter(refs)
    zq, zf, zi, zg = ([[next(it) for _ in range(g_n)] for _ in range(hs)] for _ in range(4))
    lb, gn, sel, msk = (next(it) for _ in range(4))
    s0 = next(it) if has_state else None
    if has_alias:
        next(it)
    o_ref, s_out, s_ref = next(it), next(it), next(it)
    n = MIX_ROWS
    tb = pl.program_id(2)

    @pl.when(tb == 0)
    def _():
        if has_state:
            for j in range(hs):
                s_ref[j] = s0[0, :, j]
        else:
            s_ref[...] = jnp.zeros_like(s_ref)

    def tile(j, g, r):
        hcols = slice(j * LANES, (j + 1) * LANES)
        lbv = lb[:, hcols]
        xq = zq[j][g][pl.ds(r, n), :]
        fa = zf[j][g][pl.ds(r, n), :]
        v = zi[j][g][pl.ds(r, n), :]
        xg = zg[j][g][pl.ds(r, n), :]
        q = _silu(xq)
        f_gate = lbv + (1.0 - lbv) * _sigmoid(fa)
        gl = jnp.log(jnp.maximum(f_gate, F_TINY))
        k = (1.0 - lbv) * _sigmoid(-fa)
        e = _sel_dot(sel[...], gl)
        yield
        b = e[0:n]
        bl = e[n:2 * n]
        scores = msk[0] * _dot_nt(q, k)
        for lev in range(nlev):
            x = jnp.exp(e[(lev + 2) * n:(lev + 3) * n])
            scores = scores + msk[lev + 1] * _dot_nt(q * x, k * x)
        yield
        o = _dot(scores, v)
        yield
        qe = q * jnp.exp(b)
        kt = k * jnp.exp(bl - b)
        dt = jnp.exp(bl).T
        outs = []
        for s in range(nseq):
            rows = slice(s * c, (s + 1) * c)
            ss = s_ref[j, g * nseq + s]
            outs.append(_dot(qe[rows], ss))
            dcol = jnp.broadcast_to(dt[:, s * c:s * c + 1], (A_DK, A_DV))
            s_ref[j, g * nseq + s] = ss * dcol + _dot_tn(kt[rows], v[rows])
        o = o + (outs[0] if nseq == 1 else jnp.concatenate(outs, axis=0))
        on = o * lax.rsqrt(jnp.mean(o * o, axis=-1, keepdims=True) + NORM_EPS) * gn[:, hcols]
        o_ref[g, pl.ds(r, n), hcols] = (on * _silu(xg)).astype(BF16)

    def chunk(ci, carry):
        r = pl.multiple_of(ci * n, n)
        _round_robin([tile(j, g, r) for j in range(hs) for g in range(g_n)])
        return carry

    lax.fori_loop(0, chunks, chunk, 0)

    @pl.when(tb == pl.num_programs(2) - 1)
    def _():
        for j in range(hs):
            s_out[0, :, j] = s_ref[j]


def _hgrn(z, lb, gn, state, prev_out, layer, depth, b, t, row_off):
    til = _Tiling(b, t, MIX_ROWS, row_off, A_HEADS, A_DK * A_DV * 4)
    sel, msk, nlev = _hgrn_consts(til.c)
    qk_blocks = A_QK // LANES
    in_specs, args = [], []
    for col in range(4):
        for j, g in til.streams():
            in_specs.append(til.zspec(j, g, LANES, col * qk_blocks))
            args.append(z)
    in_specs += [
        til.hspec(1, LANES), til.hspec(1, LANES),
        _const_spec(sel.shape), _const_spec(msk.shape),
    ]
    args += [lb.reshape(1, A_QK), gn.reshape(1, A_WIDTH), jnp.asarray(sel, BF16), jnp.asarray(msk, F32)]
    has_state, has_alias, aliases, s_spec, s_shape = _state_io(
        til, state, prev_out, layer, depth, b, A_HEADS, 1, A_DK, A_DV, in_specs, args)
    o, s = pl.pallas_call(
        functools.partial(_hgrn_kernel, g_n=til.g, hs=til.hs, chunks=til.chunks, nseq=til.nseq, c=til.c,
                          nlev=nlev, has_state=has_state, has_alias=has_alias),
        grid=(til.steps, til.head_steps, til.nt),
        in_specs=in_specs,
        out_specs=[til.ospec(LANES), s_spec],
        out_shape=[til.oshape(A_WIDTH), s_shape],
        scratch_shapes=[pltpu.VMEM((til.hs, til.g * til.nseq, A_DK, A_DV), F32)],
        input_output_aliases=aliases,
        compiler_params=_cparams(("parallel", "parallel", "arbitrary")),
        name="hgrn2",
    )(*args)
    return o.reshape(b * t, A_WIDTH), s


def _ret_tables(c):
    n = MIX_ROWS
    log_g = jnp.log1p(-jnp.exp2(-5.0 - jnp.arange(C_HEADS, dtype=F32)))
    t = np.arange(n)
    tt = (t % c).astype(np.float32)
    blk = t // c
    rel = tt[:, None] - tt[None, :]
    same = (blk[:, None] == blk[None, :]) & (rel >= 0)
    dmat = jnp.where(same[None], jnp.exp(log_g[:, None, None] * np.maximum(rel, 0.0)[None]), 0.0)
    inner = jnp.exp(log_g[:, None] * (tt[None, :] + 1.0))
    tail = jnp.exp(log_g[:, None] * (c - 1.0 - tt[None, :]))
    total = jnp.exp(log_g * c)
    shape = (C_HEADS, n, n)
    tab = jnp.stack([dmat, jnp.broadcast_to(inner[:, :, None], shape),
                     jnp.broadcast_to(tail[:, :, None], shape)], axis=1)
    tot = jnp.broadcast_to(total[:, None, None], (C_HEADS, 1, C_DV))
    return tab.astype(F32), tot.astype(F32)


def _rope_tables(t0, t, reps):
    half = C_DK // 2
    inv = ROPE_BASE ** (-jnp.arange(half, dtype=F32) / half)
    pos = t0 + jnp.arange(t, dtype=F32)
    ang = pos[:, None] * inv[None, :]
    cos, sin = jnp.cos(ang), jnp.sin(ang)
    cosf = jnp.concatenate([cos, cos], axis=-1)
    sinf = jnp.concatenate([-sin, sin], axis=-1)
    return jnp.tile(cosf, (reps, 1)), jnp.tile(sinf, (reps, 1))


def _ret_kernel(*refs, g_n, hs, chunks, nseq, c, has_state, has_alias):
    it = iter(refs)
    zq, zk, zv, zg = ([[next(it) for _ in range(g_n)] for _ in range(hs)] for _ in range(4))
    cos, sin, tab, tot = (next(it) for _ in range(4))
    s0 = next(it) if has_state else None
    if has_alias:
        next(it)
    o_ref, s_out, s_ref = next(it), next(it), next(it)
    n = MIX_ROWS
    half = C_DK // 2
    tb = pl.program_id(2)

    @pl.when(tb == 0)
    def _():
        if has_state:
            for j in range(hs):
                s_ref[j] = s0[0, :, j]
        else:
            s_ref[...] = jnp.zeros_like(s_ref)

    def tile(j, g, r, cs, sn):
        xq = zq[j][g][pl.ds(r, n), :]
        xk = zk[j][g][pl.ds(r, n), :]
        v = zv[j][g][pl.ds(r, n), :]
        xg = zg[j][g][pl.ds(r, n), :]
        q = xq * cs + pltpu.roll(xq, half, axis=1) * sn
        k = (xk * cs + pltpu.roll(xk, half, axis=1) * sn) * (C_DK ** -0.5)
        scores = _dot_nt(q, k) * tab[j, 0]
        yield
        o = _dot(scores, v)
        yield
        qi = q * tab[j, 1]
        ktl = k * tab[j, 2]
        outs = []
        for s in range(nseq):
            rows = slice(s * c, (s + 1) * c)
            ss = s_ref[j, g * nseq + s]
            outs.append(_dot(qi[rows], ss))
            s_ref[j, g * nseq + s] = tot[j] * ss + _dot_tn(ktl[rows], v[rows])
        o = o + (outs[0] if nseq == 1 else jnp.concatenate(outs, axis=0))
        on = o * lax.rsqrt(jnp.mean(o * o, axis=-1, keepdims=True) + NORM_EPS)
        o_ref[g, pl.ds(r, n), j * C_DV:(j + 1) * C_DV] = (on * _silu(xg)).astype(BF16)

    def chunk(ci, carry):
        r = pl.multiple_of(ci * n, n)
        cs = cos[pl.ds(r, n), :]
        sn = sin[pl.ds(r, n), :]
        _round_robin([tile(j, g, r, cs, sn) for j in range(hs) for g in range(g_n)])
        return carry

    lax.fori_loop(0, chunks, chunk, 0)

    @pl.when(tb == pl.num_programs(2) - 1)
    def _():
        for j in range(hs):
            s_out[0, :, j] = s_ref[j]


def _retention(z, state, prev_out, layer, depth, b, t, t0, row_off):
    til = _Tiling(b, t, MIX_ROWS, row_off, C_HEADS, C_DK * C_DV * 4)
    tab, tot = _ret_tables(til.c)
    cosf, sinf = _rope_tables(t0, t, max(MIX_ROWS // t, 1))
    qb = C_OFF // C_DK
    vb = (C_OFF + 2 * C_QK) // C_DV
    gb = (C_OFF + 2 * C_QK + C_WIDTH) // C_DV
    in_specs, args = [], []
    for width, col in ((C_DK, qb), (C_DK, qb + C_HEADS), (C_DV, vb), (C_DV, gb)):
        for j, g in til.streams():
            in_specs.append(til.zspec(j, g, width, col))
            args.append(z)
    in_specs += [
        pl.BlockSpec((til.blk, C_DK), lambda i, h, tb: (tb, 0)),
        pl.BlockSpec((til.blk, C_DK), lambda i, h, tb: (tb, 0)),
        pl.BlockSpec((til.hs, 3, MIX_ROWS, MIX_ROWS), lambda i, h, tb: (h, 0, 0, 0)),
        pl.BlockSpec((til.hs, 1, C_DV), lambda i, h, tb: (h, 0, 0)),
    ]
    args += [cosf, sinf, tab, tot]
    has_state, has_alias, aliases, s_spec, s_shape = _state_io(
        til, state, prev_out, layer, depth, b, C_HEADS, 1, C_DK, C_DV, in_specs, args)
    o, s = pl.pallas_call(
        functools.partial(_ret_kernel, g_n=til.g, hs=til.hs, chunks=til.chunks, nseq=til.nseq, c=til.c,
                          has_state=has_state, has_alias=has_alias),
        grid=(til.steps, til.head_steps, til.nt),
        in_specs=in_specs,
        out_specs=[til.ospec(C_DV), s_spec],
        out_shape=[til.oshape(C_WIDTH), s_shape],
        scratch_shapes=[pltpu.VMEM((til.hs, til.g * til.nseq, C_DK, C_DV), F32)],
        input_output_aliases=aliases,
        compiler_params=_cparams(("parallel", "parallel", "arbitrary")),
        name="retention",
    )(*args)
    return o.reshape(b * t, C_WIDTH), s


@functools.lru_cache(maxsize=None)
def _rwkv_consts(c):
    w = RWKV_ROWS
    t = np.arange(w)
    blk = t // c
    same = blk[:, None] == blk[None, :]
    tri = same & (t[None, :] <= t[:, None])
    cum = np.concatenate([tri, same], 0).astype(np.float32)
    rr = np.arange(2 * w)
    grp = rr // c
    tt = rr % c
    sameg = grp[:, None] == grp[None, :]
    strict = sameg & (tt[None, :] < tt[:, None])
    incl = sameg & (tt[None, :] <= tt[:, None])
    masks = np.stack([strict, incl]).astype(np.float32)
    hh = rr // B_HEAD
    gmat = (hh[:, None] == hh[None, :]).astype(np.float32)
    return cum, masks, gmat


def _rwkv_kernel(*refs, g_n, hs, chunks, nseq, c, has_state, has_alias):
    it = iter(refs)
    zr, zk, zv = ([[next(it) for _ in range(g_n)] for _ in range(hs)] for _ in range(3))
    zl = [next(it) for _ in range(g_n)]
    (mu_r, mu_k, mu_v, mu_l, w0, a0, kkp, kap, rkp, gnw, gnb,
     w2, a2, g2, cum, msk, gmat) = (next(it) for _ in range(17))
    if has_state:
        sh_r, sh_k, sh_v, sh_l, s0 = (next(it) for _ in range(5))
    if has_alias:
        next(it)
    o_ref, s_out = next(it), next(it)
    shift_outs = [next(it) for _ in range(4)]
    s_ref = next(it)
    carries = None if has_state else [next(it) for _ in range(4)]
    w = RWKV_ROWS
    n = 2 * w
    blk = chunks * w
    nsq = int(math.log2(c)) - 1
    tb = pl.program_id(2)
    lane = lax.broadcasted_iota(jnp.int32, (w, LANES), 1)
    head0 = lane < B_HEAD
    row = lax.broadcasted_iota(jnp.int32, (w, 1), 0)
    first = (row % c) == 0
    gm = gmat[...]
    gm_f = gm.astype(F32)
    eye = (lax.broadcasted_iota(jnp.int32, (n, n), 0)
           == lax.broadcasted_iota(jnp.int32, (n, n), 1)).astype(F32)

    @pl.when(tb == 0)
    def _():
        if has_state:
            zero_blk = jnp.zeros((B_HEAD, B_HEAD), F32)
            for j in range(hs):
                for s in range(g_n * nseq):
                    top = jnp.concatenate([s0[0, s, 2 * j], zero_blk], axis=1)
                    bot = jnp.concatenate([zero_blk, s0[0, s, 2 * j + 1]], axis=1)
                    s_ref[j, s] = jnp.concatenate([top, bot], axis=0)
        else:
            s_ref[...] = jnp.zeros_like(s_ref)
            for cr in carries:
                cr[...] = jnp.zeros_like(cr)

    def stack(x):
        return jnp.concatenate([jnp.where(head0, x, 0.0), jnp.where(head0, 0.0, x)], axis=0)

    def fold(x):
        return x[0:w] + x[w:n]

    def gsum(x):
        return _dot_sel(x, gm)

    def shifted(ref, sh, cols, carry, mu, g, ci, r):
        x = ref[pl.ds(r, w), :]
        width = x.shape[1]
        if has_state:
            src = jnp.concatenate(
                [jnp.broadcast_to(sh[g * nseq + s:g * nseq + s + 1, cols], (c, width))
                 for s in range(nseq)], axis=0)
        else:
            rp = pl.multiple_of(jnp.maximum(r - SUBLANES, 0), SUBLANES)
            prev8 = jnp.where(ci == 0, carry, ref[pl.ds(rp, SUBLANES), :])
            src = jnp.broadcast_to(prev8[SUBLANES - 1:SUBLANES, :], (w, width))
        prev = jnp.where(first, src, pltpu.roll(x, 1, axis=0))
        return x + mu * (prev - x)

    def tile(j, g, ci, r):
        hc = slice(j * LANES, (j + 1) * LANES)
        lc = slice(0, B_LORA_W + B_LORA_A + B_LORA_G)
        cr = [None] * 4 if carries is None else [carries[0][j, g], carries[1][j, g], carries[2][j, g],
                                                  carries[3][g]]
        xr = shifted(zr[j][g], sh_r if has_state else None, hc, cr[0], mu_r[:, hc], g, ci, r)
        xk = shifted(zk[j][g], sh_k if has_state else None, hc, cr[1], mu_k[:, hc], g, ci, r)
        xv = shifted(zv[j][g], sh_v if has_state else None, hc, cr[2], mu_v[:, hc], g, ci, r)
        xl = shifted(zl[g], sh_l if has_state else None, lc, cr[3], mu_l[...], g, ci, r)
        wd = xl[:, 0:B_LORA_W]
        ad = xl[:, B_LORA_W:B_LORA_W + B_LORA_A]
        gd = xl[:, B_LORA_W + B_LORA_A:]
        wx = -(w0[:, hc] + _dot(jnp.tanh(wd), w2[:, hc]))
        w_raw = -(jnp.maximum(wx, 0.0) + jnp.log1p(jnp.exp(-jnp.abs(wx)))) - 0.5
        lw = -jnp.exp(w_raw)
        aa = _sigmoid(a0[:, hc] + _dot(ad, a2[:, hc]))
        gb = _dot(_sigmoid(gd), g2[:, hc])
        yield
        kk = xk * kkp[:, hc]
        kk = kk / jnp.maximum(jnp.sqrt(gsum(kk * kk)), 1e-12)
        k2 = xk * (1.0 + (aa - 1.0) * kap[:, hc])
        a = -kk
        b = kk * aa
        yield
        e = _sel_dot(cum[...], lw)
        yield
        lwc = e[0:w]
        lwl = e[w:n]
        dec_in = jnp.exp(lwc)
        dec_ex = jnp.exp(lwc - lw)
        inv = jnp.exp(-lwc)
        rest = jnp.exp(lwl - lwc)
        a_t = a * dec_ex
        r_t = xr * dec_in
        b_t = b * inv
        k_t = k2 * inv
        gram = _dot_nt(jnp.concatenate([stack(a_t), stack(r_t)], axis=0),
                       jnp.concatenate([stack(b_t), stack(k_t)], axis=0))
        yield
        m_ab = gram[0:n, 0:n] * msk[0]
        m_ak = gram[0:n, n:2 * n] * msk[0]
        m_rb = gram[n:2 * n, 0:n] * msk[1]
        m_rk = gram[n:2 * n, n:2 * n] * msk[1]
        p = m_ab
        tinv = eye + p
        for _ in range(nsq):
            p = _dot(p, p)
            yield
            tinv = tinv + _dot(tinv, p)
            yield
        p0a, p0r = [], []
        for s in range(nseq):
            rows = slice(s * c, (s + 1) * c)
            pr = _dot_nt(jnp.concatenate([a_t[rows], r_t[rows]], axis=0), s_ref[j, g * nseq + s])
            p0a.append(pr[0:c])
            p0r.append(pr[c:2 * c])
        p0a = p0a[0] if nseq == 1 else jnp.concatenate(p0a, axis=0)
        p0r = p0r[0] if nseq == 1 else jnp.concatenate(p0r, axis=0)
        yield
        vs = stack(xv)
        rhs = stack(p0a) + _dot(m_ak, vs)
        yield
        us = _dot(tinv, rhs)
        yield
        ys = stack(p0r) + _dot(m_rb, us) + _dot(m_rk, vs)
        yield
        y = fold(ys)
        u_w = fold(us)
        b_g = b * rest
        k_g = k2 * rest
        dec_l = jnp.exp(lwl)
        for s in range(nseq):
            rows = slice(s * c, (s + 1) * c)
            upd = _dot_tn(jnp.concatenate([u_w[rows], xv[rows]], axis=0),
                          jnp.concatenate([b_g[rows], k_g[rows]], axis=0))
            s_ref[j, g * nseq + s] = s_ref[j, g * nseq + s] * dec_l[s * c:s * c + 1, :] + gm_f * upd
        yield
        mean = gsum(y) * (1.0 / B_HEAD)
        yield
        d = y - mean
        var = gsum(d * d) * (1.0 / B_HEAD)
        yn = d * lax.rsqrt(var + RWKV_GN_EPS) * gnw[:, hc] + gnb[:, hc]
        yield
        bonus = gsum(xr * k2 * rkp[:, hc])
        o_ref[g, pl.ds(r, w), hc] = ((yn + bonus * xv) * gb).astype(BF16)

    def chunk(ci, carry):
        r = pl.multiple_of(ci * w, w)
        _round_robin([tile(j, g, ci, r) for j in range(hs) for g in range(g_n)])
        return carry

    lax.fori_loop(0, chunks, chunk, 0)

    base = blk - w
    for g in range(g_n):
        per_head = [(zr[j][g], shift_outs[0], j) for j in range(hs)]
        per_head += [(zk[j][g], shift_outs[1], j) for j in range(hs)]
        per_head += [(zv[j][g], shift_outs[2], j) for j in range(hs)]
        for ref, out, j in per_head + [(zl[g], shift_outs[3], 0)]:
            cols = slice(j * LANES, j * LANES + ref.shape[1])
            for s in range(nseq):
                last = base + (s + 1) * c - 1
                out[g * nseq + s:g * nseq + s + 1, cols] = ref[last:last + 1, :]
        if carries is not None:
            for j in range(hs):
                carries[0][j, g] = zr[j][g][blk - SUBLANES:blk, :]
                carries[1][j, g] = zk[j][g][blk - SUBLANES:blk, :]
                carries[2][j, g] = zv[j][g][blk - SUBLANES:blk, :]
            carries[3][g] = zl[g][blk - SUBLANES:blk, :]

    @pl.when(tb == pl.num_programs(2) - 1)
    def _():
        for j in range(hs):
            for s in range(g_n * nseq):
                ss = s_ref[j, s]
                s_out[0, s, 2 * j] = ss[0:B_HEAD, 0:B_HEAD]
                s_out[0, s, 2 * j + 1] = ss[B_HEAD:n, B_HEAD:n]


def _rwkv(z, p, state, shift, prev_out, layer, depth, b, t, row_off):
    pairs = B_HEADS // 2
    til = _Tiling(b, t, RWKV_ROWS, row_off, pairs, 2 * B_HEAD * B_HEAD * 4)
    cum, msk, gmat = _rwkv_consts(til.c)
    cb = B_OFF // LANES
    wb = B_WIDTH // LANES
    lora_w = B_LORA_W + B_LORA_A + B_LORA_G
    lb_z = (B_OFF + 3 * B_WIDTH) // lora_w
    lb_s = (3 * B_WIDTH) // lora_w

    def vec(x):
        return x.reshape(1, -1)

    def pspec(rows_, col_off=0):
        return til.hspec(rows_, LANES, col_off)

    in_specs, args = [], []
    for col in (cb, cb + wb, cb + 2 * wb):
        for j, g in til.streams():
            in_specs.append(til.zspec(j, g, LANES, col))
            args.append(z)
    for g in range(til.g):
        in_specs.append(til.zspec(0, g, lora_w, lb_z, per_head=0))
        args.append(z)
    in_specs += [
        pspec(1), pspec(1, wb), pspec(1, 2 * wb),
        pl.BlockSpec((1, lora_w), lambda i, h, tb: (0, lb_s)),
        pspec(1), pspec(1), pspec(1), pspec(1), pspec(1), pspec(1), pspec(1),
        pspec(B_LORA_W), pspec(B_LORA_A), pspec(B_LORA_G),
        _const_spec(cum.shape), _const_spec(msk.shape), _const_spec(gmat.shape),
    ]
    mu = vec(p['rwkv_mu'])
    args += [mu, mu, mu, mu,
             vec(p['rwkv_w0']), vec(p['rwkv_a0']), vec(p['rwkv_kk']), vec(p['rwkv_ka']),
             vec(p['rwkv_rk']), vec(p['rwkv_gn_w']), vec(p['rwkv_gn_b']),
             p['rwkv_w2'].astype(BF16), p['rwkv_a2'].astype(BF16), p['rwkv_g2'].astype(BF16),
             jnp.asarray(cum, BF16), jnp.asarray(msk, F32), jnp.asarray(gmat, BF16)]
    ns = til.g * til.nseq
    if state is not None:
        hw = til.hs * LANES
        wbh = wb // til.hs
        in_specs += [
            pl.BlockSpec((ns, hw), lambda i, h, tb: (i, h)),
            pl.BlockSpec((ns, hw), lambda i, h, tb: (i, wbh + h)),
            pl.BlockSpec((ns, hw), lambda i, h, tb: (i, 2 * wbh + h)),
            pl.BlockSpec((ns, lora_w), lambda i, h, tb: (i, lb_s)),
        ]
        args += [shift, shift, shift, shift]
    has_state, has_alias, aliases, s_spec, s_shape = _state_io(
        til, state, prev_out, layer, depth, b, B_HEADS, 2, B_HEAD, B_HEAD, in_specs, args)
    scratch = [pltpu.VMEM((til.hs, til.g * til.nseq, 2 * B_HEAD, 2 * B_HEAD), F32)]
    if not has_state:
        scratch += [pltpu.VMEM((til.hs, til.g, SUBLANES, LANES), F32) for _ in range(3)]
        scratch += [pltpu.VMEM((til.g, SUBLANES, lora_w), F32)]
    o, s, sh_r, sh_k, sh_v, sh_l = pl.pallas_call(
        functools.partial(_rwkv_kernel, g_n=til.g, hs=til.hs, chunks=til.chunks, nseq=til.nseq, c=til.c,
                          has_state=has_state, has_alias=has_alias),
        grid=(til.steps, til.head_steps, til.nt),
        in_specs=in_specs,
        out_specs=[til.ospec(LANES), s_spec,
                   pl.BlockSpec((ns, til.hs * LANES), lambda i, h, tb: (i, h)),
                   pl.BlockSpec((ns, til.hs * LANES), lambda i, h, tb: (i, h)),
                   pl.BlockSpec((ns, til.hs * LANES), lambda i, h, tb: (i, h)),
                   pl.BlockSpec((ns, lora_w), lambda i, h, tb: (i, 0))],
        out_shape=[til.oshape(B_WIDTH), s_shape] + [
            jax.ShapeDtypeStruct((b, wd), F32) for wd in (B_WIDTH, B_WIDTH, B_WIDTH, lora_w)],
        scratch_shapes=scratch,
        input_output_aliases=aliases,
        compiler_params=_cparams(("parallel", "parallel", "arbitrary")),
        name="rwkv7",
    )(*args)
    return o.reshape(b * t, B_WIDTH), s, jnp.concatenate([sh_r, sh_k, sh_v, sh_l], axis=1)


def _merge_kernel(oa, ob, oc, wa, wb, wc, ga, gb, gc, o_ref):
    acc = _sigmoid(ga[...]) * jnp.dot(oa[...], wa[...], preferred_element_type=F32)
    acc = acc + _sigmoid(gb[...]) * jnp.dot(ob[...], wb[...], preferred_element_type=F32)
    acc = acc + _sigmoid(gc[...]) * jnp.dot(oc[...], wc[...], preferred_element_type=F32)
    o_ref[...] = acc.astype(BF16)


def _merge(z, oa, ob, oc, wa, wb, wc, layer):
    m = oa.shape[0]
    tm = _pick(m, (1024, 512, 256, 128))
    tn = 256
    gblk = G_OFF // tn
    nb = D_MODEL // tn
    assert G_OFF % tn == 0

    def ospec():
        return pl.BlockSpec((tm, oa.shape[1]), lambda i, j: (i, 0))

    def wspec():
        return pl.BlockSpec((None, wa.shape[1], tn), lambda i, j: (layer, 0, j))

    def gspec(br):
        return pl.BlockSpec((tm, tn), lambda i, j: (i, gblk + br * nb + j))

    return pl.pallas_call(
        _merge_kernel,
        grid=(m // tm, nb),
        in_specs=[ospec(), ospec(), ospec(), wspec(), wspec(), wspec(), gspec(0), gspec(1), gspec(2)],
        out_specs=pl.BlockSpec((tm, tn), lambda i, j: (i, j)),
        out_shape=jax.ShapeDtypeStruct((m, D_MODEL), BF16),
        compiler_params=_cparams(("parallel", "arbitrary")),
        name="merge",
    )(oa, ob, oc, wa, wb, wc, z, z, z)


def _proj_res_kernel(m_ref, w_ref, g_ref, x_ref, o_ref):
    y = jnp.dot(m_ref[...], w_ref[...], preferred_element_type=F32)
    yn = y * lax.rsqrt(jnp.mean(y * y, axis=-1, keepdims=True) + NORM_EPS) * g_ref[...]
    o_ref[...] = x_ref[...] + yn


def _proj_res(mrg, w, g, x, layer):
    m = x.shape[0]
    tm = _pick(m, (512, 256, 128))
    return pl.pallas_call(
        _proj_res_kernel,
        grid=(m // tm,),
        in_specs=[
            pl.BlockSpec((tm, D_MODEL), lambda i: (i, 0)),
            pl.BlockSpec((None, D_MODEL, D_MODEL), lambda i: (layer, 0, 0)),
            pl.BlockSpec((1, D_MODEL), lambda i: (0, 0)),
            pl.BlockSpec((tm, D_MODEL), lambda i: (i, 0)),
        ],
        out_specs=pl.BlockSpec((tm, D_MODEL), lambda i: (i, 0)),
        out_shape=jax.ShapeDtypeStruct((m, D_MODEL), F32),
        compiler_params=_cparams(("parallel",)),
        name="proj_res",
    )(mrg, w, g.reshape(1, D_MODEL), x)


def _gelu(x):
    return 0.5 * x * (1.0 + jnp.tanh(math.sqrt(2.0 / math.pi) * (x + 0.044715 * (x * x * x))))


def _ffn_down_kernel(*refs, tm, t, blocks_per_seq, has_state, has_alias):
    it = iter(refs)
    ua, ub = next(it), next(it)
    halo = None if has_state else next(it)
    cw, cb, wd, g_ref, x_ref = (next(it) for _ in range(5))
    st = next(it) if has_state else None
    if has_alias:
        next(it)
    o_ref, nc_ref, acc, act = (next(it) for _ in range(4))
    kstep = pl.program_id(1)
    tk = ua.shape[1]

    @pl.when(kstep == 0)
    def _():
        acc[...] = jnp.zeros_like(acc)

    if has_state:
        ns = tm // t
        tt = lax.broadcasted_iota(jnp.int32, (ns, t, LANES), 1)
    else:
        seq_start = (pl.program_id(0) % blocks_per_seq) == 0
        rr = lax.broadcasted_iota(jnp.int32, (tm, LANES), 0)
    for j in range(tk // LANES):
        cols = slice(j * LANES, (j + 1) * LANES)
        x = ua[:, cols]
        if has_state:
            x3 = x.reshape(ns, t, LANES)
            s_old = st[:, 0:1, cols]
            s_new = st[:, 1:2, cols]
            prev1 = jnp.where(tt >= 1, pltpu.roll(x3, 1, axis=1), s_new)
            prev2 = jnp.where(tt >= 2, pltpu.roll(x3, 2, axis=1), jnp.where(tt == 1, s_new, s_old))
            prev1 = prev1.reshape(tm, LANES)
            prev2 = prev2.reshape(tm, LANES)
            nc_ref[:, :, cols] = x3[:, t - (CONV_W - 1):, :]
        else:
            h = jnp.where(seq_start, 0.0, halo[:, cols])
            h1 = h[SUBLANES - 1:SUBLANES, :]
            h2 = h[SUBLANES - 2:SUBLANES - 1, :]
            prev1 = jnp.where(rr == 0, h1, pltpu.roll(x, 1, axis=0))
            prev2 = jnp.where(rr == 0, h2, jnp.where(rr == 1, h1, pltpu.roll(x, 2, axis=0)))
            nc_ref[0, :, cols] = x[tm - (CONV_W - 1):, :]
        conv = cb[:, cols] + cw[0:1, cols] * prev2 + cw[1:2, cols] * prev1 + cw[2:3, cols] * x
        act[:, cols] = (_gelu(conv) * ub[:, cols]).astype(BF16)
    acc[...] += jnp.dot(act[...], wd[...], preferred_element_type=F32)

    @pl.when(kstep == pl.num_programs(1) - 1)
    def _():
        y = acc[...]
        yn = y * lax.rsqrt(jnp.mean(y * y, axis=-1, keepdims=True) + NORM_EPS) * g_ref[...]
        o_ref[...] = x_ref[...] + yn


def _ffn_down(u, x1, cw, cb, wd, g, state, prev_out, layer, b, t, row_off):
    m = b * t
    total = x1.shape[0]
    has_state = state is not None
    has_alias = prev_out is not None
    tk = FFN_TK
    nk = D_FF // tk
    if has_state:
        tm = _pick(m, (512, 256, 128))
        assert tm % t == 0 and t == SUBLANES
        blocks_per_seq = 1
        nc_spec = pl.BlockSpec((tm // t, CONV_W - 1, tk), lambda i, k: (i, 0, k))
    else:
        tm = _pick(t, (512, 256, 128))
        blocks_per_seq = t // tm
        nc_spec = pl.BlockSpec((1, CONV_W - 1, tk), lambda i, k: (i // blocks_per_seq, 0, k))
    off = row_off // tm
    assert row_off % tm == 0
    in_specs = [
        pl.BlockSpec((tm, tk), lambda i, k: (i + off, k)),
        pl.BlockSpec((tm, tk), lambda i, k: (i + off, nk + k)),
    ]
    args = [u, u]
    if not has_state:
        hb = tm // SUBLANES
        hoff = row_off // SUBLANES
        in_specs.append(pl.BlockSpec((SUBLANES, tk), lambda i, k: (jnp.maximum(i * hb + hoff - 1, 0), k)))
        args.append(u)
    in_specs += [
        pl.BlockSpec((CONV_W, tk), lambda i, k: (0, k)),
        pl.BlockSpec((1, tk), lambda i, k: (0, k)),
        pl.BlockSpec((None, tk, D_MODEL), lambda i, k: (layer, k, 0)),
        pl.BlockSpec((1, D_MODEL), lambda i, k: (0, 0)),
        pl.BlockSpec((tm, D_MODEL), lambda i, k: (i + off, 0)),
    ]
    args += [cw, cb.reshape(1, D_FF), wd, g.reshape(1, D_MODEL), x1]
    if has_state:
        in_specs.append(pl.BlockSpec((tm // t, CONV_W - 1, tk), lambda i, k: (i, 0, k)))
        args.append(state)
    aliases = {}
    if has_alias:
        aliases = {len(args): 0}
        in_specs.append(pl.BlockSpec(memory_space=pl.ANY))
        args.append(prev_out)
    return pl.pallas_call(
        functools.partial(_ffn_down_kernel, tm=tm, t=t, blocks_per_seq=blocks_per_seq,
                          has_state=has_state, has_alias=has_alias),
        grid=(m // tm, nk),
        in_specs=in_specs,
        out_specs=[pl.BlockSpec((tm, D_MODEL), lambda i, k: (i + off, 0)), nc_spec],
        out_shape=[jax.ShapeDtypeStruct((total, D_MODEL), F32),
                   jax.ShapeDtypeStruct((b, CONV_W - 1, D_FF), F32)],
        scratch_shapes=[pltpu.VMEM((tm, D_MODEL), F32), pltpu.VMEM((tm, tk), BF16)],
        input_output_aliases=aliases,
        compiler_params=_cparams(("arbitrary", "arbitrary")),
        name="ffn_down",
    )(*args)


def _layer(x, groups, lb, p, layer, depth, prev):
    z = _rms_matmul(x, p['pre_mix_g'], p['w_in'], layer, tn=1280)
    oa, ob, oc, states = [], [], [], []
    row = 0
    for gi, (b, t, t0, st) in enumerate(groups):
        s_a, s_b, s_sh, s_c, _ = st if st is not None else (None,) * 5
        pv = prev[gi] if prev is not None else (None,) * 3
        o_a, n_a = _hgrn(z, lb, p['a_norm_g'], s_a, pv[0], layer, depth, b, t, row)
        o_b, n_b, n_sh = _rwkv(z, p, s_b, s_sh, pv[1], layer, depth, b, t, row)
        o_c, n_c = _retention(z, s_c, pv[2], layer, depth, b, t, t0, row)
        oa.append(o_a)
        ob.append(o_b)
        oc.append(o_c)
        states.append([n_a, n_b, n_sh, n_c])
        row += b * t
    oa, ob, oc = (jnp.concatenate(v, axis=0) for v in (oa, ob, oc))
    mrg = _merge(z, oa, ob, oc, p['w_br_a'], p['w_br_b'], p['w_br_c'], layer)
    x1 = _proj_res(mrg, p['w_out'], p['post_mix_g'], x, layer)
    u = _rms_matmul(x1, p['pre_ffn_g'], p['w_up'], layer, tn=1024)
    x2 = None
    row = 0
    for gi, (b, t, t0, st) in enumerate(groups):
        s_cv = st[4] if st is not None else None
        x2, n_cv = _ffn_down(u, x1, p['conv_w'], p['conv_b'], p['w_down'], p['post_ffn_g'],
                             s_cv, x2, layer, b, t, row)
        states[gi].append(n_cv)
        row += b * t
    return x2, states


def kernel(x_prompt, x_sample, state_hgrn, state_rwkv, state_rwkv_shift, state_ret, state_conv,
           lb_logits, pre_mix_g, w_in, a_norm_g, rwkv_mu, rwkv_w0, rwkv_w2, rwkv_a0, rwkv_a2,
           rwkv_g2, rwkv_kk, rwkv_ka, rwkv_rk, rwkv_gn_w, rwkv_gn_b, w_br_a, w_br_b, w_br_c,
           w_out, post_mix_g, pre_ffn_g, w_up, conv_w, conv_b, w_down, post_ffn_g):
    depth = w_in.shape[0]
    bp, tp, _ = x_prompt.shape
    bs, ts, _ = x_sample.shape
    past_len = 16384
    lb_soft = jax.nn.softmax(lb_logits.astype(F32), axis=0)
    lbs = jnp.cumsum(lb_soft, axis=0) - lb_soft[0]
    big = {'w_in': w_in.astype(BF16), 'w_br_a': w_br_a.astype(BF16), 'w_br_b': w_br_b.astype(BF16),
           'w_br_c': w_br_c.astype(BF16), 'w_out': w_out.astype(BF16), 'w_up': w_up.astype(BF16),
           'w_down': w_down.astype(BF16)}
    x = jnp.concatenate([x_prompt.reshape(bp * tp, D_MODEL), x_sample.reshape(bs * ts, D_MODEL)], axis=0)
    small = [[[], []], [[], []]]
    prev = None
    for l in range(depth):
        p = dict(big)
        p.update({
            'pre_mix_g': pre_mix_g[l], 'a_norm_g': a_norm_g[l],
            'rwkv_mu': rwkv_mu[l], 'rwkv_w0': rwkv_w0[l], 'rwkv_w2': rwkv_w2[l],
            'rwkv_a0': rwkv_a0[l], 'rwkv_a2': rwkv_a2[l], 'rwkv_g2': rwkv_g2[l],
            'rwkv_kk': rwkv_kk[l], 'rwkv_ka': rwkv_ka[l], 'rwkv_rk': rwkv_rk[l],
            'rwkv_gn_w': rwkv_gn_w[l], 'rwkv_gn_b': rwkv_gn_b[l],
            'post_mix_g': post_mix_g[l], 'pre_ffn_g': pre_ffn_g[l],
            'conv_w': conv_w[l], 'conv_b': conv_b[l], 'post_ffn_g': post_ffn_g[l],
        })
        groups = [
            (bp, tp, 0, None),
            (bs, ts, past_len, (state_hgrn, state_rwkv, state_rwkv_shift[l], state_ret, state_conv[l])),
        ]
        x, states = _layer(x, groups, lbs[l], p, l, depth, prev)
        prev = [(st[0], st[1], st[3]) for st in states]
        for gi, st in enumerate(states):
            small[gi][0].append(st[2])
            small[gi][1].append(st[4])
    y_p = x[:bp * tp].reshape(bp, tp, D_MODEL)
    y_s = x[bp * tp:].reshape(bs, ts, D_MODEL)
    outs = []
    for gi in range(2):
        outs += [prev[gi][0], prev[gi][1], jnp.stack(small[gi][0]), prev[gi][2], jnp.stack(small[gi][1])]
    return (y_p, y_s, *outs)
```

```python
import functools
import itertools
import math

import jax
import jax.numpy as jnp
import numpy as np
from jax import lax
from jax.experimental import pallas as pl
from jax.experimental.pallas import tpu as pltpu

F32 = jnp.float32
BF16 = jnp.bfloat16

D_MODEL = 2048
A_HEADS, A_DK, A_DV = 8, 128, 128
A_QK = A_HEADS * A_DK
A_WIDTH = A_HEADS * A_DV
F_TINY = 1e-30
B_HEAD = 64
B_WIDTH = 1024
B_HEADS = B_WIDTH // B_HEAD
B_LORA_W, B_LORA_A, B_LORA_G = 64, 64, 128
RWKV_GN_EPS = 64e-5
C_HEADS, C_DK, C_DV = 4, 128, 256
C_QK = C_HEADS * C_DK
C_WIDTH = C_HEADS * C_DV
ROPE_BASE = 10000.0
A_COLS = 2 * A_QK + 2 * A_WIDTH
B_COLS = 3 * B_WIDTH + B_LORA_W + B_LORA_A + B_LORA_G
C_COLS = 2 * C_QK + 2 * C_WIDTH
N_BRANCH = 3
P_COLS = A_COLS + B_COLS + C_COLS + N_BRANCH * D_MODEL
B_OFF = A_COLS
C_OFF = A_COLS + B_COLS
G_OFF = A_COLS + B_COLS + C_COLS
D_FF = 5632
CONV_W = 3
NORM_EPS = 1e-6

LANES = 128
SUBLANES = 8
MIX_ROWS = 128
RWKV_ROWS = 64
MIX_TIME_BLOCK = 512
MIX_STREAMS = 4
MIX_STREAMS_TOTAL = 8
STATE_WINDOW_BYTES = 4 * 1024 * 1024
UP_TN = 512
UP_SUB = 256
DOWN_TK = 2816
VMEM_LIMIT = 56 * 1024 * 1024


def _cparams(sem):
    return pltpu.CompilerParams(dimension_semantics=sem, vmem_limit_bytes=VMEM_LIMIT)


def _dot(a, b):
    return jnp.dot(a.astype(BF16), b.astype(BF16), preferred_element_type=F32)


def _dot_nt(a, b):
    return lax.dot_general(a.astype(BF16), b.astype(BF16), (((1,), (1,)), ((), ())),
                           preferred_element_type=F32)


def _dot_tn(a, b):
    return lax.dot_general(a.astype(BF16), b.astype(BF16), (((0,), (0,)), ((), ())),
                           preferred_element_type=F32)


def _split(x):
    hi = x.astype(BF16)
    lo = (x - hi.astype(F32)).astype(BF16)
    return hi, lo


def _sel_dot(m, x):
    hi, lo = _split(x)
    n = x.shape[1]
    both = jnp.dot(m, jnp.concatenate([hi, lo], axis=1), preferred_element_type=F32)
    return both[:, :n] + both[:, n:]


def _dot_sel(x, m):
    hi, lo = _split(x)
    return (jnp.dot(hi, m, preferred_element_type=F32)
            + jnp.dot(lo, m, preferred_element_type=F32))


def _sigmoid(x):
    return jax.nn.sigmoid(x)


def _silu(x):
    return x * jax.nn.sigmoid(x)


def _round_robin(gens):
    for _ in itertools.zip_longest(*gens):
        pass


def _pick(n, cands):
    for c in cands:
        if n % c == 0:
            return c
    raise ValueError(f"no tile in {cands} divides {n}")


def _rms_matmul_kernel(x_ref, g_ref, w_ref, o_ref, xn_ref, *, tm, sub):
    @pl.when(pl.program_id(1) == 0)
    def _():
        def body(i, carry):
            r = pl.multiple_of(i * sub, sub)
            x = x_ref[pl.ds(r, sub), :]
            ms = jnp.mean(x * x, axis=-1, keepdims=True)
            xn_ref[pl.ds(r, sub), :] = (x * lax.rsqrt(ms + NORM_EPS) * g_ref[...]).astype(BF16)
            return carry
        lax.fori_loop(0, tm // sub, body, 0)

    o_ref[...] = jnp.dot(xn_ref[...], w_ref[...], preferred_element_type=F32)


def _rms_matmul(x, g, w, layer, tn):
    m, k = x.shape
    n = w.shape[2]
    tm = _pick(m, (1024, 512, 256, 128))
    sub = min(tm, 128)
    return pl.pallas_call(
        functools.partial(_rms_matmul_kernel, tm=tm, sub=sub),
        grid=(m // tm, n // tn),
        in_specs=[
            pl.BlockSpec((tm, k), lambda i, j: (i, 0)),
            pl.BlockSpec((1, k), lambda i, j: (0, 0)),
            pl.BlockSpec((None, k, tn), lambda i, j: (layer, 0, j)),
        ],
        out_specs=pl.BlockSpec((tm, tn), lambda i, j: (i, j)),
        out_shape=jax.ShapeDtypeStruct((m, n), F32),
        scratch_shapes=[pltpu.VMEM((tm, k), BF16)],
        compiler_params=_cparams(("parallel", "arbitrary")),
        name="rms_matmul",
    )(x, g.reshape(1, k), w)


class _Tiling:
    def __init__(self, b, t, rows, row_off, heads, state_bytes):
        if t >= rows:
            self.nseq, self.c = 1, rows
            self.blk = _pick(t, (MIX_TIME_BLOCK, rows))
            self.nt = t // self.blk
            n_streams = b
        else:
            assert rows % t == 0 and b % (rows // t) == 0
            self.nseq, self.c = rows // t, t
            self.blk, self.nt = rows, 1
            n_streams = b // self.nseq
        self.g = _pick(n_streams, (MIX_STREAMS, 2, 1))
        self.hs = _pick(heads, (max(MIX_STREAMS_TOTAL // self.g, 1), 1))
        while self.g * self.nseq * self.hs * state_bytes > STATE_WINDOW_BYTES and self.hs > 1:
            self.hs //= 2
        while self.g * self.nseq * self.hs * state_bytes > STATE_WINDOW_BYTES and self.g > 1:
            self.g //= 2
        self.head_steps = heads // self.hs
        self.steps = n_streams // self.g
        self.n_streams = n_streams
        self.chunks = self.blk // rows
        assert row_off % self.blk == 0
        self.off = row_off // self.blk

    def streams(self):
        return [(j, g) for j in range(self.hs) for g in range(self.g)]

    def zspec(self, j, stream, width, col_blk, per_head=1):
        g, nt, off, hs = self.g, self.nt, self.off, self.hs
        return pl.BlockSpec(
            (self.blk, width),
            lambda i, h, tb: (off + (i * g + stream) * nt + tb, col_blk + per_head * (h * hs + j)))

    def hspec(self, rows, width, blk_off=0):
        assert blk_off % self.hs == 0
        off = blk_off // self.hs
        return pl.BlockSpec((rows, self.hs * width), lambda i, h, tb: (0, off + h))

    def ospec(self, width):
        return pl.BlockSpec((self.g, self.blk, self.hs * width), lambda i, h, tb: (i, tb, h))

    def oshape(self, width):
        return jax.ShapeDtypeStruct((self.n_streams, self.nt * self.blk, width), BF16)

    def sspec(self, layer, heads_per_step, d0, d1):
        return pl.BlockSpec((1, self.g * self.nseq, self.hs * heads_per_step, d0, d1),
                            lambda i, h, tb: (layer, i, h, 0, 0))


def _const_spec(shape):
    nd = len(shape)
    return pl.BlockSpec(shape, lambda i, h, tb: (0,) * nd)


def _state_io(til, state, prev_out, layer, depth, b, heads, heads_per_step, d0, d1,
              in_specs, args):
    has_state = state is not None
    if has_state:
        in_specs.append(til.sspec(layer, heads_per_step, d0, d1))
        args.append(state)
    aliases = {}
    if prev_out is not None:
        aliases = {len(args): 1}
        in_specs.append(pl.BlockSpec(memory_space=pl.ANY))
        args.append(prev_out)
    out_spec = til.sspec(layer, heads_per_step, d0, d1)
    out_shape = jax.ShapeDtypeStruct((depth, b, heads, d0, d1), F32)
    return has_state, prev_out is not None, aliases, out_spec, out_shape


@functools.lru_cache(maxsize=None)
def _hgrn_consts(c):
    n = MIX_ROWS
    nlev = int(math.log2(c))
    t = np.arange(n)
    u = np.arange(n)[None, :]
    blk = t // c
    same = blk[:, None] == blk[None, :]
    mats = [same & (u <= t[:, None]), same]
    masks = [np.eye(n, dtype=bool)]
    for lev in range(nlev):
        h = 1 << lev
        base = (t // (2 * h)) * (2 * h)
        mid = base + h
        upper = t >= mid
        e_up = (u >= mid[:, None]) & (u <= t[:, None])
        e_lo = (u >= t[:, None] + 1) & (u <= mid[:, None] - 1)
        mats.append(np.where(upper[:, None], e_up, e_lo))
        masks.append((base[:, None] == base[None, :]) & upper[:, None] & (~upper)[None, :])
    sel = np.concatenate(mats, 0).astype(np.float32)
    msk = np.stack(masks).astype(np.float32)
    return sel, msk, nlev


def _hgrn_kernel(*refs, g_n, hs, chunks, nseq, c, nlev, has_state, has_alias):
    it = iter(refs)
    zq, zf, zi, zg = ([[next(it) for _ in range(g_n)] for _ in range(hs)] for _ in range(4))
    lb, gn, sel, msk = (next(it) for _ in range(4))
    s0 = next(it) if has_state else None
    if has_alias:
        next(it)
    o_ref, s_out, s_ref = next(it), next(it), next(it)
    n = MIX_ROWS
    tb = pl.program_id(2)

    @pl.when(tb == 0)
    def _():
        if has_state:
            for j in range(hs):
                s_ref[j] = s0[0, :, j]
        else:
            s_ref[...] = jnp.zeros_like(s_ref)

    def tile(j, g, r):
        hcols = slice(j * LANES, (j + 1) * LANES)
        lbv = lb[:, hcols]
        xq = zq[j][g][pl.ds(r, n), :]
        fa = zf[j][g][pl.ds(r, n), :]
        v = zi[j][g][pl.ds(r, n), :]
        xg = zg[j][g][pl.ds(r, n), :]
        q = _silu(xq)
        f_gate = lbv + (1.0 - lbv) * _sigmoid(fa)
        gl = jnp.log(jnp.maximum(f_gate, F_TINY))
        k = (1.0 - lbv) * _sigmoid(-fa)
        e = _sel_dot(sel[...], gl)
        yield
        b = e[0:n]
        bl = e[n:2 * n]
        scores = msk[0] * _dot_nt(q, k)
        for lev in range(nlev):
            x = jnp.exp(e[(lev + 2) * n:(lev + 3) * n])
            scores = scores + msk[lev + 1] * _dot_nt(q * x, k * x)
        yield
        o = _dot(scores, v)
        yield
        qe = q * jnp.exp(b)
        kt = k * jnp.exp(bl - b)
        dt = jnp.exp(bl).T
        outs = []
        for s in range(nseq):
            rows = slice(s * c, (s + 1) * c)
            ss = s_ref[j, g * nseq + s]
            outs.append(_dot(qe[rows], ss))
            dcol = jnp.broadcast_to(dt[:, s * c:s * c + 1], (A_DK, A_DV))
            s_ref[j, g * nseq + s] = ss * dcol + _dot_tn(kt[rows], v[rows])
        o = o + (outs[0] if nseq == 1 else jnp.concatenate(outs, axis=0))
        on = o * lax.rsqrt(jnp.mean(o * o, axis=-1, keepdims=True) + NORM_EPS) * gn[:, hcols]
        o_ref[g, pl.ds(r, n), hcols] = (on * _silu(xg)).astype(BF16)

    def chunk(ci, carry):
        r = pl.multiple_of(ci * n, n)
        _round_robin([tile(j, g, r) for j in range(hs) for g in range(g_n)])
        return carry

    lax.fori_loop(0, chunks, chunk, 0)

    @pl.when(tb == pl.num_programs(2) - 1)
    def _():
        for j in range(hs):
            s_out[0, :, j] = s_ref[j]


def _hgrn(z, lb, gn, state, prev_out, layer, depth, b, t, row_off):
    til = _Tiling(b, t, MIX_ROWS, row_off, A_HEADS, A_DK * A_DV * 4)
    sel, msk, nlev = _hgrn_consts(til.c)
    qk_blocks = A_QK // LANES
    in_specs, args = [], []
    for col in range(4):
        for j, g in til.streams():
            in_specs.append(til.zspec(j, g, LANES, col * qk_blocks))
            args.append(z)
    in_specs += [
        til.hspec(1, LANES), til.hspec(1, LANES),
        _const_spec(sel.shape), _const_spec(msk.shape),
    ]
    args += [lb.reshape(1, A_QK), gn.reshape(1, A_WIDTH), jnp.asarray(sel, BF16), jnp.asarray(msk, F32)]
    has_state, has_alias, aliases, s_spec, s_shape = _state_io(
        til, state, prev_out, layer, depth, b, A_HEADS, 1, A_DK, A_DV, in_specs, args)
    o, s = pl.pallas_call(
        functools.partial(_hgrn_kernel, g_n=til.g, hs=til.hs, chunks=til.chunks, nseq=til.nseq, c=til.c,
                          nlev=nlev, has_state=has_state, has_alias=has_alias),
        grid=(til.steps, til.head_steps, til.nt),
        in_specs=in_specs,
        out_specs=[til.ospec(LANES), s_spec],
        out_shape=[til.oshape(A_WIDTH), s_shape],
        scratch_shapes=[pltpu.VMEM((til.hs, til.g * til.nseq, A_DK, A_DV), F32)],
        input_output_aliases=aliases,
        compiler_params=_cparams(("parallel", "parallel", "arbitrary")),
        name="hgrn2",
    )(*args)
    return o.reshape(b * t, A_WIDTH), s


def _ret_tables(c):
    n = MIX_ROWS
    log_g = jnp.log1p(-jnp.exp2(-5.0 - jnp.arange(C_HEADS, dtype=F32)))
    t = np.arange(n)
    tt = (t % c).astype(np.float32)
    blk = t // c
    rel = tt[:, None] - tt[None, :]
    same = (blk[:, None] == blk[None, :]) & (rel >= 0)
    dmat = jnp.where(same[None], jnp.exp(log_g[:, None, None] * np.maximum(rel, 0.0)[None]), 0.0)
    inner = jnp.exp(log_g[:, None] * (tt[None, :] + 1.0))
    tail = jnp.exp(log_g[:, None] * (c - 1.0 - tt[None, :]))
    total = jnp.exp(log_g * c)
    shape = (C_HEADS, n, n)
    tab = jnp.stack([dmat, jnp.broadcast_to(inner[:, :, None], shape),
                     jnp.broadcast_to(tail[:, :, None], shape)], axis=1)
    tot = jnp.broadcast_to(total[:, None, None], (C_HEADS, 1, C_DV))
    return tab.astype(F32), tot.astype(F32)


def _rope_tables(t0, t, reps):
    half = C_DK // 2
    inv = ROPE_BASE ** (-jnp.arange(half, dtype=F32) / half)
    pos = t0 + jnp.arange(t, dtype=F32)
    ang = pos[:, None] * inv[None, :]
    cos, sin = jnp.cos(ang), jnp.sin(ang)
    cosf = jnp.concatenate([cos, cos], axis=-1)
    sinf = jnp.concatenate([-sin, sin], axis=-1)
    return jnp.tile(cosf, (reps, 1)), jnp.tile(sinf, (reps, 1))


def _ret_kernel(*refs, g_n, hs, chunks, nseq, c, has_state, has_alias):
    it = iter(refs)
    zq, zk, zv, zg = ([[next(it) for _ in range(g_n)] for _ in range(hs)] for _ in range(4))
    cos, sin, tab, tot = (next(it) for _ in range(4))
    s0 = next(it) if has_state else None
    if has_alias:
        next(it)
    o_ref, s_out, s_ref = next(it), next(it), next(it)
    n = MIX_ROWS
    half = C_DK // 2
    tb = pl.program_id(2)

    @pl.when(tb == 0)
    def _():
        if has_state:
            for j in range(hs):
                s_ref[j] = s0[0, :, j]
        else:
            s_ref[...] = jnp.zeros_like(s_ref)

    def tile(j, g, r, cs, sn):
        xq = zq[j][g][pl.ds(r, n), :]
        xk = zk[j][g][pl.ds(r, n), :]
        v = zv[j][g][pl.ds(r, n), :]
        xg = zg[j][g][pl.ds(r, n), :]
        q = xq * cs + pltpu.roll(xq, half, axis=1) * sn
        k = (xk * cs + pltpu.roll(xk, half, axis=1) * sn) * (C_DK ** -0.5)
        scores = _dot_nt(q, k) * tab[j, 0]
        yield
        o = _dot(scores, v)
        yield
        qi = q * tab[j, 1]
        ktl = k * tab[j, 2]
        outs = []
        for s in range(nseq):
            rows = slice(s * c, (s + 1) * c)
            ss = s_ref[j, g * nseq + s]
            outs.append(_dot(qi[rows], ss))
            s_ref[j, g * nseq + s] = tot[j] * ss + _dot_tn(ktl[rows], v[rows])
        o = o + (outs[0] if nseq == 1 else jnp.concatenate(outs, axis=0))
        on = o * lax.rsqrt(jnp.mean(o * o, axis=-1, keepdims=True) + NORM_EPS)
        o_ref[g, pl.ds(r, n), j * C_DV:(j + 1) * C_DV] = (on * _silu(xg)).astype(BF16)

    def chunk(ci, carry):
        r = pl.multiple_of(ci * n, n)
        cs = cos[pl.ds(r, n), :]
        sn = sin[pl.ds(r, n), :]
        _round_robin([tile(j, g, r, cs, sn) for j in range(hs) for g in range(g_n)])
        return carry

    lax.fori_loop(0, chunks, chunk, 0)

    @pl.when(tb == pl.num_programs(2) - 1)
    def _():
        for j in range(hs):
            s_out[0, :, j] = s_ref[j]


def _retention(z, state, prev_out, layer, depth, b, t, t0, row_off):
    til = _Tiling(b, t, MIX_ROWS, row_off, C_HEADS, C_DK * C_DV * 4)
    tab, tot = _ret_tables(til.c)
    cosf, sinf = _rope_tables(t0, t, max(MIX_ROWS // t, 1))
    qb = C_OFF // C_DK
    vb = (C_OFF + 2 * C_QK) // C_DV
    gb = (C_OFF + 2 * C_QK + C_WIDTH) // C_DV
    in_specs, args = [], []
    for width, col in ((C_DK, qb), (C_DK, qb + C_HEADS), (C_DV, vb), (C_DV, gb)):
        for j, g in til.streams():
            in_specs.append(til.zspec(j, g, width, col))
            args.append(z)
    in_specs += [
        pl.BlockSpec((til.blk, C_DK), lambda i, h, tb: (tb, 0)),
        pl.BlockSpec((til.blk, C_DK), lambda i, h, tb: (tb, 0)),
        pl.BlockSpec((til.hs, 3, MIX_ROWS, MIX_ROWS), lambda i, h, tb: (h, 0, 0, 0)),
        pl.BlockSpec((til.hs, 1, C_DV), lambda i, h, tb: (h, 0, 0)),
    ]
    args += [cosf, sinf, tab, tot]
    has_state, has_alias, aliases, s_spec, s_shape = _state_io(
        til, state, prev_out, layer, depth, b, C_HEADS, 1, C_DK, C_DV, in_specs, args)
    o, s = pl.pallas_call(
        functools.partial(_ret_kernel, g_n=til.g, hs=til.hs, chunks=til.chunks, nseq=til.nseq, c=til.c,
                          has_state=has_state, has_alias=has_alias),
        grid=(til.steps, til.head_steps, til.nt),
        in_specs=in_specs,
        out_specs=[til.ospec(C_DV), s_spec],
        out_shape=[til.oshape(C_WIDTH), s_shape],
        scratch_shapes=[pltpu.VMEM((til.hs, til.g * til.nseq, C_DK, C_DV), F32)],
        input_output_aliases=aliases,
        compiler_params=_cparams(("parallel", "parallel", "arbitrary")),
        name="retention",
    )(*args)
    return o.reshape(b * t, C_WIDTH), s


@functools.lru_cache(maxsize=None)
def _rwkv_consts(c):
    w = RWKV_ROWS
    t = np.arange(w)
    blk = t // c
    same = blk[:, None] == blk[None, :]
    tri = same & (t[None, :] <= t[:, None])
    cum = np.concatenate([tri, same], 0).astype(np.float32)
    rr = np.arange(2 * w)
    grp = rr // c
    tt = rr % c
    sameg = grp[:, None] == grp[None, :]
    strict = sameg & (tt[None, :] < tt[:, None])
    incl = sameg & (tt[None, :] <= tt[:, None])
    masks = np.stack([strict, incl]).astype(np.float32)
    hh = rr // B_HEAD
    gmat = (hh[:, None] == hh[None, :]).astype(np.float32)
    return cum, masks, gmat


def _rwkv_kernel(*refs, g_n, hs, chunks, nseq, c, has_state, has_alias):
    it = iter(refs)
    zr, zk, zv = ([[next(it) for _ in range(g_n)] for _ in range(hs)] for _ in range(3))
    zl = [next(it) for _ in range(g_n)]
    (mu_r, mu_k, mu_v, mu_l, w0, a0, kkp, kap, rkp, gnw, gnb,
     w2, a2, g2, cum, msk, gmat) = (next(it) for _ in range(17))
    if has_state:
        sh_r, sh_k, sh_v, sh_l, s0 = (next(it) for _ in range(5))
    if has_alias:
        next(it)
    o_ref, s_out = next(it), next(it)
    shift_outs = [next(it) for _ in range(4)]
    s_ref = next(it)
    carries = None if has_state else [next(it) for _ in range(4)]
    w = RWKV_ROWS
    n = 2 * w
    blk = chunks * w
    nsq = int(math.log2(c)) - 1
    tb = pl.program_id(2)
    lane = lax.broadcasted_iota(jnp.int32, (w, LANES), 1)
    head0 = lane < B_HEAD
    row = lax.broadcasted_iota(jnp.int32, (w, 1), 0)
    first = (row % c) == 0
    gm = gmat[...]
    gm_f = gm.astype(F32)
    eye = (lax.broadcasted_iota(jnp.int32, (n, n), 0)
           == lax.broadcasted_iota(jnp.int32, (n, n), 1)).astype(F32)

    @pl.when(tb == 0)
    def _():
        if has_state:
            zero_blk = jnp.zeros((B_HEAD, B_HEAD), F32)
            for j in range(hs):
                for s in range(g_n * nseq):
                    top = jnp.concatenate([s0[0, s, 2 * j], zero_blk], axis=1)
                    bot = jnp.concatenate([zero_blk, s0[0, s, 2 * j + 1]], axis=1)
                    s_ref[j, s] = jnp.concatenate([top, bot], axis=0)
        else:
            s_ref[...] = jnp.zeros_like(s_ref)
            for cr in carries:
                cr[...] = jnp.zeros_like(cr)

    def stack(x):
        return jnp.concatenate([jnp.where(head0, x, 0.0), jnp.where(head0, 0.0, x)], axis=0)

    def fold(x):
        return x[0:w] + x[w:n]

    def gsum(x):
        return _dot_sel(x, gm)

    def shifted(ref, sh, cols, carry, mu, g, ci, r):
        x = ref[pl.ds(r, w), :]
        width = x.shape[1]
        if has_state:
            src = jnp.concatenate(
                [jnp.broadcast_to(sh[g * nseq + s:g * nseq + s + 1, cols], (c, width))
                 for s in range(nseq)], axis=0)
        else:
            rp = pl.multiple_of(jnp.maximum(r - SUBLANES, 0), SUBLANES)
            prev8 = jnp.where(ci == 0, carry, ref[pl.ds(rp, SUBLANES), :])
            src = jnp.broadcast_to(prev8[SUBLANES - 1:SUBLANES, :], (w, width))
        prev = jnp.where(first, src, pltpu.roll(x, 1, axis=0))
        return x + mu * (prev - x)

    def tile(j, g, ci, r):
        hc = slice(j * LANES, (j + 1) * LANES)
        lc = slice(0, B_LORA_W + B_LORA_A + B_LORA_G)
        cr = [None] * 4 if carries is None else [carries[0][j, g], carries[1][j, g], carries[2][j, g],
                                                  carries[3][g]]
        xr = shifted(zr[j][g], sh_r if has_state else None, hc, cr[0], mu_r[:, hc], g, ci, r)
        xk = shifted(zk[j][g], sh_k if has_state else None, hc, cr[1], mu_k[:, hc], g, ci, r)
        xv = shifted(zv[j][g], sh_v if has_state else None, hc, cr[2], mu_v[:, hc], g, ci, r)
        xl = shifted(zl[g], sh_l if has_state else None, lc, cr[3], mu_l[...], g, ci, r)
        wd = xl[:, 0:B_LORA_W]
        ad = xl[:, B_LORA_W:B_LORA_W + B_LORA_A]
        gd = xl[:, B_LORA_W + B_LORA_A:]
        wx = -(w0[:, hc] + _dot(jnp.tanh(wd), w2[:, hc]))
        w_raw = -(jnp.maximum(wx, 0.0) + jnp.log1p(jnp.exp(-jnp.abs(wx)))) - 0.5
        lw = -jnp.exp(w_raw)
        aa = _sigmoid(a0[:, hc] + _dot(ad, a2[:, hc]))
        gb = _dot(_sigmoid(gd), g2[:, hc])
        yield
        kk = xk * kkp[:, hc]
        kk = kk / jnp.maximum(jnp.sqrt(gsum(kk * kk)), 1e-12)
        k2 = xk * (1.0 + (aa - 1.0) * kap[:, hc])
        a = -kk
        b = kk * aa
        yield
        e = _sel_dot(cum[...], lw)
        yield
        lwc = e[0:w]
        lwl = e[w:n]
        dec_in = jnp.exp(lwc)
        dec_ex = jnp.exp(lwc - lw)
        inv = jnp.exp(-lwc)
        rest = jnp.exp(lwl - lwc)
        a_t = a * dec_ex
        r_t = xr * dec_in
        b_t = b * inv
        k_t = k2 * inv
        gram = _dot_nt(jnp.concatenate([stack(a_t), stack(r_t)], axis=0),
                       jnp.concatenate([stack(b_t), stack(k_t)], axis=0))
        yield
        m_ab = gram[0:n, 0:n] * msk[0]
        m_ak = gram[0:n, n:2 * n] * msk[0]
        m_rb = gram[n:2 * n, 0:n] * msk[1]
        m_rk = gram[n:2 * n, n:2 * n] * msk[1]
        p = m_ab
        tinv = eye + p
        for _ in range(nsq):
            p = _dot(p, p)
            yield
            tinv = tinv + _dot(tinv, p)
            yield
        p0a, p0r = [], []
        for s in range(nseq):
            rows = slice(s * c, (s + 1) * c)
            pr = _dot_nt(jnp.concatenate([a_t[rows], r_t[rows]], axis=0), s_ref[j, g * nseq + s])
            p0a.append(pr[0:c])
            p0r.append(pr[c:2 * c])
        p0a = p0a[0] if nseq == 1 else jnp.concatenate(p0a, axis=0)
        p0r = p0r[0] if nseq == 1 else jnp.concatenate(p0r, axis=0)
        yield
        vs = stack(xv)
        rhs = stack(p0a) + _dot(m_ak, vs)
        yield
        us = _dot(tinv, rhs)
        yield
        ys = stack(p0r) + _dot(m_rb, us) + _dot(m_rk, vs)
        yield
        y = fold(ys)
        u_w = fold(us)
        b_g = b * rest
        k_g = k2 * rest
        dec_l = jnp.exp(lwl)
        for s in range(nseq):
            rows = slice(s * c, (s + 1) * c)
            upd = _dot_tn(jnp.concatenate([u_w[rows], xv[rows]], axis=0),
                          jnp.concatenate([b_g[rows], k_g[rows]], axis=0))
            s_ref[j, g * nseq + s] = s_ref[j, g * nseq + s] * dec_l[s * c:s * c + 1, :] + gm_f * upd
        yield
        mean = gsum(y) * (1.0 / B_HEAD)
        yield
        d = y - mean
        var = gsum(d * d) * (1.0 / B_HEAD)
        yn = d * lax.rsqrt(var + RWKV_GN_EPS) * gnw[:, hc] + gnb[:, hc]
        yield
        bonus = gsum(xr * k2 * rkp[:, hc])
        o_ref[g, pl.ds(r, w), hc] = ((yn + bonus * xv) * gb).astype(BF16)

    def chunk(ci, carry):
        r = pl.multiple_of(ci * w, w)
        _round_robin([tile(j, g, ci, r) for j in range(hs) for g in range(g_n)])
        return carry

    lax.fori_loop(0, chunks, chunk, 0)

    base = blk - w
    for g in range(g_n):
        per_head = [(zr[j][g], shift_outs[0], j) for j in range(hs)]
        per_head += [(zk[j][g], shift_outs[1], j) for j in range(hs)]
        per_head += [(zv[j][g], shift_outs[2], j) for j in range(hs)]
        for ref, out, j in per_head + [(zl[g], shift_outs[3], 0)]:
            cols = slice(j * LANES, j * LANES + ref.shape[1])
            for s in range(nseq):
                last = base + (s + 1) * c - 1
                out[g * nseq + s:g * nseq + s + 1, cols] = ref[last:last + 1, :]
        if carries is not None:
            for j in range(hs):
                carries[0][j, g] = zr[j][g][blk - SUBLANES:blk, :]
                carries[1][j, g] = zk[j][g][blk - SUBLANES:blk, :]
                carries[2][j, g] = zv[j][g][blk - SUBLANES:blk, :]
            carries[3][g] = zl[g][blk - SUBLANES:blk, :]

    @pl.when(tb == pl.num_programs(2) - 1)
    def _():
        for j in range(hs):
            for s in range(g_n * nseq):
                ss = s_ref[j, s]
                s_out[0, s, 2 * j] = ss[0:B_HEAD, 0:B_HEAD]
                s_out[0, s, 2 * j + 1] = ss[B_HEAD:n, B_HEAD:n]


def _rwkv(z, p, state, shift, prev_out, layer, depth, b, t, row_off):
    pairs = B_HEADS // 2
    til = _Tiling(b, t, RWKV_ROWS, row_off, pairs, 2 * B_HEAD * B_HEAD * 4)
    cum, msk, gmat = _rwkv_consts(til.c)
    cb = B_OFF // LANES
    wb = B_WIDTH // LANES
    lora_w = B_LORA_W + B_LORA_A + B_LORA_G
    lb_z = (B_OFF + 3 * B_WIDTH) // lora_w
    lb_s = (3 * B_WIDTH) // lora_w

    def vec(x):
        return x.reshape(1, -1)

    def pspec(rows_, col_off=0):
        return til.hspec(rows_, LANES, col_off)

    in_specs, args = [], []
    for col in (cb, cb + wb, cb + 2 * wb):
        for j, g in til.streams():
            in_specs.append(til.zspec(j, g, LANES, col))
            args.append(z)
    for g in range(til.g):
        in_specs.append(til.zspec(0, g, lora_w, lb_z, per_head=0))
        args.append(z)
    in_specs += [
        pspec(1), pspec(1, wb), pspec(1, 2 * wb),
        pl.BlockSpec((1, lora_w), lambda i, h, tb: (0, lb_s)),
        pspec(1), pspec(1), pspec(1), pspec(1), pspec(1), pspec(1), pspec(1),
        pspec(B_LORA_W), pspec(B_LORA_A), pspec(B_LORA_G),
        _const_spec(cum.shape), _const_spec(msk.shape), _const_spec(gmat.shape),
    ]
    mu = vec(p['rwkv_mu'])
    args += [mu, mu, mu, mu,
             vec(p['rwkv_w0']), vec(p['rwkv_a0']), vec(p['rwkv_kk']), vec(p['rwkv_ka']),
             vec(p['rwkv_rk']), vec(p['rwkv_gn_w']), vec(p['rwkv_gn_b']),
             p['rwkv_w2'].astype(BF16), p['rwkv_a2'].astype(BF16), p['rwkv_g2'].astype(BF16),
             jnp.asarray(cum, BF16), jnp.asarray(msk, F32), jnp.asarray(gmat, BF16)]
    ns = til.g * til.nseq
    if state is not None:
        hw = til.hs * LANES
        wbh = wb // til.hs
        in_specs += [
            pl.BlockSpec((ns, hw), lambda i, h, tb: (i, h)),
            pl.BlockSpec((ns, hw), lambda i, h, tb: (i, wbh + h)),
            pl.BlockSpec((ns, hw), lambda i, h, tb: (i, 2 * wbh + h)),
            pl.BlockSpec((ns, lora_w), lambda i, h, tb: (i, lb_s)),
        ]
        args += [shift, shift, shift, shift]
    has_state, has_alias, aliases, s_spec, s_shape = _state_io(
        til, state, prev_out, layer, depth, b, B_HEADS, 2, B_HEAD, B_HEAD, in_specs, args)
    scratch = [pltpu.VMEM((til.hs, til.g * til.nseq, 2 * B_HEAD, 2 * B_HEAD), F32)]
    if not has_state:
        scratch += [pltpu.VMEM((til.hs, til.g, SUBLANES, LANES), F32) for _ in range(3)]
        scratch += [pltpu.VMEM((til.g, SUBLANES, lora_w), F32)]
    o, s, sh_r, sh_k, sh_v, sh_l = pl.pallas_call(
        functools.partial(_rwkv_kernel, g_n=til.g, hs=til.hs, chunks=til.chunks, nseq=til.nseq, c=til.c,
                          has_state=has_state, has_alias=has_alias),
        grid=(til.steps, til.head_steps, til.nt),
        in_specs=in_specs,
        out_specs=[til.ospec(LANES), s_spec,
                   pl.BlockSpec((ns, til.hs * LANES), lambda i, h, tb: (i, h)),
                   pl.BlockSpec((ns, til.hs * LANES), lambda i, h, tb: (i, h)),
                   pl.BlockSpec((ns, til.hs * LANES), lambda i, h, tb: (i, h)),
                   pl.BlockSpec((ns, lora_w), lambda i, h, tb: (i, 0))],
        out_shape=[til.oshape(B_WIDTH), s_shape] + [
            jax.ShapeDtypeStruct((b, wd), F32) for wd in (B_WIDTH, B_WIDTH, B_WIDTH, lora_w)],
        scratch_shapes=scratch,
        input_output_aliases=aliases,
        compiler_params=_cparams(("parallel", "parallel", "arbitrary")),
        name="rwkv7",
    )(*args)
    return o.reshape(b * t, B_WIDTH), s, jnp.concatenate([sh_r, sh_k, sh_v, sh_l], axis=1)


def _merge_kernel(oa, ob, oc, wa, wb, wc, ga, gb, gc, *rest):
    o_ref = rest[-1]
    acc = _sigmoid(ga[...]) * jnp.dot(oa[...], wa[...], preferred_element_type=F32)
    acc = acc + _sigmoid(gb[...]) * jnp.dot(ob[...], wb[...], preferred_element_type=F32)
    acc = acc + _sigmoid(gc[...]) * jnp.dot(oc[...], wc[...], preferred_element_type=F32)
    o_ref[...] = acc.astype(BF16)


def _merge(z, oa, ob, oc, wa, wb, wc, layer, row_off, prev_out):
    m = oa.shape[0]
    tm = _pick(m, (1024, 512, 256, 128))
    tn = 256
    gblk = G_OFF // tn
    nb = D_MODEL // tn
    assert G_OFF % tn == 0 and row_off % tm == 0
    off = row_off // tm

    def ospec():
        return pl.BlockSpec((tm, oa.shape[1]), lambda i, j: (i, 0))

    def wspec():
        return pl.BlockSpec((None, wa.shape[1], tn), lambda i, j: (layer, 0, j))

    def gspec(br):
        return pl.BlockSpec((tm, tn), lambda i, j: (i + off, gblk + br * nb + j))

    in_specs = [ospec(), ospec(), ospec(), wspec(), wspec(), wspec(), gspec(0), gspec(1), gspec(2)]
    args = [oa, ob, oc, wa, wb, wc, z, z, z]
    aliases = {}
    if prev_out is not None:
        aliases = {len(args): 0}
        in_specs.append(pl.BlockSpec(memory_space=pl.ANY))
        args.append(prev_out)
    return pl.pallas_call(
        _merge_kernel,
        grid=(m // tm, nb),
        in_specs=in_specs,
        out_specs=pl.BlockSpec((tm, tn), lambda i, j: (i + off, j)),
        out_shape=jax.ShapeDtypeStruct((z.shape[0], D_MODEL), BF16),
        input_output_aliases=aliases,
        compiler_params=_cparams(("parallel", "arbitrary")),
        name="merge",
    )(*args)


def _proj_res_kernel(m_ref, w_ref, g_ref, x_ref, o_ref):
    y = jnp.dot(m_ref[...], w_ref[...], preferred_element_type=F32)
    yn = y * lax.rsqrt(jnp.mean(y * y, axis=-1, keepdims=True) + NORM_EPS) * g_ref[...]
    o_ref[...] = x_ref[...] + yn


def _proj_res(mrg, w, g, x, layer):
    m = x.shape[0]
    tm = _pick(m, (512, 256, 128))
    return pl.pallas_call(
        _proj_res_kernel,
        grid=(m // tm,),
        in_specs=[
            pl.BlockSpec((tm, D_MODEL), lambda i: (i, 0)),
            pl.BlockSpec((None, D_MODEL, D_MODEL), lambda i: (layer, 0, 0)),
            pl.BlockSpec((1, D_MODEL), lambda i: (0, 0)),
            pl.BlockSpec((tm, D_MODEL), lambda i: (i, 0)),
        ],
        out_specs=pl.BlockSpec((tm, D_MODEL), lambda i: (i, 0)),
        out_shape=jax.ShapeDtypeStruct((m, D_MODEL), F32),
        compiler_params=_cparams(("parallel",)),
        name="proj_res",
    )(mrg, w, g.reshape(1, D_MODEL), x)


def _gelu(x):
    return 0.5 * x * (1.0 + jnp.tanh(math.sqrt(2.0 / math.pi) * (x + 0.044715 * (x * x * x))))


def _up_act_kernel(*refs, tm, t, blocks_per_seq, has_state, has_alias):
    it = iter(refs)
    x_ref, g_ref, wa, wb, cw, cb = (next(it) for _ in range(6))
    st = next(it) if has_state else None
    if has_alias:
        next(it)
    o_ref, nc_ref, xn_ref = next(it), next(it), next(it)
    tail = None if has_state else next(it)
    i = pl.program_id(0)
    j = pl.program_id(1)
    tn = wa.shape[1]
    sub = UP_SUB

    @pl.when(j == 0)
    def _():
        rows = min(tm, LANES)

        def body(r, carry):
            r0 = pl.multiple_of(r * rows, rows)
            x = x_ref[pl.ds(r0, rows), :]
            ms = jnp.mean(x * x, axis=-1, keepdims=True)
            xn_ref[pl.ds(r0, rows), :] = (x * lax.rsqrt(ms + NORM_EPS) * g_ref[...]).astype(BF16)
            return carry
        lax.fori_loop(0, tm // rows, body, 0)

    if has_state:
        ns = tm // t
        tt = lax.broadcasted_iota(jnp.int32, (ns, t, sub), 1)
    else:
        seq_start = (i % blocks_per_seq) == 0
        rr = lax.broadcasted_iota(jnp.int32, (tm, sub), 0)
    for c in range(tn // sub):
        cols = slice(c * sub, (c + 1) * sub)
        ua = jnp.dot(xn_ref[...], wa[:, cols], preferred_element_type=F32)
        ub = jnp.dot(xn_ref[...], wb[:, cols], preferred_element_type=F32)
        if has_state:
            x3 = ua.reshape(ns, t, sub)
            s_old = st[:, 0:1, cols]
            s_new = st[:, 1:2, cols]
            prev1 = jnp.where(tt >= 1, pltpu.roll(x3, 1, axis=1), s_new)
            prev2 = jnp.where(tt >= 2, pltpu.roll(x3, 2, axis=1), jnp.where(tt == 1, s_new, s_old))
            prev1 = prev1.reshape(tm, sub)
            prev2 = prev2.reshape(tm, sub)
            nc_ref[:, :, cols] = x3[:, t - (CONV_W - 1):, :]
        else:
            h = jnp.where(seq_start, 0.0, tail[j, :, cols])
            h1 = h[SUBLANES - 1:SUBLANES, :]
            h2 = h[SUBLANES - 2:SUBLANES - 1, :]
            prev1 = jnp.where(rr == 0, h1, pltpu.roll(ua, 1, axis=0))
            prev2 = jnp.where(rr == 0, h2, jnp.where(rr == 1, h1, pltpu.roll(ua, 2, axis=0)))
            tail[j, :, cols] = ua[tm - SUBLANES:, :]
            nc_ref[0, :, cols] = ua[tm - (CONV_W - 1):, :]
        conv = cb[:, cols] + cw[0:1, cols] * prev2 + cw[1:2, cols] * prev1 + cw[2:3, cols] * ua
        o_ref[:, cols] = (_gelu(conv) * ub).astype(BF16)


def _up_act(x1, g, w_up, cw, cb, state, prev_out, layer, b, t, row_off):
    m = b * t
    total = x1.shape[0]
    has_state = state is not None
    has_alias = prev_out is not None
    tn = UP_TN
    nj = D_FF // tn
    if has_state:
        tm = _pick(m, (1024, 512, 256, 128))
        assert tm % t == 0 and t == SUBLANES
        blocks_per_seq = 1
        nc_spec = pl.BlockSpec((tm // t, CONV_W - 1, tn), lambda i, j: (i, 0, j))
    else:
        tm = _pick(t, (1024, 512, 256, 128))
        blocks_per_seq = t // tm
        nc_spec = pl.BlockSpec((1, CONV_W - 1, tn), lambda i, j: (i, 0, j))
    nc_rows = b * blocks_per_seq
    off = row_off // tm
    assert row_off % tm == 0
    in_specs = [
        pl.BlockSpec((tm, D_MODEL), lambda i, j: (i + off, 0)),
        pl.BlockSpec((1, D_MODEL), lambda i, j: (0, 0)),
        pl.BlockSpec((None, D_MODEL, tn), lambda i, j: (layer, 0, j)),
        pl.BlockSpec((None, D_MODEL, tn), lambda i, j: (layer, 0, nj + j)),
        pl.BlockSpec((CONV_W, tn), lambda i, j: (0, j)),
        pl.BlockSpec((1, tn), lambda i, j: (0, j)),
    ]
    args = [x1, g.reshape(1, D_MODEL), w_up, w_up, cw, cb.reshape(1, D_FF)]
    if has_state:
        in_specs.append(pl.BlockSpec((tm // t, CONV_W - 1, tn), lambda i, j: (i, 0, j)))
        args.append(state)
    aliases = {}
    if has_alias:
        aliases = {len(args): 0}
        in_specs.append(pl.BlockSpec(memory_space=pl.ANY))
        args.append(prev_out)
    scratch = [pltpu.VMEM((tm, D_MODEL), BF16)]
    if not has_state:
        scratch.append(pltpu.VMEM((nj, SUBLANES, tn), F32))
    act, nc = pl.pallas_call(
        functools.partial(_up_act_kernel, tm=tm, t=t, blocks_per_seq=blocks_per_seq,
                          has_state=has_state, has_alias=has_alias),
        grid=(m // tm, nj),
        in_specs=in_specs,
        out_specs=[pl.BlockSpec((tm, tn), lambda i, j: (i + off, j)), nc_spec],
        out_shape=[jax.ShapeDtypeStruct((total, D_FF), BF16),
                   jax.ShapeDtypeStruct((nc_rows, CONV_W - 1, D_FF), F32)],
        scratch_shapes=scratch,
        input_output_aliases=aliases,
        compiler_params=_cparams(("arbitrary", "arbitrary")),
        name="up_act",
    )(*args)
    return act, nc[blocks_per_seq - 1::blocks_per_seq]


def _down_res_kernel(a_ref, w_ref, g_ref, x_ref, o_ref, acc):
    kstep = pl.program_id(1)

    @pl.when(kstep == 0)
    def _():
        acc[...] = jnp.zeros_like(acc)

    acc[...] += jnp.dot(a_ref[...], w_ref[...], preferred_element_type=F32)

    @pl.when(kstep == pl.num_programs(1) - 1)
    def _():
        y = acc[...]
        yn = y * lax.rsqrt(jnp.mean(y * y, axis=-1, keepdims=True) + NORM_EPS) * g_ref[...]
        o_ref[...] = x_ref[...] + yn


def _down_res(act, wd, g, x1, layer, row_off, rows):
    tm = _pick(rows, (512, 256, 128))
    tk = DOWN_TK
    off = row_off // tm
    assert row_off % tm == 0
    return pl.pallas_call(
        _down_res_kernel,
        grid=(rows // tm, D_FF // tk),
        in_specs=[
            pl.BlockSpec((tm, tk), lambda i, k: (i + off, k)),
            pl.BlockSpec((None, tk, D_MODEL), lambda i, k: (layer, k, 0)),
            pl.BlockSpec((1, D_MODEL), lambda i, k: (0, 0)),
            pl.BlockSpec((tm, D_MODEL), lambda i, k: (i + off, 0)),
        ],
        out_specs=pl.BlockSpec((tm, D_MODEL), lambda i, k: (i, 0)),
        out_shape=jax.ShapeDtypeStruct((rows, D_MODEL), F32),
        scratch_shapes=[pltpu.VMEM((tm, D_MODEL), F32)],
        compiler_params=_cparams(("parallel", "arbitrary")),
        name="down_res",
    )(act, wd, g.reshape(1, D_MODEL), x1)


def _layer(x, groups, lb, p, layer, depth, prev, split_out):
    z = _rms_matmul(x, p['pre_mix_g'], p['w_in'], layer, tn=1280)
    states = []
    mrg = None
    row = 0
    for gi, (b, t, t0, st) in enumerate(groups):
        s_a, s_b, s_sh, s_c, _ = st if st is not None else (None,) * 5
        pv = prev[gi] if prev is not None else (None,) * 3
        o_a, n_a = _hgrn(z, lb, p['a_norm_g'], s_a, pv[0], layer, depth, b, t, row)
        o_b, n_b, n_sh = _rwkv(z, p, s_b, s_sh, pv[1], layer, depth, b, t, row)
        o_c, n_c = _retention(z, s_c, pv[2], layer, depth, b, t, t0, row)
        mrg = _merge(z, o_a, o_b, o_c, p['w_br_a'], p['w_br_b'], p['w_br_c'], layer, row, mrg)
        states.append([n_a, n_b, n_sh, n_c])
        row += b * t
    x1 = _proj_res(mrg, p['w_out'], p['post_mix_g'], x, layer)
    act = None
    row = 0
    for gi, (b, t, t0, st) in enumerate(groups):
        s_cv = st[4] if st is not None else None
        act, n_cv = _up_act(x1, p['pre_ffn_g'], p['w_up'], p['conv_w'], p['conv_b'], s_cv, act,
                            layer, b, t, row)
        states[gi].append(n_cv)
        row += b * t
    if split_out:
        x2, row = [], 0
        for (b, t, _, _) in groups:
            x2.append(_down_res(act, p['w_down'], p['post_ffn_g'], x1, layer, row, b * t))
            row += b * t
    else:
        x2 = _down_res(act, p['w_down'], p['post_ffn_g'], x1, layer, 0, x1.shape[0])
    return x2, states


def kernel(x_prompt, x_sample, state_hgrn, state_rwkv, state_rwkv_shift, state_ret, state_conv,
           lb_logits, pre_mix_g, w_in, a_norm_g, rwkv_mu, rwkv_w0, rwkv_w2, rwkv_a0, rwkv_a2,
           rwkv_g2, rwkv_kk, rwkv_ka, rwkv_rk, rwkv_gn_w, rwkv_gn_b, w_br_a, w_br_b, w_br_c,
           w_out, post_mix_g, pre_ffn_g, w_up, conv_w, conv_b, w_down, post_ffn_g):
    depth = w_in.shape[0]
    bp, tp, _ = x_prompt.shape
    bs, ts, _ = x_sample.shape
    past_len = 16384
    lb_soft = jax.nn.softmax(lb_logits.astype(F32), axis=0)
    lbs = jnp.cumsum(lb_soft, axis=0) - lb_soft[0]
    big = {'w_in': w_in.astype(BF16), 'w_br_a': w_br_a.astype(BF16), 'w_br_b': w_br_b.astype(BF16),
           'w_br_c': w_br_c.astype(BF16), 'w_out': w_out.astype(BF16), 'w_up': w_up.astype(BF16),
           'w_down': w_down.astype(BF16)}
    x = jnp.concatenate([x_prompt.reshape(bp * tp, D_MODEL), x_sample.reshape(bs * ts, D_MODEL)], axis=0)
    small = [[[], []], [[], []]]
    prev = None
    for l in range(depth):
        p = dict(big)
        p.update({
            'pre_mix_g': pre_mix_g[l], 'a_norm_g': a_norm_g[l],
            'rwkv_mu': rwkv_mu[l], 'rwkv_w0': rwkv_w0[l], 'rwkv_w2': rwkv_w2[l],
            'rwkv_a0': rwkv_a0[l], 'rwkv_a2': rwkv_a2[l], 'rwkv_g2': rwkv_g2[l],
            'rwkv_kk': rwkv_kk[l], 'rwkv_ka': rwkv_ka[l], 'rwkv_rk': rwkv_rk[l],
            'rwkv_gn_w': rwkv_gn_w[l], 'rwkv_gn_b': rwkv_gn_b[l],
            'post_mix_g': post_mix_g[l], 'pre_ffn_g': pre_ffn_g[l],
            'conv_w': conv_w[l], 'conv_b': conv_b[l], 'post_ffn_g': post_ffn_g[l],
        })
        groups = [
            (bp, tp, 0, None),
            (bs, ts, past_len, (state_hgrn, state_rwkv, state_rwkv_shift[l], state_ret, state_conv[l])),
        ]
        x, states = _layer(x, groups, lbs[l], p, l, depth, prev, l == depth - 1)
        prev = [(st[0], st[1], st[3]) for st in states]
        for gi, st in enumerate(states):
            small[gi][0].append(st[2])
            small[gi][1].append(st[4])
    y_p = x[0].reshape(bp, tp, D_MODEL)
    y_s = x[1].reshape(bs, ts, D_MODEL)
    outs = []
    for gi in range(2):
        outs += [prev[gi][0], prev[gi][1], jnp.stack(small[gi][0]), prev[gi][2], jnp.stack(small[gi][1])]
    return (y_p, y_s, *outs)
```

```python
import functools
import itertools
import math

import jax
import jax.numpy as jnp
import numpy as np
from jax import lax
from jax.experimental import pallas as pl
from jax.experimental.pallas import tpu as pltpu

F32 = jnp.float32
BF16 = jnp.bfloat16

D_MODEL = 2048
A_HEADS, A_DK, A_DV = 8, 128, 128
A_QK = A_HEADS * A_DK
A_WIDTH = A_HEADS * A_DV
F_TINY = 1e-30
B_HEAD = 64
B_WIDTH = 1024
B_HEADS = B_WIDTH // B_HEAD
B_LORA_W, B_LORA_A, B_LORA_G = 64, 64, 128
RWKV_GN_EPS = 64e-5
C_HEADS, C_DK, C_DV = 4, 128, 256
C_QK = C_HEADS * C_DK
C_WIDTH = C_HEADS * C_DV
ROPE_BASE = 10000.0
A_COLS = 2 * A_QK + 2 * A_WIDTH
B_COLS = 3 * B_WIDTH + B_LORA_W + B_LORA_A + B_LORA_G
C_COLS = 2 * C_QK + 2 * C_WIDTH
N_BRANCH = 3
P_COLS = A_COLS + B_COLS + C_COLS + N_BRANCH * D_MODEL
B_OFF = A_COLS
C_OFF = A_COLS + B_COLS
G_OFF = A_COLS + B_COLS + C_COLS
D_FF = 5632
CONV_W = 3
NORM_EPS = 1e-6

LANES = 128
SUBLANES = 8
MIX_ROWS = 128
RWKV_ROWS = 64
MIX_TIME_BLOCK = 512
MIX_STREAMS = 4
HGRN_STREAMS = 8
RET_STREAMS = 8
RWKV_STREAMS = 16
STATE_WINDOW_BYTES = 4 * 1024 * 1024
UP_TN = 512
UP_SUB = 256
DOWN_TK = 2816
VMEM_LIMIT = 56 * 1024 * 1024


def _cparams(sem):
    return pltpu.CompilerParams(dimension_semantics=sem, vmem_limit_bytes=VMEM_LIMIT)


def _dot(a, b):
    return jnp.dot(a.astype(BF16), b.astype(BF16), preferred_element_type=F32)


def _dot_nt(a, b):
    return lax.dot_general(a.astype(BF16), b.astype(BF16), (((1,), (1,)), ((), ())),
                           preferred_element_type=F32)


def _dot_tn(a, b):
    return lax.dot_general(a.astype(BF16), b.astype(BF16), (((0,), (0,)), ((), ())),
                           preferred_element_type=F32)


def _split(x):
    hi = x.astype(BF16)
    lo = (x - hi.astype(F32)).astype(BF16)
    return hi, lo


def _sel_dot(m, x):
    hi, lo = _split(x)
    n = x.shape[1]
    both = jnp.dot(m, jnp.concatenate([hi, lo], axis=1), preferred_element_type=F32)
    return both[:, :n] + both[:, n:]


def _dot_sel(x, m):
    hi, lo = _split(x)
    return (jnp.dot(hi, m, preferred_element_type=F32)
            + jnp.dot(lo, m, preferred_element_type=F32))


def _sigmoid(x):
    return jax.nn.sigmoid(x)


def _silu(x):
    return x * jax.nn.sigmoid(x)


def _round_robin(gens):
    for _ in itertools.zip_longest(*gens):
        pass


def _pick(n, cands):
    for c in cands:
        if n % c == 0:
            return c
    raise ValueError(f"no tile in {cands} divides {n}")


def _rms_matmul_kernel(x_ref, g_ref, w_ref, o_ref, xn_ref, *, tm, sub):
    @pl.when(pl.program_id(1) == 0)
    def _():
        def body(i, carry):
            r = pl.multiple_of(i * sub, sub)
            x = x_ref[pl.ds(r, sub), :]
            ms = jnp.mean(x * x, axis=-1, keepdims=True)
            xn_ref[pl.ds(r, sub), :] = (x * lax.rsqrt(ms + NORM_EPS) * g_ref[...]).astype(BF16)
            return carry
        lax.fori_loop(0, tm // sub, body, 0)

    o_ref[...] = jnp.dot(xn_ref[...], w_ref[...], preferred_element_type=F32)


def _rms_matmul(x, g, w, layer, tn):
    m, k = x.shape
    n = w.shape[2]
    tm = _pick(m, (1024, 512, 256, 128))
    sub = min(tm, 128)
    return pl.pallas_call(
        functools.partial(_rms_matmul_kernel, tm=tm, sub=sub),
        grid=(m // tm, n // tn),
        in_specs=[
            pl.BlockSpec((tm, k), lambda i, j: (i, 0)),
            pl.BlockSpec((1, k), lambda i, j: (0, 0)),
            pl.BlockSpec((None, k, tn), lambda i, j: (layer, 0, j)),
        ],
        out_specs=pl.BlockSpec((tm, tn), lambda i, j: (i, j)),
        out_shape=jax.ShapeDtypeStruct((m, n), F32),
        scratch_shapes=[pltpu.VMEM((tm, k), BF16)],
        compiler_params=_cparams(("parallel", "arbitrary")),
        name="rms_matmul",
    )(x, g.reshape(1, k), w)


class _Tiling:
    def __init__(self, b, t, rows, row_off, heads, state_bytes, streams_total):
        if t >= rows:
            self.nseq, self.c = 1, rows
            self.blk = _pick(t, (MIX_TIME_BLOCK, rows))
            self.nt = t // self.blk
            n_streams = b
        else:
            assert rows % t == 0 and b % (rows // t) == 0
            self.nseq, self.c = rows // t, t
            self.blk, self.nt = rows, 1
            n_streams = b // self.nseq
        self.g = _pick(n_streams, (MIX_STREAMS, 2, 1))
        self.hs = _pick(heads, (max(streams_total // self.g, 1), 2, 1))
        while self.g * self.nseq * self.hs * state_bytes > STATE_WINDOW_BYTES and self.hs > 1:
            self.hs //= 2
        while self.g * self.nseq * self.hs * state_bytes > STATE_WINDOW_BYTES and self.g > 1:
            self.g //= 2
        self.head_steps = heads // self.hs
        self.steps = n_streams // self.g
        self.n_streams = n_streams
        self.chunks = self.blk // rows
        assert row_off % self.blk == 0
        self.off = row_off // self.blk

    def streams(self):
        return [(j, g) for j in range(self.hs) for g in range(self.g)]

    def zspec(self, j, stream, width, col_blk, per_head=1):
        g, nt, off, hs = self.g, self.nt, self.off, self.hs
        return pl.BlockSpec(
            (self.blk, width),
            lambda i, h, tb: (off + (i * g + stream) * nt + tb, col_blk + per_head * (h * hs + j)))

    def hspec(self, rows, width, blk_off=0):
        assert blk_off % self.hs == 0
        off = blk_off // self.hs
        return pl.BlockSpec((rows, self.hs * width), lambda i, h, tb: (0, off + h))

    def ospec(self, width):
        return pl.BlockSpec((self.g, self.blk, self.hs * width), lambda i, h, tb: (i, tb, h))

    def oshape(self, width):
        return jax.ShapeDtypeStruct((self.n_streams, self.nt * self.blk, width), BF16)

    def sspec(self, layer, heads_per_step, d0, d1):
        return pl.BlockSpec((1, self.g * self.nseq, self.hs * heads_per_step, d0, d1),
                            lambda i, h, tb: (layer, i, h, 0, 0))


def _const_spec(shape):
    nd = len(shape)
    return pl.BlockSpec(shape, lambda i, h, tb: (0,) * nd)


def _state_io(til, state, prev_out, layer, depth, b, heads, heads_per_step, d0, d1,
              in_specs, args):
    has_state = state is not None
    if has_state:
        in_specs.append(til.sspec(layer, heads_per_step, d0, d1))
        args.append(state)
    aliases = {}
    if prev_out is not None:
        aliases = {len(args): 1}
        in_specs.append(pl.BlockSpec(memory_space=pl.ANY))
        args.append(prev_out)
    out_spec = til.sspec(layer, heads_per_step, d0, d1)
    out_shape = jax.ShapeDtypeStruct((depth, b, heads, d0, d1), F32)
    return has_state, prev_out is not None, aliases, out_spec, out_shape


@functools.lru_cache(maxsize=None)
def _hgrn_consts(c):
    n = MIX_ROWS
    nlev = int(math.log2(c))
    t = np.arange(n)
    u = np.arange(n)[None, :]
    blk = t // c
    same = blk[:, None] == blk[None, :]
    mats = [same & (u <= t[:, None]), same]
    masks = [np.eye(n, dtype=bool)]
    for lev in range(nlev):
        h = 1 << lev
        base = (t // (2 * h)) * (2 * h)
        mid = base + h
        upper = t >= mid
        e_up = (u >= mid[:, None]) & (u <= t[:, None])
        e_lo = (u >= t[:, None] + 1) & (u <= mid[:, None] - 1)
        mats.append(np.where(upper[:, None], e_up, e_lo))
        masks.append((base[:, None] == base[None, :]) & upper[:, None] & (~upper)[None, :])
    sel = np.concatenate(mats, 0).astype(np.float32)
    msk = np.stack(masks).astype(np.float32)
    return sel, msk, nlev


def _hgrn_kernel(*refs, g_n, hs, chunks, nseq, c, nlev, has_state, has_alias):
    it = iter(refs)
    zq, zf, zi, zg = ([[next(it) for _ in range(g_n)] for _ in range(hs)] for _ in range(4))
    lb, gn, sel, msk = (next(it) for _ in range(4))
    s0 = next(it) if has_state else None
    if has_alias:
        next(it)
    o_ref, s_out, s_ref = next(it), next(it), next(it)
    n = MIX_ROWS
    tb = pl.program_id(2)

    @pl.when(tb == 0)
    def _():
        if has_state:
            for j in range(hs):
                s_ref[j] = s0[0, :, j]
        else:
            s_ref[...] = jnp.zeros_like(s_ref)

    def tile(j, g, r):
        hcols = slice(j * LANES, (j + 1) * LANES)
        lbv = lb[:, hcols]
        xq = zq[j][g][pl.ds(r, n), :]
        fa = zf[j][g][pl.ds(r, n), :]
        v = zi[j][g][pl.ds(r, n), :]
        xg = zg[j][g][pl.ds(r, n), :]
        q = _silu(xq)
        f_gate = lbv + (1.0 - lbv) * _sigmoid(fa)
        gl = jnp.log(jnp.maximum(f_gate, F_TINY))
        k = (1.0 - lbv) * _sigmoid(-fa)
        e = _sel_dot(sel[...], gl)
        yield
        b = e[0:n]
        bl = e[n:2 * n]
        scores = msk[0] * _dot_nt(q, k)
        for lev in range(nlev):
            x = jnp.exp(e[(lev + 2) * n:(lev + 3) * n])
            scores = scores + msk[lev + 1] * _dot_nt(q * x, k * x)
        yield
        o = _dot(scores, v)
        yield
        qe = q * jnp.exp(b)
        kt = k * jnp.exp(bl - b)
        dt = jnp.exp(bl).T
        outs = []
        for s in range(nseq):
            rows = slice(s * c, (s + 1) * c)
            ss = s_ref[j, g * nseq + s]
            outs.append(_dot(qe[rows], ss))
            dcol = jnp.broadcast_to(dt[:, s * c:s * c + 1], (A_DK, A_DV))
            s_ref[j, g * nseq + s] = ss * dcol + _dot_tn(kt[rows], v[rows])
        o = o + (outs[0] if nseq == 1 else jnp.concatenate(outs, axis=0))
        on = o * lax.rsqrt(jnp.mean(o * o, axis=-1, keepdims=True) + NORM_EPS) * gn[:, hcols]
        o_ref[g, pl.ds(r, n), hcols] = (on * _silu(xg)).astype(BF16)

    def chunk(ci, carry):
        r = pl.multiple_of(ci * n, n)
        _round_robin([tile(j, g, r) for j in range(hs) for g in range(g_n)])
        return carry

    lax.fori_loop(0, chunks, chunk, 0)

    @pl.when(tb == pl.num_programs(2) - 1)
    def _():
        for j in range(hs):
            s_out[0, :, j] = s_ref[j]


def _hgrn(z, lb, gn, state, prev_out, layer, depth, b, t, row_off):
    til = _Tiling(b, t, MIX_ROWS, row_off, A_HEADS, A_DK * A_DV * 4, HGRN_STREAMS)
    sel, msk, nlev = _hgrn_consts(til.c)
    qk_blocks = A_QK // LANES
    in_specs, args = [], []
    for col in range(4):
        for j, g in til.streams():
            in_specs.append(til.zspec(j, g, LANES, col * qk_blocks))
            args.append(z)
    in_specs += [
        til.hspec(1, LANES), til.hspec(1, LANES),
        _const_spec(sel.shape), _const_spec(msk.shape),
    ]
    args += [lb.reshape(1, A_QK), gn.reshape(1, A_WIDTH), jnp.asarray(sel, BF16), jnp.asarray(msk, F32)]
    has_state, has_alias, aliases, s_spec, s_shape = _state_io(
        til, state, prev_out, layer, depth, b, A_HEADS, 1, A_DK, A_DV, in_specs, args)
    o, s = pl.pallas_call(
        functools.partial(_hgrn_kernel, g_n=til.g, hs=til.hs, chunks=til.chunks, nseq=til.nseq, c=til.c,
                          nlev=nlev, has_state=has_state, has_alias=has_alias),
        grid=(til.steps, til.head_steps, til.nt),
        in_specs=in_specs,
        out_specs=[til.ospec(LANES), s_spec],
        out_shape=[til.oshape(A_WIDTH), s_shape],
        scratch_shapes=[pltpu.VMEM((til.hs, til.g * til.nseq, A_DK, A_DV), F32)],
        input_output_aliases=aliases,
        compiler_params=_cparams(("parallel", "parallel", "arbitrary")),
        name="hgrn2",
    )(*args)
    return o.reshape(b * t, A_WIDTH), s


def _ret_tables(c):
    n = MIX_ROWS
    log_g = jnp.log1p(-jnp.exp2(-5.0 - jnp.arange(C_HEADS, dtype=F32)))
    t = np.arange(n)
    tt = (t % c).astype(np.float32)
    blk = t // c
    rel = tt[:, None] - tt[None, :]
    same = (blk[:, None] == blk[None, :]) & (rel >= 0)
    dmat = jnp.where(same[None], jnp.exp(log_g[:, None, None] * np.maximum(rel, 0.0)[None]), 0.0)
    inner = jnp.exp(log_g[:, None] * (tt[None, :] + 1.0))
    tail = jnp.exp(log_g[:, None] * (c - 1.0 - tt[None, :]))
    total = jnp.exp(log_g * c)
    shape = (C_HEADS, n, n)
    tab = jnp.stack([dmat, jnp.broadcast_to(inner[:, :, None], shape),
                     jnp.broadcast_to(tail[:, :, None], shape)], axis=1)
    tot = jnp.broadcast_to(total[:, None, None], (C_HEADS, 1, C_DV))
    return tab.astype(F32), tot.astype(F32)


def _rope_tables(t0, t, reps):
    half = C_DK // 2
    inv = ROPE_BASE ** (-jnp.arange(half, dtype=F32) / half)
    pos = t0 + jnp.arange(t, dtype=F32)
    ang = pos[:, None] * inv[None, :]
    cos, sin = jnp.cos(ang), jnp.sin(ang)
    cosf = jnp.concatenate([cos, cos], axis=-1)
    sinf = jnp.concatenate([-sin, sin], axis=-1)
    return jnp.tile(cosf, (reps, 1)), jnp.tile(sinf, (reps, 1))


def _ret_kernel(*refs, g_n, hs, chunks, nseq, c, has_state, has_alias):
    it = iter(refs)
    zq, zk, zv, zg = ([[next(it) for _ in range(g_n)] for _ in range(hs)] for _ in range(4))
    cos, sin, tab, tot = (next(it) for _ in range(4))
    s0 = next(it) if has_state else None
    if has_alias:
        next(it)
    o_ref, s_out, s_ref = next(it), next(it), next(it)
    n = MIX_ROWS
    half = C_DK // 2
    tb = pl.program_id(2)

    @pl.when(tb == 0)
    def _():
        if has_state:
            for j in range(hs):
                s_ref[j] = s0[0, :, j]
        else:
            s_ref[...] = jnp.zeros_like(s_ref)

    def tile(j, g, r, cs, sn):
        xq = zq[j][g][pl.ds(r, n), :]
        xk = zk[j][g][pl.ds(r, n), :]
        v = zv[j][g][pl.ds(r, n), :]
        xg = zg[j][g][pl.ds(r, n), :]
        q = xq * cs + pltpu.roll(xq, half, axis=1) * sn
        k = (xk * cs + pltpu.roll(xk, half, axis=1) * sn) * (C_DK ** -0.5)
        scores = _dot_nt(q, k) * tab[j, 0]
        yield
        o = _dot(scores, v)
        yield
        qi = q * tab[j, 1]
        ktl = k * tab[j, 2]
        outs = []
        for s in range(nseq):
            rows = slice(s * c, (s + 1) * c)
            ss = s_ref[j, g * nseq + s]
            outs.append(_dot(qi[rows], ss))
            s_ref[j, g * nseq + s] = tot[j] * ss + _dot_tn(ktl[rows], v[rows])
        o = o + (outs[0] if nseq == 1 else jnp.concatenate(outs, axis=0))
        on = o * lax.rsqrt(jnp.mean(o * o, axis=-1, keepdims=True) + NORM_EPS)
        o_ref[g, pl.ds(r, n), j * C_DV:(j + 1) * C_DV] = (on * _silu(xg)).astype(BF16)

    def chunk(ci, carry):
        r = pl.multiple_of(ci * n, n)
        cs = cos[pl.ds(r, n), :]
        sn = sin[pl.ds(r, n), :]
        _round_robin([tile(j, g, r, cs, sn) for j in range(hs) for g in range(g_n)])
        return carry

    lax.fori_loop(0, chunks, chunk, 0)

    @pl.when(tb == pl.num_programs(2) - 1)
    def _():
        for j in range(hs):
            s_out[0, :, j] = s_ref[j]


def _retention(z, state, prev_out, layer, depth, b, t, t0, row_off):
    til = _Tiling(b, t, MIX_ROWS, row_off, C_HEADS, C_DK * C_DV * 4, RET_STREAMS)
    tab, tot = _ret_tables(til.c)
    cosf, sinf = _rope_tables(t0, t, max(MIX_ROWS // t, 1))
    qb = C_OFF // C_DK
    vb = (C_OFF + 2 * C_QK) // C_DV
    gb = (C_OFF + 2 * C_QK + C_WIDTH) // C_DV
    in_specs, args = [], []
    for width, col in ((C_DK, qb), (C_DK, qb + C_HEADS), (C_DV, vb), (C_DV, gb)):
        for j, g in til.streams():
            in_specs.append(til.zspec(j, g, width, col))
            args.append(z)
    in_specs += [
        pl.BlockSpec((til.blk, C_DK), lambda i, h, tb: (tb, 0)),
        pl.BlockSpec((til.blk, C_DK), lambda i, h, tb: (tb, 0)),
        pl.BlockSpec((til.hs, 3, MIX_ROWS, MIX_ROWS), lambda i, h, tb: (h, 0, 0, 0)),
        pl.BlockSpec((til.hs, 1, C_DV), lambda i, h, tb: (h, 0, 0)),
    ]
    args += [cosf, sinf, tab, tot]
    has_state, has_alias, aliases, s_spec, s_shape = _state_io(
        til, state, prev_out, layer, depth, b, C_HEADS, 1, C_DK, C_DV, in_specs, args)
    o, s = pl.pallas_call(
        functools.partial(_ret_kernel, g_n=til.g, hs=til.hs, chunks=til.chunks, nseq=til.nseq, c=til.c,
                          has_state=has_state, has_alias=has_alias),
        grid=(til.steps, til.head_steps, til.nt),
        in_specs=in_specs,
        out_specs=[til.ospec(C_DV), s_spec],
        out_shape=[til.oshape(C_WIDTH), s_shape],
        scratch_shapes=[pltpu.VMEM((til.hs, til.g * til.nseq, C_DK, C_DV), F32)],
        input_output_aliases=aliases,
        compiler_params=_cparams(("parallel", "parallel", "arbitrary")),
        name="retention",
    )(*args)
    return o.reshape(b * t, C_WIDTH), s


@functools.lru_cache(maxsize=None)
def _rwkv_consts(c):
    w = RWKV_ROWS
    t = np.arange(w)
    blk = t // c
    same = blk[:, None] == blk[None, :]
    tri = same & (t[None, :] <= t[:, None])
    cum = np.concatenate([tri, same], 0).astype(np.float32)
    strict = same & (t[None, :] < t[:, None])
    masks = np.stack([np.tile(strict, (1, 2)), np.tile(tri, (1, 2)),
                      np.tile(np.eye(w, dtype=bool), (1, 2))]).astype(np.float32)
    rr = np.arange(2 * w)
    hh = rr // B_HEAD
    gmat = (hh[:, None] == hh[None, :]).astype(np.float32)
    return cum, masks, gmat


def _rwkv_kernel(*refs, g_n, hs, chunks, nseq, c, has_state, has_alias):
    it = iter(refs)
    zr, zk, zv = ([[next(it) for _ in range(g_n)] for _ in range(hs)] for _ in range(3))
    zl = [next(it) for _ in range(g_n)]
    (mu_r, mu_k, mu_v, mu_l, w0, a0, kkp, kap, rkp, gnw, gnb,
     w2, a2, g2, cum, msk, gmat) = (next(it) for _ in range(17))
    if has_state:
        sh_r, sh_k, sh_v, sh_l, s0 = (next(it) for _ in range(5))
    if has_alias:
        next(it)
    o_ref, s_out = next(it), next(it)
    shift_outs = [next(it) for _ in range(4)]
    s_ref = next(it)
    carries = None if has_state else [next(it) for _ in range(4)]
    w = RWKV_ROWS
    n = 2 * w
    blk = chunks * w
    nsq = int(math.log2(c)) - 1
    tb = pl.program_id(2)
    lane = lax.broadcasted_iota(jnp.int32, (w, LANES), 1)
    head0 = lane < B_HEAD
    row = lax.broadcasted_iota(jnp.int32, (w, 1), 0)
    first = (row % c) == 0
    gm = gmat[...]
    gm_f = gm.astype(F32)

    @pl.when(tb == 0)
    def _():
        if has_state:
            zero_blk = jnp.zeros((B_HEAD, B_HEAD), F32)
            for j in range(hs):
                for s in range(g_n * nseq):
                    top = jnp.concatenate([s0[0, s, 2 * j], zero_blk], axis=1)
                    bot = jnp.concatenate([zero_blk, s0[0, s, 2 * j + 1]], axis=1)
                    s_ref[j, s] = jnp.concatenate([top, bot], axis=0)
        else:
            s_ref[...] = jnp.zeros_like(s_ref)
            for cr in carries:
                cr[...] = jnp.zeros_like(cr)

    def stack(x):
        return jnp.concatenate([jnp.where(head0, x, 0.0), jnp.where(head0, 0.0, x)], axis=0)

    def gsum(x):
        return _dot_sel(x, gm)

    def shifted(ref, sh, cols, carry, mu, g, ci, r):
        x = ref[pl.ds(r, w), :]
        width = x.shape[1]
        if has_state:
            src = jnp.concatenate(
                [jnp.broadcast_to(sh[g * nseq + s:g * nseq + s + 1, cols], (c, width))
                 for s in range(nseq)], axis=0)
        else:
            rp = pl.multiple_of(jnp.maximum(r - SUBLANES, 0), SUBLANES)
            prev8 = jnp.where(ci == 0, carry, ref[pl.ds(rp, SUBLANES), :])
            src = jnp.broadcast_to(prev8[SUBLANES - 1:SUBLANES, :], (w, width))
        prev = jnp.where(first, src, pltpu.roll(x, 1, axis=0))
        return x + mu * (prev - x)

    def tile(j, g, ci, r):
        hc = slice(j * LANES, (j + 1) * LANES)
        lc = slice(0, B_LORA_W + B_LORA_A + B_LORA_G)
        cr = [None] * 4 if carries is None else [carries[0][j, g], carries[1][j, g], carries[2][j, g],
                                                  carries[3][g]]
        xr = shifted(zr[j][g], sh_r if has_state else None, hc, cr[0], mu_r[:, hc], g, ci, r)
        xk = shifted(zk[j][g], sh_k if has_state else None, hc, cr[1], mu_k[:, hc], g, ci, r)
        xv = shifted(zv[j][g], sh_v if has_state else None, hc, cr[2], mu_v[:, hc], g, ci, r)
        xl = shifted(zl[g], sh_l if has_state else None, lc, cr[3], mu_l[...], g, ci, r)
        wd = xl[:, 0:B_LORA_W]
        ad = xl[:, B_LORA_W:B_LORA_W + B_LORA_A]
        gd = xl[:, B_LORA_W + B_LORA_A:]
        wx = -(w0[:, hc] + _dot(jnp.tanh(wd), w2[:, hc]))
        w_raw = -(jnp.maximum(wx, 0.0) + jnp.log1p(jnp.exp(-jnp.abs(wx)))) - 0.5
        lw = -jnp.exp(w_raw)
        aa = _sigmoid(a0[:, hc] + _dot(ad, a2[:, hc]))
        gb = _dot(_sigmoid(gd), g2[:, hc])
        yield
        kk = xk * kkp[:, hc]
        kk = kk / jnp.maximum(jnp.sqrt(gsum(kk * kk)), 1e-12)
        k2 = xk * (1.0 + (aa - 1.0) * kap[:, hc])
        a = -kk
        b = kk * aa
        yield
        e = _sel_dot(cum[...], lw)
        yield
        lwc = e[0:w]
        lwl = e[w:n]
        dec_in = jnp.exp(lwc)
        dec_ex = jnp.exp(lwc - lw)
        inv = jnp.exp(-lwc)
        rest = jnp.exp(lwl - lwc)
        a_t = a * dec_ex
        r_t = xr * dec_in
        b_t = b * inv
        k_t = k2 * inv
        gram = _dot_nt(jnp.concatenate([a_t, r_t], axis=0),
                       jnp.concatenate([stack(b_t), stack(k_t)], axis=0))
        yield
        m_ab = gram[0:w, 0:n] * msk[0]
        m_ak = gram[0:w, n:2 * n] * msk[0]
        m_rb = gram[w:n, 0:n] * msk[1]
        m_rk = gram[w:n, n:2 * n] * msk[1]
        p = _dot(m_ab, stack(m_ab))
        tinv = msk[2] + m_ab
        yield
        for lev in range(nsq):
            if lev + 1 < nsq:
                both = _dot(p, jnp.concatenate([stack(p), stack(tinv)], axis=1))
                p = both[:, 0:n]
                tinv = tinv + both[:, n:2 * n]
            else:
                tinv = tinv + _dot(p, stack(tinv))
            yield
        p0a, p0r = [], []
        for s in range(nseq):
            rows = slice(s * c, (s + 1) * c)
            pr = _dot_nt(jnp.concatenate([a_t[rows], r_t[rows]], axis=0), s_ref[j, g * nseq + s])
            p0a.append(pr[0:c])
            p0r.append(pr[c:2 * c])
        p0a = p0a[0] if nseq == 1 else jnp.concatenate(p0a, axis=0)
        p0r = p0r[0] if nseq == 1 else jnp.concatenate(p0r, axis=0)
        yield
        vs = stack(xv)
        rhs = p0a + _dot(m_ak, vs)
        yield
        u_w = _dot(tinv, stack(rhs))
        yield
        y = p0r + _dot(m_rb, stack(u_w)) + _dot(m_rk, vs)
        yield
        b_g = b * rest
        k_g = k2 * rest
        dec_l = jnp.exp(lwl)
        for s in range(nseq):
            rows = slice(s * c, (s + 1) * c)
            upd = _dot_tn(jnp.concatenate([u_w[rows], xv[rows]], axis=0),
                          jnp.concatenate([b_g[rows], k_g[rows]], axis=0))
            s_ref[j, g * nseq + s] = s_ref[j, g * nseq + s] * dec_l[s * c:s * c + 1, :] + gm_f * upd
        yield
        mean = gsum(y) * (1.0 / B_HEAD)
        yield
        d = y - mean
        var = gsum(d * d) * (1.0 / B_HEAD)
        yn = d * lax.rsqrt(var + RWKV_GN_EPS) * gnw[:, hc] + gnb[:, hc]
        yield
        bonus = gsum(xr * k2 * rkp[:, hc])
        o_ref[g, pl.ds(r, w), hc] = ((yn + bonus * xv) * gb).astype(BF16)

    def chunk(ci, carry):
        r = pl.multiple_of(ci * w, w)
        _round_robin([tile(j, g, ci, r) for j in range(hs) for g in range(g_n)])
        return carry

    lax.fori_loop(0, chunks, chunk, 0)

    base = blk - w
    for g in range(g_n):
        per_head = [(zr[j][g], shift_outs[0], j) for j in range(hs)]
        per_head += [(zk[j][g], shift_outs[1], j) for j in range(hs)]
        per_head += [(zv[j][g], shift_outs[2], j) for j in range(hs)]
        for ref, out, j in per_head + [(zl[g], shift_outs[3], 0)]:
            cols = slice(j * LANES, j * LANES + ref.shape[1])
            for s in range(nseq):
                last = base + (s + 1) * c - 1
                out[g * nseq + s:g * nseq + s + 1, cols] = ref[last:last + 1, :]
        if carries is not None:
            for j in range(hs):
                carries[0][j, g] = zr[j][g][blk - SUBLANES:blk, :]
                carries[1][j, g] = zk[j][g][blk - SUBLANES:blk, :]
                carries[2][j, g] = zv[j][g][blk - SUBLANES:blk, :]
            carries[3][g] = zl[g][blk - SUBLANES:blk, :]

    @pl.when(tb == pl.num_programs(2) - 1)
    def _():
        for j in range(hs):
            for s in range(g_n * nseq):
                ss = s_ref[j, s]
                s_out[0, s, 2 * j] = ss[0:B_HEAD, 0:B_HEAD]
                s_out[0, s, 2 * j + 1] = ss[B_HEAD:n, B_HEAD:n]


def _rwkv(z, p, state, shift, prev_out, layer, depth, b, t, row_off):
    pairs = B_HEADS // 2
    til = _Tiling(b, t, RWKV_ROWS, row_off, pairs, 2 * B_HEAD * B_HEAD * 4, RWKV_STREAMS)
    cum, msk, gmat = _rwkv_consts(til.c)
    cb = B_OFF // LANES
    wb = B_WIDTH // LANES
    lora_w = B_LORA_W + B_LORA_A + B_LORA_G
    lb_z = (B_OFF + 3 * B_WIDTH) // lora_w
    lb_s = (3 * B_WIDTH) // lora_w

    def vec(x):
        return x.reshape(1, -1)

    def pspec(rows_, col_off=0):
        return til.hspec(rows_, LANES, col_off)

    in_specs, args = [], []
    for col in (cb, cb + wb, cb + 2 * wb):
        for j, g in til.streams():
            in_specs.append(til.zspec(j, g, LANES, col))
            args.append(z)
    for g in range(til.g):
        in_specs.append(til.zspec(0, g, lora_w, lb_z, per_head=0))
        args.append(z)
    in_specs += [
        pspec(1), pspec(1, wb), pspec(1, 2 * wb),
        pl.BlockSpec((1, lora_w), lambda i, h, tb: (0, lb_s)),
        pspec(1), pspec(1), pspec(1), pspec(1), pspec(1), pspec(1), pspec(1),
        pspec(B_LORA_W), pspec(B_LORA_A), pspec(B_LORA_G),
        _const_spec(cum.shape), _const_spec(msk.shape), _const_spec(gmat.shape),
    ]
    mu = vec(p['rwkv_mu'])
    args += [mu, mu, mu, mu,
             vec(p['rwkv_w0']), vec(p['rwkv_a0']), vec(p['rwkv_kk']), vec(p['rwkv_ka']),
             vec(p['rwkv_rk']), vec(p['rwkv_gn_w']), vec(p['rwkv_gn_b']),
             p['rwkv_w2'].astype(BF16), p['rwkv_a2'].astype(BF16), p['rwkv_g2'].astype(BF16),
             jnp.asarray(cum, BF16), jnp.asarray(msk, F32), jnp.asarray(gmat, BF16)]
    ns = til.g * til.nseq
    if state is not None:
        hw = til.hs * LANES
        wbh = wb // til.hs
        in_specs += [
            pl.BlockSpec((ns, hw), lambda i, h, tb: (i, h)),
            pl.BlockSpec((ns, hw), lambda i, h, tb: (i, wbh + h)),
            pl.BlockSpec((ns, hw), lambda i, h, tb: (i, 2 * wbh + h)),
            pl.BlockSpec((ns, lora_w), lambda i, h, tb: (i, lb_s)),
        ]
        args += [shift, shift, shift, shift]
    has_state, has_alias, aliases, s_spec, s_shape = _state_io(
        til, state, prev_out, layer, depth, b, B_HEADS, 2, B_HEAD, B_HEAD, in_specs, args)
    scratch = [pltpu.VMEM((til.hs, til.g * til.nseq, 2 * B_HEAD, 2 * B_HEAD), F32)]
    if not has_state:
        scratch += [pltpu.VMEM((til.hs, til.g, SUBLANES, LANES), F32) for _ in range(3)]
        scratch += [pltpu.VMEM((til.g, SUBLANES, lora_w), F32)]
    o, s, sh_r, sh_k, sh_v, sh_l = pl.pallas_call(
        functools.partial(_rwkv_kernel, g_n=til.g, hs=til.hs, chunks=til.chunks, nseq=til.nseq, c=til.c,
                          has_state=has_state, has_alias=has_alias),
        grid=(til.steps, til.head_steps, til.nt),
        in_specs=in_specs,
        out_specs=[til.ospec(LANES), s_spec,
                   pl.BlockSpec((ns, til.hs * LANES), lambda i, h, tb: (i, h)),
                   pl.BlockSpec((ns, til.hs * LANES), lambda i, h, tb: (i, h)),
                   pl.BlockSpec((ns, til.hs * LANES), lambda i, h, tb: (i, h)),
                   pl.BlockSpec((ns, lora_w), lambda i, h, tb: (i, 0))],
        out_shape=[til.oshape(B_WIDTH), s_shape] + [
            jax.ShapeDtypeStruct((b, wd), F32) for wd in (B_WIDTH, B_WIDTH, B_WIDTH, lora_w)],
        scratch_shapes=scratch,
        input_output_aliases=aliases,
        compiler_params=_cparams(("parallel", "parallel", "arbitrary")),
        name="rwkv7",
    )(*args)
    return o.reshape(b * t, B_WIDTH), s, jnp.concatenate([sh_r, sh_k, sh_v, sh_l], axis=1)


def _merge_kernel(oa, ob, oc, wa, wb, wc, ga, gb, gc, *rest):
    o_ref = rest[-1]
    acc = _sigmoid(ga[...]) * jnp.dot(oa[...], wa[...], preferred_element_type=F32)
    acc = acc + _sigmoid(gb[...]) * jnp.dot(ob[...], wb[...], preferred_element_type=F32)
    acc = acc + _sigmoid(gc[...]) * jnp.dot(oc[...], wc[...], preferred_element_type=F32)
    o_ref[...] = acc.astype(BF16)


def _merge(z, oa, ob, oc, wa, wb, wc, layer, row_off, prev_out):
    m = oa.shape[0]
    tm = _pick(m, (1024, 512, 256, 128))
    tn = 256
    gblk = G_OFF // tn
    nb = D_MODEL // tn
    assert G_OFF % tn == 0 and row_off % tm == 0
    off = row_off // tm

    def ospec():
        return pl.BlockSpec((tm, oa.shape[1]), lambda i, j: (i, 0))

    def wspec():
        return pl.BlockSpec((None, wa.shape[1], tn), lambda i, j: (layer, 0, j))

    def gspec(br):
        return pl.BlockSpec((tm, tn), lambda i, j: (i + off, gblk + br * nb + j))

    in_specs = [ospec(), ospec(), ospec(), wspec(), wspec(), wspec(), gspec(0), gspec(1), gspec(2)]
    args = [oa, ob, oc, wa, wb, wc, z, z, z]
    aliases = {}
    if prev_out is not None:
        aliases = {len(args): 0}
        in_specs.append(pl.BlockSpec(memory_space=pl.ANY))
        args.append(prev_out)
    return pl.pallas_call(
        _merge_kernel,
        grid=(m // tm, nb),
        in_specs=in_specs,
        out_specs=pl.BlockSpec((tm, tn), lambda i, j: (i + off, j)),
        out_shape=jax.ShapeDtypeStruct((z.shape[0], D_MODEL), BF16),
        input_output_aliases=aliases,
        compiler_params=_cparams(("parallel", "arbitrary")),
        name="merge",
    )(*args)


def _proj_res_kernel(m_ref, w_ref, g_ref, x_ref, o_ref):
    y = jnp.dot(m_ref[...], w_ref[...], preferred_element_type=F32)
    yn = y * lax.rsqrt(jnp.mean(y * y, axis=-1, keepdims=True) + NORM_EPS) * g_ref[...]
    o_ref[...] = x_ref[...] + yn


def _proj_res(mrg, w, g, x, layer):
    m = x.shape[0]
    tm = _pick(m, (512, 256, 128))
    return pl.pallas_call(
        _proj_res_kernel,
        grid=(m // tm,),
        in_specs=[
            pl.BlockSpec((tm, D_MODEL), lambda i: (i, 0)),
            pl.BlockSpec((None, D_MODEL, D_MODEL), lambda i: (layer, 0, 0)),
            pl.BlockSpec((1, D_MODEL), lambda i: (0, 0)),
            pl.BlockSpec((tm, D_MODEL), lambda i: (i, 0)),
        ],
        out_specs=pl.BlockSpec((tm, D_MODEL), lambda i: (i, 0)),
        out_shape=jax.ShapeDtypeStruct((m, D_MODEL), F32),
        compiler_params=_cparams(("parallel",)),
        name="proj_res",
    )(mrg, w, g.reshape(1, D_MODEL), x)


def _gelu(x):
    return 0.5 * x * (1.0 + jnp.tanh(math.sqrt(2.0 / math.pi) * (x + 0.044715 * (x * x * x))))


def _up_act_kernel(*refs, tm, t, blocks_per_seq, has_state, has_alias):
    it = iter(refs)
    x_ref, g_ref, wa, wb, cw, cb = (next(it) for _ in range(6))
    st = next(it) if has_state else None
    if has_alias:
        next(it)
    o_ref, nc_ref, xn_ref = next(it), next(it), next(it)
    tail = None if has_state else next(it)
    i = pl.program_id(0)
    j = pl.program_id(1)
    tn = wa.shape[1]
    sub = UP_SUB

    @pl.when(j == 0)
    def _():
        rows = min(tm, LANES)

        def body(r, carry):
            r0 = pl.multiple_of(r * rows, rows)
            x = x_ref[pl.ds(r0, rows), :]
            ms = jnp.mean(x * x, axis=-1, keepdims=True)
            xn_ref[pl.ds(r0, rows), :] = (x * lax.rsqrt(ms + NORM_EPS) * g_ref[...]).astype(BF16)
            return carry
        lax.fori_loop(0, tm // rows, body, 0)

    if has_state:
        ns = tm // t
        tt = lax.broadcasted_iota(jnp.int32, (ns, t, sub), 1)
    else:
        seq_start = (i % blocks_per_seq) == 0
        rr = lax.broadcasted_iota(jnp.int32, (tm, sub), 0)
    for c in range(tn // sub):
        cols = slice(c * sub, (c + 1) * sub)
        ua = jnp.dot(xn_ref[...], wa[:, cols].astype(BF16), preferred_element_type=F32)
        ub = jnp.dot(xn_ref[...], wb[:, cols].astype(BF16), preferred_element_type=F32)
        if has_state:
            x3 = ua.reshape(ns, t, sub)
            s_old = st[:, 0:1, cols]
            s_new = st[:, 1:2, cols]
            prev1 = jnp.where(tt >= 1, pltpu.roll(x3, 1, axis=1), s_new)
            prev2 = jnp.where(tt >= 2, pltpu.roll(x3, 2, axis=1), jnp.where(tt == 1, s_new, s_old))
            prev1 = prev1.reshape(tm, sub)
            prev2 = prev2.reshape(tm, sub)
            nc_ref[:, :, cols] = x3[:, t - (CONV_W - 1):, :]
        else:
            h = jnp.where(seq_start, 0.0, tail[j, :, cols])
            h1 = h[SUBLANES - 1:SUBLANES, :]
            h2 = h[SUBLANES - 2:SUBLANES - 1, :]
            prev1 = jnp.where(rr == 0, h1, pltpu.roll(ua, 1, axis=0))
            prev2 = jnp.where(rr == 0, h2, jnp.where(rr == 1, h1, pltpu.roll(ua, 2, axis=0)))
            tail[j, :, cols] = ua[tm - SUBLANES:, :]
            nc_ref[0, :, cols] = ua[tm - (CONV_W - 1):, :]
        conv = cb[:, cols] + cw[0:1, cols] * prev2 + cw[1:2, cols] * prev1 + cw[2:3, cols] * ua
        o_ref[:, cols] = (_gelu(conv) * ub).astype(BF16)


def _up_act(x1, g, w_up, cw, cb, state, prev_out, layer, b, t, row_off):
    m = b * t
    total = x1.shape[0]
    has_state = state is not None
    has_alias = prev_out is not None
    tn = UP_TN
    nj = D_FF // tn
    if has_state:
        tm = _pick(m, (1024, 512, 256, 128))
        assert tm % t == 0 and t == SUBLANES
        blocks_per_seq = 1
        nc_spec = pl.BlockSpec((tm // t, CONV_W - 1, tn), lambda i, j: (i, 0, j))
    else:
        tm = _pick(t, (1024, 512, 256, 128))
        blocks_per_seq = t // tm
        nc_spec = pl.BlockSpec((1, CONV_W - 1, tn), lambda i, j: (i, 0, j))
    nc_rows = b * blocks_per_seq
    off = row_off // tm
    assert row_off % tm == 0
    in_specs = [
        pl.BlockSpec((tm, D_MODEL), lambda i, j: (i + off, 0)),
        pl.BlockSpec((1, D_MODEL), lambda i, j: (0, 0)),
        pl.BlockSpec((None, D_MODEL, tn), lambda i, j: (layer, 0, j)),
        pl.BlockSpec((None, D_MODEL, tn), lambda i, j: (layer, 0, nj + j)),
        pl.BlockSpec((CONV_W, tn), lambda i, j: (0, j)),
        pl.BlockSpec((1, tn), lambda i, j: (0, j)),
    ]
    args = [x1, g.reshape(1, D_MODEL), w_up, w_up, cw, cb.reshape(1, D_FF)]
    if has_state:
        in_specs.append(pl.BlockSpec((tm // t, CONV_W - 1, tn), lambda i, j: (i, 0, j)))
        args.append(state)
    aliases = {}
    if has_alias:
        aliases = {len(args): 0}
        in_specs.append(pl.BlockSpec(memory_space=pl.ANY))
        args.append(prev_out)
    scratch = [pltpu.VMEM((tm, D_MODEL), BF16)]
    if not has_state:
        scratch.append(pltpu.VMEM((nj, SUBLANES, tn), F32))
    act, nc = pl.pallas_call(
        functools.partial(_up_act_kernel, tm=tm, t=t, blocks_per_seq=blocks_per_seq,
                          has_state=has_state, has_alias=has_alias),
        grid=(m // tm, nj),
        in_specs=in_specs,
        out_specs=[pl.BlockSpec((tm, tn), lambda i, j: (i + off, j)), nc_spec],
        out_shape=[jax.ShapeDtypeStruct((total, D_FF), BF16),
                   jax.ShapeDtypeStruct((nc_rows, CONV_W - 1, D_FF), F32)],
        scratch_shapes=scratch,
        input_output_aliases=aliases,
        compiler_params=_cparams(("arbitrary", "arbitrary")),
        name="up_act",
    )(*args)
    return act, nc[blocks_per_seq - 1::blocks_per_seq]


def _down_res_kernel(a_ref, w_ref, g_ref, x_ref, o_ref, acc):
    kstep = pl.program_id(1)

    @pl.when(kstep == 0)
    def _():
        acc[...] = jnp.zeros_like(acc)

    acc[...] += jnp.dot(a_ref[...], w_ref[...], preferred_element_type=F32)

    @pl.when(kstep == pl.num_programs(1) - 1)
    def _():
        y = acc[...]
        yn = y * lax.rsqrt(jnp.mean(y * y, axis=-1, keepdims=True) + NORM_EPS) * g_ref[...]
        o_ref[...] = x_ref[...] + yn


def _down_res(act, wd, g, x1, layer, row_off, rows):
    tm = _pick(rows, (512, 256, 128))
    tk = DOWN_TK
    off = row_off // tm
    assert row_off % tm == 0
    return pl.pallas_call(
        _down_res_kernel,
        grid=(rows // tm, D_FF // tk),
        in_specs=[
            pl.BlockSpec((tm, tk), lambda i, k: (i + off, k)),
            pl.BlockSpec((None, tk, D_MODEL), lambda i, k: (layer, k, 0)),
            pl.BlockSpec((1, D_MODEL), lambda i, k: (0, 0)),
            pl.BlockSpec((tm, D_MODEL), lambda i, k: (i + off, 0)),
        ],
        out_specs=pl.BlockSpec((tm, D_MODEL), lambda i, k: (i, 0)),
        out_shape=jax.ShapeDtypeStruct((rows, D_MODEL), F32),
        scratch_shapes=[pltpu.VMEM((tm, D_MODEL), F32)],
        compiler_params=_cparams(("parallel", "arbitrary")),
        name="down_res",
    )(act, wd, g.reshape(1, D_MODEL), x1)


def _layer(x, groups, lb, p, layer, depth, prev, split_out):
    z = _rms_matmul(x, p['pre_mix_g'], p['w_in'], layer, tn=1280)
    states = []
    mrg = None
    row = 0
    for gi, (b, t, t0, st) in enumerate(groups):
        s_a, s_b, s_sh, s_c, _ = st if st is not None else (None,) * 5
        pv = prev[gi] if prev is not None else (None,) * 3
        o_a, n_a = _hgrn(z, lb, p['a_norm_g'], s_a, pv[0], layer, depth, b, t, row)
        o_b, n_b, n_sh = _rwkv(z, p, s_b, s_sh, pv[1], layer, depth, b, t, row)
        o_c, n_c = _retention(z, s_c, pv[2], layer, depth, b, t, t0, row)
        mrg = _merge(z, o_a, o_b, o_c, p['w_br_a'], p['w_br_b'], p['w_br_c'], layer, row, mrg)
        states.append([n_a, n_b, n_sh, n_c])
        row += b * t
    x1 = _proj_res(mrg, p['w_out'], p['post_mix_g'], x, layer)
    act = None
    row = 0
    for gi, (b, t, t0, st) in enumerate(groups):
        s_cv = st[4] if st is not None else None
        act, n_cv = _up_act(x1, p['pre_ffn_g'], p['w_up'], p['conv_w'], p['conv_b'], s_cv, act,
                            layer, b, t, row)
        states[gi].append(n_cv)
        row += b * t
    if split_out:
        x2, row = [], 0
        for (b, t, _, _) in groups:
            x2.append(_down_res(act, p['w_down'], p['post_ffn_g'], x1, layer, row, b * t))
            row += b * t
    else:
        x2 = _down_res(act, p['w_down'], p['post_ffn_g'], x1, layer, 0, x1.shape[0])
    return x2, states


def kernel(x_prompt, x_sample, state_hgrn, state_rwkv, state_rwkv_shift, state_ret, state_conv,
           lb_logits, pre_mix_g, w_in, a_norm_g, rwkv_mu, rwkv_w0, rwkv_w2, rwkv_a0, rwkv_a2,
           rwkv_g2, rwkv_kk, rwkv_ka, rwkv_rk, rwkv_gn_w, rwkv_gn_b, w_br_a, w_br_b, w_br_c,
           w_out, post_mix_g, pre_ffn_g, w_up, conv_w, conv_b, w_down, post_ffn_g):
    depth = w_in.shape[0]
    bp, tp, _ = x_prompt.shape
    bs, ts, _ = x_sample.shape
    past_len = 16384
    lb_soft = jax.nn.softmax(lb_logits.astype(F32), axis=0)
    lbs = jnp.cumsum(lb_soft, axis=0) - lb_soft[0]
    big = {'w_in': w_in.astype(BF16), 'w_br_a': w_br_a.astype(BF16), 'w_br_b': w_br_b.astype(BF16),
           'w_br_c': w_br_c.astype(BF16), 'w_out': w_out.astype(BF16), 'w_up': w_up,
           'w_down': w_down.astype(BF16)}
    x = jnp.concatenate([x_prompt.reshape(bp * tp, D_MODEL), x_sample.reshape(bs * ts, D_MODEL)], axis=0)
    small = [[[], []], [[], []]]
    prev = None
    for l in range(depth):
        p = dict(big)
        p.update({
            'pre_mix_g': pre_mix_g[l], 'a_norm_g': a_norm_g[l],
            'rwkv_mu': rwkv_mu[l], 'rwkv_w0': rwkv_w0[l], 'rwkv_w2': rwkv_w2[l],
            'rwkv_a0': rwkv_a0[l], 'rwkv_a2': rwkv_a2[l], 'rwkv_g2': rwkv_g2[l],
            'rwkv_kk': rwkv_kk[l], 'rwkv_ka': rwkv_ka[l], 'rwkv_rk': rwkv_rk[l],
            'rwkv_gn_w': rwkv_gn_w[l], 'rwkv_gn_b': rwkv_gn_b[l],
            'post_mix_g': post_mix_g[l], 'pre_ffn_g': pre_ffn_g[l],
            'conv_w': conv_w[l], 'conv_b': conv_b[l], 'post_ffn_g': post_ffn_g[l],
        })
        groups = [
            (bp, tp, 0, None),
            (bs, ts, past_len, (state_hgrn, state_rwkv, state_rwkv_shift[l], state_ret, state_conv[l])),
        ]
        x, states = _layer(x, groups, lbs[l], p, l, depth, prev, l == depth - 1)
        prev = [(st[0], st[1], st[3]) for st in states]
        for gi, st in enumerate(states):
            small[gi][0].append(st[2])
            small[gi][1].append(st[4])
    y_p = x[0].reshape(bp, tp, D_MODEL)
    y_s = x[1].reshape(bs, ts, D_MODEL)
    outs = []
    for gi in range(2):
        outs += [prev[gi][0], prev[gi][1], jnp.stack(small[gi][0]), prev[gi][2], jnp.stack(small[gi][1])]
    return (y_p, y_s, *outs)
```

```python
import functools
import itertools
import math

import jax
import jax.numpy as jnp
import numpy as np
from jax import lax
from jax.experimental import pallas as pl
from jax.experimental.pallas import tpu as pltpu

F32 = jnp.float32
BF16 = jnp.bfloat16

D_MODEL = 2048
A_HEADS, A_DK, A_DV = 8, 128, 128
A_QK = A_HEADS * A_DK
A_WIDTH = A_HEADS * A_DV
F_TINY = 1e-30
B_HEAD = 64
B_WIDTH = 1024
B_HEADS = B_WIDTH // B_HEAD
B_LORA_W, B_LORA_A, B_LORA_G = 64, 64, 128
RWKV_GN_EPS = 64e-5
C_HEADS, C_DK, C_DV = 4, 128, 256
C_QK = C_HEADS * C_DK
C_WIDTH = C_HEADS * C_DV
ROPE_BASE = 10000.0
A_COLS = 2 * A_QK + 2 * A_WIDTH
B_COLS = 3 * B_WIDTH + B_LORA_W + B_LORA_A + B_LORA_G
C_COLS = 2 * C_QK + 2 * C_WIDTH
N_BRANCH = 3
P_COLS = A_COLS + B_COLS + C_COLS + N_BRANCH * D_MODEL
B_OFF = A_COLS
C_OFF = A_COLS + B_COLS
G_OFF = A_COLS + B_COLS + C_COLS
D_FF = 5632
CONV_W = 3
NORM_EPS = 1e-6

LANES = 128
SUBLANES = 8
MXU_COLS = 256
MIX_ROWS = 128
RWKV_ROWS = 64
MIX_TIME_BLOCK = 512
MIX_STREAMS = 4
HGRN_STREAMS = 8
RET_STREAMS = 8
RWKV_STREAMS = 16
STATE_WINDOW_BYTES = 4 * 1024 * 1024
UP_TN = 512
UP_SUB = 256
DOWN_TK = 2816
VMEM_LIMIT = 56 * 1024 * 1024


def _cparams(sem):
    return pltpu.CompilerParams(dimension_semantics=sem, vmem_limit_bytes=VMEM_LIMIT)


def _dot(a, b):
    return jnp.dot(a.astype(BF16), b.astype(BF16), preferred_element_type=F32)


def _dot_nt(a, b):
    return lax.dot_general(a.astype(BF16), b.astype(BF16), (((1,), (1,)), ((), ())),
                           preferred_element_type=F32)


def _dot_tn(a, b):
    return lax.dot_general(a.astype(BF16), b.astype(BF16), (((0,), (0,)), ((), ())),
                           preferred_element_type=F32)


def _split(x):
    hi = x.astype(BF16)
    lo = (x - hi.astype(F32)).astype(BF16)
    return hi, lo


def _sel_dot(m, x):
    hi, lo = _split(x)
    n = x.shape[1]
    both = jnp.dot(m, jnp.concatenate([hi, lo], axis=1), preferred_element_type=F32)
    return both[:, :n] + both[:, n:]


def _dot_sel(x, m):
    hi, lo = _split(x)
    return (jnp.dot(hi, m, preferred_element_type=F32)
            + jnp.dot(lo, m, preferred_element_type=F32))


def _sigmoid(x):
    return jax.nn.sigmoid(x)


def _silu(x):
    return x * jax.nn.sigmoid(x)


def _round_robin(gens):
    for _ in itertools.zip_longest(*gens):
        pass


def _pick(n, cands):
    for c in cands:
        if n % c == 0:
            return c
    raise ValueError(f"no tile in {cands} divides {n}")


def _rms_matmul_kernel(x_ref, g_ref, w_ref, o_ref, xn_ref, *, tm, sub):
    @pl.when(pl.program_id(1) == 0)
    def _():
        def body(i, carry):
            r = pl.multiple_of(i * sub, sub)
            x = x_ref[pl.ds(r, sub), :]
            ms = jnp.mean(x * x, axis=-1, keepdims=True)
            xn_ref[pl.ds(r, sub), :] = (x * lax.rsqrt(ms + NORM_EPS) * g_ref[...]).astype(BF16)
            return carry
        lax.fori_loop(0, tm // sub, body, 0)

    tn = w_ref.shape[1]
    for c in range(tn // MXU_COLS):
        cols = slice(c * MXU_COLS, (c + 1) * MXU_COLS)
        o_ref[:, cols] = jnp.dot(xn_ref[...], w_ref[:, cols].astype(BF16), preferred_element_type=F32)


def _rms_matmul(x, g, w, layer, tn):
    m, k = x.shape
    n = w.shape[2]
    tm = _pick(m, (1024, 512, 256, 128))
    sub = min(tm, 128)
    assert tn % MXU_COLS == 0
    return pl.pallas_call(
        functools.partial(_rms_matmul_kernel, tm=tm, sub=sub),
        grid=(m // tm, n // tn),
        in_specs=[
            pl.BlockSpec((tm, k), lambda i, j: (i, 0), pipeline_mode=pl.Buffered(1)),
            pl.BlockSpec((1, k), lambda i, j: (0, 0)),
            pl.BlockSpec((None, k, tn), lambda i, j: (layer, 0, j)),
        ],
        out_specs=pl.BlockSpec((tm, tn), lambda i, j: (i, j)),
        out_shape=jax.ShapeDtypeStruct((m, n), F32),
        scratch_shapes=[pltpu.VMEM((tm, k), BF16)],
        compiler_params=_cparams(("parallel", "arbitrary")),
        name="rms_matmul",
    )(x, g.reshape(1, k), w)


class _Tiling:
    def __init__(self, b, t, rows, row_off, heads, state_bytes, streams_total):
        if t >= rows:
            self.nseq, self.c = 1, rows
            self.blk = _pick(t, (MIX_TIME_BLOCK, rows))
            self.nt = t // self.blk
            n_streams = b
        else:
            assert rows % t == 0 and b % (rows // t) == 0
            self.nseq, self.c = rows // t, t
            self.blk, self.nt = rows, 1
            n_streams = b // self.nseq
        self.g = _pick(n_streams, (MIX_STREAMS, 2, 1))
        self.hs = _pick(heads, (max(streams_total // self.g, 1), 2, 1))
        while self.g * self.nseq * self.hs * state_bytes > STATE_WINDOW_BYTES and self.hs > 1:
            self.hs //= 2
        while self.g * self.nseq * self.hs * state_bytes > STATE_WINDOW_BYTES and self.g > 1:
            self.g //= 2
        self.head_steps = heads // self.hs
        self.steps = n_streams // self.g
        self.n_streams = n_streams
        self.chunks = self.blk // rows
        assert row_off % self.blk == 0
        self.off = row_off // self.blk

    def streams(self):
        return [(j, g) for j in range(self.hs) for g in range(self.g)]

    def zspec(self, j, stream, width, col_blk, per_head=1):
        g, nt, off, hs = self.g, self.nt, self.off, self.hs
        return pl.BlockSpec(
            (self.blk, width),
            lambda i, h, tb: (off + (i * g + stream) * nt + tb, col_blk + per_head * (h * hs + j)))

    def hspec(self, rows, width, blk_off=0):
        assert blk_off % self.hs == 0
        off = blk_off // self.hs
        return pl.BlockSpec((rows, self.hs * width), lambda i, h, tb: (0, off + h))

    def ospec(self, width):
        return pl.BlockSpec((self.g, self.blk, self.hs * width), lambda i, h, tb: (i, tb, h))

    def oshape(self, width):
        return jax.ShapeDtypeStruct((self.n_streams, self.nt * self.blk, width), BF16)

    def sspec(self, layer, heads_per_step, d0, d1):
        return pl.BlockSpec((1, self.g * self.nseq, self.hs * heads_per_step, d0, d1),
                            lambda i, h, tb: (layer, i, h, 0, 0))


def _const_spec(shape):
    nd = len(shape)
    return pl.BlockSpec(shape, lambda i, h, tb: (0,) * nd)


def _state_io(til, state, prev_out, layer, depth, b, heads, heads_per_step, d0, d1,
              in_specs, args):
    has_state = state is not None
    if has_state:
        in_specs.append(til.sspec(layer, heads_per_step, d0, d1))
        args.append(state)
    aliases = {}
    if prev_out is not None:
        aliases = {len(args): 1}
        in_specs.append(pl.BlockSpec(memory_space=pl.ANY))
        args.append(prev_out)
    out_spec = til.sspec(layer, heads_per_step, d0, d1)
    out_shape = jax.ShapeDtypeStruct((depth, b, heads, d0, d1), F32)
    return has_state, prev_out is not None, aliases, out_spec, out_shape


@functools.lru_cache(maxsize=None)
def _hgrn_consts(c):
    n = MIX_ROWS
    nlev = int(math.log2(c))
    t = np.arange(n)
    u = np.arange(n)[None, :]
    blk = t // c
    same = blk[:, None] == blk[None, :]
    mats = [same & (u <= t[:, None]), same]
    masks = [np.eye(n, dtype=bool)]
    for lev in range(nlev):
        h = 1 << lev
        base = (t // (2 * h)) * (2 * h)
        mid = base + h
        upper = t >= mid
        e_up = (u >= mid[:, None]) & (u <= t[:, None])
        e_lo = (u >= t[:, None] + 1) & (u <= mid[:, None] - 1)
        mats.append(np.where(upper[:, None], e_up, e_lo))
        masks.append((base[:, None] == base[None, :]) & upper[:, None] & (~upper)[None, :])
    sel = np.concatenate(mats, 0).astype(np.float32)
    msk = np.stack(masks).astype(np.float32)
    return sel, msk, nlev


def _hgrn_kernel(*refs, g_n, hs, chunks, nseq, c, nlev, has_state, has_alias):
    it = iter(refs)
    zq, zf, zi, zg = ([[next(it) for _ in range(g_n)] for _ in range(hs)] for _ in range(4))
    lb, gn, sel, msk = (next(it) for _ in range(4))
    s0 = next(it) if has_state else None
    if has_alias:
        next(it)
    o_ref, s_out, s_ref = next(it), next(it), next(it)
    n = MIX_ROWS
    tb = pl.program_id(2)

    @pl.when(tb == 0)
    def _():
        if has_state:
            for j in range(hs):
                s_ref[j] = s0[0, :, j]
        else:
            s_ref[...] = jnp.zeros_like(s_ref)

    def tile(j, g, r):
        hcols = slice(j * LANES, (j + 1) * LANES)
        lbv = lb[:, hcols]
        xq = zq[j][g][pl.ds(r, n), :]
        fa = zf[j][g][pl.ds(r, n), :]
        v = zi[j][g][pl.ds(r, n), :]
        xg = zg[j][g][pl.ds(r, n), :]
        q = _silu(xq)
        f_gate = lbv + (1.0 - lbv) * _sigmoid(fa)
        gl = jnp.log(jnp.maximum(f_gate, F_TINY))
        k = (1.0 - lbv) * _sigmoid(-fa)
        e = _sel_dot(sel[...], gl)
        yield
        b = e[0:n]
        bl = e[n:2 * n]
        scores = msk[0] * _dot_nt(q, k)
        for lev in range(nlev):
            x = jnp.exp(e[(lev + 2) * n:(lev + 3) * n])
            scores = scores + msk[lev + 1] * _dot_nt(q * x, k * x)
        yield
        o = _dot(scores, v)
        yield
        qe = q * jnp.exp(b)
        kt = k * jnp.exp(bl - b)
        dt = jnp.exp(bl).T
        outs = []
        for s in range(nseq):
            rows = slice(s * c, (s + 1) * c)
            ss = s_ref[j, g * nseq + s]
            outs.append(_dot(qe[rows], ss))
            dcol = jnp.broadcast_to(dt[:, s * c:s * c + 1], (A_DK, A_DV))
            s_ref[j, g * nseq + s] = ss * dcol + _dot_tn(kt[rows], v[rows])
        o = o + (outs[0] if nseq == 1 else jnp.concatenate(outs, axis=0))
        on = o * lax.rsqrt(jnp.mean(o * o, axis=-1, keepdims=True) + NORM_EPS) * gn[:, hcols]
        o_ref[g, pl.ds(r, n), hcols] = (on * _silu(xg)).astype(BF16)

    def chunk(ci, carry):
        r = pl.multiple_of(ci * n, n)
        _round_robin([tile(j, g, r) for j in range(hs) for g in range(g_n)])
        return carry

    lax.fori_loop(0, chunks, chunk, 0)

    @pl.when(tb == pl.num_programs(2) - 1)
    def _():
        for j in range(hs):
            s_out[0, :, j] = s_ref[j]


def _hgrn(z, lb, gn, state, prev_out, layer, depth, b, t, row_off):
    til = _Tiling(b, t, MIX_ROWS, row_off, A_HEADS, A_DK * A_DV * 4, HGRN_STREAMS)
    sel, msk, nlev = _hgrn_consts(til.c)
    qk_blocks = A_QK // LANES
    in_specs, args = [], []
    for col in range(4):
        for j, g in til.streams():
            in_specs.append(til.zspec(j, g, LANES, col * qk_blocks))
            args.append(z)
    in_specs += [
        til.hspec(1, LANES), til.hspec(1, LANES),
        _const_spec(sel.shape), _const_spec(msk.shape),
    ]
    args += [lb.reshape(1, A_QK), gn.reshape(1, A_WIDTH), jnp.asarray(sel, BF16), jnp.asarray(msk, F32)]
    has_state, has_alias, aliases, s_spec, s_shape = _state_io(
        til, state, prev_out, layer, depth, b, A_HEADS, 1, A_DK, A_DV, in_specs, args)
    o, s = pl.pallas_call(
        functools.partial(_hgrn_kernel, g_n=til.g, hs=til.hs, chunks=til.chunks, nseq=til.nseq, c=til.c,
                          nlev=nlev, has_state=has_state, has_alias=has_alias),
        grid=(til.steps, til.head_steps, til.nt),
        in_specs=in_specs,
        out_specs=[til.ospec(LANES), s_spec],
        out_shape=[til.oshape(A_WIDTH), s_shape],
        scratch_shapes=[pltpu.VMEM((til.hs, til.g * til.nseq, A_DK, A_DV), F32)],
        input_output_aliases=aliases,
        compiler_params=_cparams(("parallel", "parallel", "arbitrary")),
        name="hgrn2",
    )(*args)
    return o.reshape(b * t, A_WIDTH), s


def _ret_tables(c):
    n = MIX_ROWS
    log_g = jnp.log1p(-jnp.exp2(-5.0 - jnp.arange(C_HEADS, dtype=F32)))
    t = np.arange(n)
    tt = (t % c).astype(np.float32)
    blk = t // c
    rel = tt[:, None] - tt[None, :]
    same = (blk[:, None] == blk[None, :]) & (rel >= 0)
    dmat = jnp.where(same[None], jnp.exp(log_g[:, None, None] * np.maximum(rel, 0.0)[None]), 0.0)
    inner = jnp.exp(log_g[:, None] * (tt[None, :] + 1.0))
    tail = jnp.exp(log_g[:, None] * (c - 1.0 - tt[None, :]))
    total = jnp.exp(log_g * c)
    shape = (C_HEADS, n, n)
    tab = jnp.stack([dmat, jnp.broadcast_to(inner[:, :, None], shape),
                     jnp.broadcast_to(tail[:, :, None], shape)], axis=1)
    tot = jnp.broadcast_to(total[:, None, None], (C_HEADS, 1, C_DV))
    return tab.astype(F32), tot.astype(F32)


def _rope_tables(t0, t, reps):
    half = C_DK // 2
    inv = ROPE_BASE ** (-jnp.arange(half, dtype=F32) / half)
    pos = t0 + jnp.arange(t, dtype=F32)
    ang = pos[:, None] * inv[None, :]
    cos, sin = jnp.cos(ang), jnp.sin(ang)
    cosf = jnp.concatenate([cos, cos], axis=-1)
    sinf = jnp.concatenate([-sin, sin], axis=-1)
    return jnp.tile(cosf, (reps, 1)), jnp.tile(sinf, (reps, 1))


def _ret_kernel(*refs, g_n, hs, chunks, nseq, c, has_state, has_alias):
    it = iter(refs)
    zq, zk, zv, zg = ([[next(it) for _ in range(g_n)] for _ in range(hs)] for _ in range(4))
    cos, sin, tab, tot = (next(it) for _ in range(4))
    s0 = next(it) if has_state else None
    if has_alias:
        next(it)
    o_ref, s_out, s_ref = next(it), next(it), next(it)
    n = MIX_ROWS
    half = C_DK // 2
    tb = pl.program_id(2)

    @pl.when(tb == 0)
    def _():
        if has_state:
            for j in range(hs):
                s_ref[j] = s0[0, :, j]
        else:
            s_ref[...] = jnp.zeros_like(s_ref)

    def tile(j, g, r, cs, sn):
        xq = zq[j][g][pl.ds(r, n), :]
        xk = zk[j][g][pl.ds(r, n), :]
        v = zv[j][g][pl.ds(r, n), :]
        xg = zg[j][g][pl.ds(r, n), :]
        q = xq * cs + pltpu.roll(xq, half, axis=1) * sn
        k = (xk * cs + pltpu.roll(xk, half, axis=1) * sn) * (C_DK ** -0.5)
        scores = _dot_nt(q, k) * tab[j, 0]
        yield
        o = _dot(scores, v)
        yield
        qi = q * tab[j, 1]
        ktl = k * tab[j, 2]
        outs = []
        for s in range(nseq):
            rows = slice(s * c, (s + 1) * c)
            ss = s_ref[j, g * nseq + s]
            outs.append(_dot(qi[rows], ss))
            s_ref[j, g * nseq + s] = tot[j] * ss + _dot_tn(ktl[rows], v[rows])
        o = o + (outs[0] if nseq == 1 else jnp.concatenate(outs, axis=0))
        on = o * lax.rsqrt(jnp.mean(o * o, axis=-1, keepdims=True) + NORM_EPS)
        o_ref[g, pl.ds(r, n), j * C_DV:(j + 1) * C_DV] = (on * _silu(xg)).astype(BF16)

    def chunk(ci, carry):
        r = pl.multiple_of(ci * n, n)
        cs = cos[pl.ds(r, n), :]
        sn = sin[pl.ds(r, n), :]
        _round_robin([tile(j, g, r, cs, sn) for j in range(hs) for g in range(g_n)])
        return carry

    lax.fori_loop(0, chunks, chunk, 0)

    @pl.when(tb == pl.num_programs(2) - 1)
    def _():
        for j in range(hs):
            s_out[0, :, j] = s_ref[j]


def _retention(z, state, prev_out, layer, depth, b, t, t0, row_off):
    til = _Tiling(b, t, MIX_ROWS, row_off, C_HEADS, C_DK * C_DV * 4, RET_STREAMS)
    tab, tot = _ret_tables(til.c)
    cosf, sinf = _rope_tables(t0, t, max(MIX_ROWS // t, 1))
    qb = C_OFF // C_DK
    vb = (C_OFF + 2 * C_QK) // C_DV
    gb = (C_OFF + 2 * C_QK + C_WIDTH) // C_DV
    in_specs, args = [], []
    for width, col in ((C_DK, qb), (C_DK, qb + C_HEADS), (C_DV, vb), (C_DV, gb)):
        for j, g in til.streams():
            in_specs.append(til.zspec(j, g, width, col))
            args.append(z)
    in_specs += [
        pl.BlockSpec((til.blk, C_DK), lambda i, h, tb: (tb, 0)),
        pl.BlockSpec((til.blk, C_DK), lambda i, h, tb: (tb, 0)),
        pl.BlockSpec((til.hs, 3, MIX_ROWS, MIX_ROWS), lambda i, h, tb: (h, 0, 0, 0)),
        pl.BlockSpec((til.hs, 1, C_DV), lambda i, h, tb: (h, 0, 0)),
    ]
    args += [cosf, sinf, tab, tot]
    has_state, has_alias, aliases, s_spec, s_shape = _state_io(
        til, state, prev_out, layer, depth, b, C_HEADS, 1, C_DK, C_DV, in_specs, args)
    o, s = pl.pallas_call(
        functools.partial(_ret_kernel, g_n=til.g, hs=til.hs, chunks=til.chunks, nseq=til.nseq, c=til.c,
                          has_state=has_state, has_alias=has_alias),
        grid=(til.steps, til.head_steps, til.nt),
        in_specs=in_specs,
        out_specs=[til.ospec(C_DV), s_spec],
        out_shape=[til.oshape(C_WIDTH), s_shape],
        scratch_shapes=[pltpu.VMEM((til.hs, til.g * til.nseq, C_DK, C_DV), F32)],
        input_output_aliases=aliases,
        compiler_params=_cparams(("parallel", "parallel", "arbitrary")),
        name="retention",
    )(*args)
    return o.reshape(b * t, C_WIDTH), s


@functools.lru_cache(maxsize=None)
def _rwkv_consts(c):
    w = RWKV_ROWS
    t = np.arange(w)
    blk = t // c
    same = blk[:, None] == blk[None, :]
    tri = same & (t[None, :] <= t[:, None])
    cum = np.concatenate([tri, same], 0).astype(np.float32)
    strict = same & (t[None, :] < t[:, None])
    masks = np.stack([np.tile(strict, (1, 2)), np.tile(tri, (1, 2)),
                      np.tile(np.eye(w, dtype=bool), (1, 2))]).astype(np.float32)
    rr = np.arange(2 * w)
    hh = rr // B_HEAD
    gmat = (hh[:, None] == hh[None, :]).astype(np.float32)
    return cum, masks, gmat


def _rwkv_kernel(*refs, g_n, hs, chunks, nseq, c, has_state, has_alias):
    it = iter(refs)
    zr, zk, zv = ([[next(it) for _ in range(g_n)] for _ in range(hs)] for _ in range(3))
    zl = [next(it) for _ in range(g_n)]
    (mu_r, mu_k, mu_v, mu_l, w0, a0, kkp, kap, rkp, gnw, gnb,
     w2, a2, g2, cum, msk, gmat) = (next(it) for _ in range(17))
    if has_state:
        sh_r, sh_k, sh_v, sh_l, s0 = (next(it) for _ in range(5))
    if has_alias:
        next(it)
    o_ref, s_out = next(it), next(it)
    shift_outs = [next(it) for _ in range(4)]
    s_ref = next(it)
    carries = None if has_state else [next(it) for _ in range(4)]
    w = RWKV_ROWS
    n = 2 * w
    blk = chunks * w
    nsq = int(math.log2(c)) - 1
    tb = pl.program_id(2)
    lane = lax.broadcasted_iota(jnp.int32, (w, LANES), 1)
    head0 = lane < B_HEAD
    row = lax.broadcasted_iota(jnp.int32, (w, 1), 0)
    first = (row % c) == 0
    gm = gmat[...]
    gm_f = gm.astype(F32)

    @pl.when(tb == 0)
    def _():
        if has_state:
            zero_blk = jnp.zeros((B_HEAD, B_HEAD), F32)
            for j in range(hs):
                for s in range(g_n * nseq):
                    top = jnp.concatenate([s0[0, s, 2 * j], zero_blk], axis=1)
                    bot = jnp.concatenate([zero_blk, s0[0, s, 2 * j + 1]], axis=1)
                    s_ref[j, s] = jnp.concatenate([top, bot], axis=0)
        else:
            s_ref[...] = jnp.zeros_like(s_ref)
            for cr in carries:
                cr[...] = jnp.zeros_like(cr)

    def stack(x):
        return jnp.concatenate([jnp.where(head0, x, 0.0), jnp.where(head0, 0.0, x)], axis=0)

    def gsum(x):
        return _dot_sel(x, gm)

    def shifted(ref, sh, cols, carry, mu, g, ci, r):
        x = ref[pl.ds(r, w), :]
        width = x.shape[1]
        if has_state:
            src = jnp.concatenate(
                [jnp.broadcast_to(sh[g * nseq + s:g * nseq + s + 1, cols], (c, width))
                 for s in range(nseq)], axis=0)
        else:
            rp = pl.multiple_of(jnp.maximum(r - SUBLANES, 0), SUBLANES)
            prev8 = jnp.where(ci == 0, carry, ref[pl.ds(rp, SUBLANES), :])
            src = jnp.broadcast_to(prev8[SUBLANES - 1:SUBLANES, :], (w, width))
        prev = jnp.where(first, src, pltpu.roll(x, 1, axis=0))
        return x + mu * (prev - x)

    def tile(j, g, ci, r):
        hc = slice(j * LANES, (j + 1) * LANES)
        lc = slice(0, B_LORA_W + B_LORA_A + B_LORA_G)
        cr = [None] * 4 if carries is None else [carries[0][j, g], carries[1][j, g], carries[2][j, g],
                                                  carries[3][g]]
        xr = shifted(zr[j][g], sh_r if has_state else None, hc, cr[0], mu_r[:, hc], g, ci, r)
        xk = shifted(zk[j][g], sh_k if has_state else None, hc, cr[1], mu_k[:, hc], g, ci, r)
        xv = shifted(zv[j][g], sh_v if has_state else None, hc, cr[2], mu_v[:, hc], g, ci, r)
        xl = shifted(zl[g], sh_l if has_state else None, lc, cr[3], mu_l[...], g, ci, r)
        wd = xl[:, 0:B_LORA_W]
        ad = xl[:, B_LORA_W:B_LORA_W + B_LORA_A]
        gd = xl[:, B_LORA_W + B_LORA_A:]
        wx = -(w0[:, hc] + _dot(jnp.tanh(wd), w2[:, hc]))
        w_raw = -(jnp.maximum(wx, 0.0) + jnp.log1p(jnp.exp(-jnp.abs(wx)))) - 0.5
        lw = -jnp.exp(w_raw)
        aa = _sigmoid(a0[:, hc] + _dot(ad, a2[:, hc]))
        gb = _dot(_sigmoid(gd), g2[:, hc])
        yield
        kk = xk * kkp[:, hc]
        kk = kk / jnp.maximum(jnp.sqrt(gsum(kk * kk)), 1e-12)
        k2 = xk * (1.0 + (aa - 1.0) * kap[:, hc])
        a = -kk
        b = kk * aa
        yield
        e = _sel_dot(cum[...], lw)
        yield
        lwc = e[0:w]
        lwl = e[w:n]
        dec_in = jnp.exp(lwc)
        dec_ex = jnp.exp(lwc - lw)
        inv = jnp.exp(-lwc)
        rest = jnp.exp(lwl - lwc)
        a_t = a * dec_ex
        r_t = xr * dec_in
        b_t = b * inv
        k_t = k2 * inv
        gram = _dot_nt(jnp.concatenate([a_t, r_t], axis=0),
                       jnp.concatenate([stack(b_t), stack(k_t)], axis=0))
        yield
        m_ab = gram[0:w, 0:n] * msk[0]
        m_ak = gram[0:w, n:2 * n] * msk[0]
        m_rb = gram[w:n, 0:n] * msk[1]
        m_rk = gram[w:n, n:2 * n] * msk[1]
        p = _dot(m_ab, stack(m_ab))
        tinv = msk[2] + m_ab
        yield
        for lev in range(nsq):
            if lev + 1 < nsq:
                both = _dot(p, jnp.concatenate([stack(p), stack(tinv)], axis=1))
                p = both[:, 0:n]
                tinv = tinv + both[:, n:2 * n]
            else:
                tinv = tinv + _dot(p, stack(tinv))
            yield
        p0a, p0r = [], []
        for s in range(nseq):
            rows = slice(s * c, (s + 1) * c)
            pr = _dot_nt(jnp.concatenate([a_t[rows], r_t[rows]], axis=0), s_ref[j, g * nseq + s])
            p0a.append(pr[0:c])
            p0r.append(pr[c:2 * c])
        p0a = p0a[0] if nseq == 1 else jnp.concatenate(p0a, axis=0)
        p0r = p0r[0] if nseq == 1 else jnp.concatenate(p0r, axis=0)
        yield
        vs = stack(xv)
        rhs = p0a + _dot(m_ak, vs)
        yield
        u_w = _dot(tinv, stack(rhs))
        yield
        y = p0r + _dot(m_rb, stack(u_w)) + _dot(m_rk, vs)
        yield
        b_g = b * rest
        k_g = k2 * rest
        dec_l = jnp.exp(lwl)
        for s in range(nseq):
            rows = slice(s * c, (s + 1) * c)
            upd = _dot_tn(jnp.concatenate([u_w[rows], xv[rows]], axis=0),
                          jnp.concatenate([b_g[rows], k_g[rows]], axis=0))
            s_ref[j, g * nseq + s] = s_ref[j, g * nseq + s] * dec_l[s * c:s * c + 1, :] + gm_f * upd
        yield
        mean = gsum(y) * (1.0 / B_HEAD)
        yield
        d = y - mean
        var = gsum(d * d) * (1.0 / B_HEAD)
        yn = d * lax.rsqrt(var + RWKV_GN_EPS) * gnw[:, hc] + gnb[:, hc]
        yield
        bonus = gsum(xr * k2 * rkp[:, hc])
        o_ref[g, pl.ds(r, w), hc] = ((yn + bonus * xv) * gb).astype(BF16)

    def chunk(ci, carry):
        r = pl.multiple_of(ci * w, w)
        _round_robin([tile(j, g, ci, r) for j in range(hs) for g in range(g_n)])
        return carry

    lax.fori_loop(0, chunks, chunk, 0)

    base = blk - w
    for g in range(g_n):
        per_head = [(zr[j][g], shift_outs[0], j) for j in range(hs)]
        per_head += [(zk[j][g], shift_outs[1], j) for j in range(hs)]
        per_head += [(zv[j][g], shift_outs[2], j) for j in range(hs)]
        for ref, out, j in per_head + [(zl[g], shift_outs[3], 0)]:
            cols = slice(j * LANES, j * LANES + ref.shape[1])
            for s in range(nseq):
                last = base + (s + 1) * c - 1
                out[g * nseq + s:g * nseq + s + 1, cols] = ref[last:last + 1, :]
        if carries is not None:
            for j in range(hs):
                carries[0][j, g] = zr[j][g][blk - SUBLANES:blk, :]
                carries[1][j, g] = zk[j][g][blk - SUBLANES:blk, :]
                carries[2][j, g] = zv[j][g][blk - SUBLANES:blk, :]
            carries[3][g] = zl[g][blk - SUBLANES:blk, :]

    @pl.when(tb == pl.num_programs(2) - 1)
    def _():
        for j in range(hs):
            for s in range(g_n * nseq):
                ss = s_ref[j, s]
                s_out[0, s, 2 * j] = ss[0:B_HEAD, 0:B_HEAD]
                s_out[0, s, 2 * j + 1] = ss[B_HEAD:n, B_HEAD:n]


def _rwkv(z, p, state, shift, prev_out, layer, depth, b, t, row_off):
    pairs = B_HEADS // 2
    til = _Tiling(b, t, RWKV_ROWS, row_off, pairs, 2 * B_HEAD * B_HEAD * 4, RWKV_STREAMS)
    cum, msk, gmat = _rwkv_consts(til.c)
    cb = B_OFF // LANES
    wb = B_WIDTH // LANES
    lora_w = B_LORA_W + B_LORA_A + B_LORA_G
    lb_z = (B_OFF + 3 * B_WIDTH) // lora_w
    lb_s = (3 * B_WIDTH) // lora_w

    def vec(x):
        return x.reshape(1, -1)

    def pspec(rows_, col_off=0):
        return til.hspec(rows_, LANES, col_off)

    in_specs, args = [], []
    for col in (cb, cb + wb, cb + 2 * wb):
        for j, g in til.streams():
            in_specs.append(til.zspec(j, g, LANES, col))
            args.append(z)
    for g in range(til.g):
        in_specs.append(til.zspec(0, g, lora_w, lb_z, per_head=0))
        args.append(z)
    in_specs += [
        pspec(1), pspec(1, wb), pspec(1, 2 * wb),
        pl.BlockSpec((1, lora_w), lambda i, h, tb: (0, lb_s)),
        pspec(1), pspec(1), pspec(1), pspec(1), pspec(1), pspec(1), pspec(1),
        pspec(B_LORA_W), pspec(B_LORA_A), pspec(B_LORA_G),
        _const_spec(cum.shape), _const_spec(msk.shape), _const_spec(gmat.shape),
    ]
    mu = vec(p['rwkv_mu'])
    args += [mu, mu, mu, mu,
             vec(p['rwkv_w0']), vec(p['rwkv_a0']), vec(p['rwkv_kk']), vec(p['rwkv_ka']),
             vec(p['rwkv_rk']), vec(p['rwkv_gn_w']), vec(p['rwkv_gn_b']),
             p['rwkv_w2'].astype(BF16), p['rwkv_a2'].astype(BF16), p['rwkv_g2'].astype(BF16),
             jnp.asarray(cum, BF16), jnp.asarray(msk, F32), jnp.asarray(gmat, BF16)]
    ns = til.g * til.nseq
    if state is not None:
        hw = til.hs * LANES
        wbh = wb // til.hs
        in_specs += [
            pl.BlockSpec((ns, hw), lambda i, h, tb: (i, h)),
            pl.BlockSpec((ns, hw), lambda i, h, tb: (i, wbh + h)),
            pl.BlockSpec((ns, hw), lambda i, h, tb: (i, 2 * wbh + h)),
            pl.BlockSpec((ns, lora_w), lambda i, h, tb: (i, lb_s)),
        ]
        args += [shift, shift, shift, shift]
    has_state, has_alias, aliases, s_spec, s_shape = _state_io(
        til, state, prev_out, layer, depth, b, B_HEADS, 2, B_HEAD, B_HEAD, in_specs, args)
    scratch = [pltpu.VMEM((til.hs, til.g * til.nseq, 2 * B_HEAD, 2 * B_HEAD), F32)]
    if not has_state:
        scratch += [pltpu.VMEM((til.hs, til.g, SUBLANES, LANES), F32) for _ in range(3)]
        scratch += [pltpu.VMEM((til.g, SUBLANES, lora_w), F32)]
    o, s, sh_r, sh_k, sh_v, sh_l = pl.pallas_call(
        functools.partial(_rwkv_kernel, g_n=til.g, hs=til.hs, chunks=til.chunks, nseq=til.nseq, c=til.c,
                          has_state=has_state, has_alias=has_alias),
        grid=(til.steps, til.head_steps, til.nt),
        in_specs=in_specs,
        out_specs=[til.ospec(LANES), s_spec,
                   pl.BlockSpec((ns, til.hs * LANES), lambda i, h, tb: (i, h)),
                   pl.BlockSpec((ns, til.hs * LANES), lambda i, h, tb: (i, h)),
                   pl.BlockSpec((ns, til.hs * LANES), lambda i, h, tb: (i, h)),
                   pl.BlockSpec((ns, lora_w), lambda i, h, tb: (i, 0))],
        out_shape=[til.oshape(B_WIDTH), s_shape] + [
            jax.ShapeDtypeStruct((b, wd), F32) for wd in (B_WIDTH, B_WIDTH, B_WIDTH, lora_w)],
        scratch_shapes=scratch,
        input_output_aliases=aliases,
        compiler_params=_cparams(("parallel", "parallel", "arbitrary")),
        name="rwkv7",
    )(*args)
    return o.reshape(b * t, B_WIDTH), s, jnp.concatenate([sh_r, sh_k, sh_v, sh_l], axis=1)


def _merge_kernel(oa, ob, oc, wa, wb, wc, ga, gb, gc, *rest):
    o_ref = rest[-1]
    acc = _sigmoid(ga[...]) * jnp.dot(oa[...], wa[...].astype(BF16), preferred_element_type=F32)
    acc = acc + _sigmoid(gb[...]) * jnp.dot(ob[...], wb[...].astype(BF16), preferred_element_type=F32)
    acc = acc + _sigmoid(gc[...]) * jnp.dot(oc[...], wc[...].astype(BF16), preferred_element_type=F32)
    o_ref[...] = acc.astype(BF16)


def _merge(z, oa, ob, oc, wa, wb, wc, layer, row_off, prev_out):
    m = oa.shape[0]
    tm = _pick(m, (1024, 512, 256, 128))
    tn = 256
    gblk = G_OFF // tn
    nb = D_MODEL // tn
    assert G_OFF % tn == 0 and row_off % tm == 0
    off = row_off // tm

    def ospec():
        return pl.BlockSpec((tm, oa.shape[1]), lambda i, j: (i, 0))

    def wspec():
        return pl.BlockSpec((None, wa.shape[1], tn), lambda i, j: (layer, 0, j))

    def gspec(br):
        return pl.BlockSpec((tm, tn), lambda i, j: (i + off, gblk + br * nb + j))

    in_specs = [ospec(), ospec(), ospec(), wspec(), wspec(), wspec(), gspec(0), gspec(1), gspec(2)]
    args = [oa, ob, oc, wa, wb, wc, z, z, z]
    aliases = {}
    if prev_out is not None:
        aliases = {len(args): 0}
        in_specs.append(pl.BlockSpec(memory_space=pl.ANY))
        args.append(prev_out)
    return pl.pallas_call(
        _merge_kernel,
        grid=(m // tm, nb),
        in_specs=in_specs,
        out_specs=pl.BlockSpec((tm, tn), lambda i, j: (i + off, j)),
        out_shape=jax.ShapeDtypeStruct((z.shape[0], D_MODEL), BF16),
        input_output_aliases=aliases,
        compiler_params=_cparams(("parallel", "arbitrary")),
        name="merge",
    )(*args)


def _proj_res_kernel(m_ref, w_ref, g_ref, x_ref, o_ref):
    y = jnp.dot(m_ref[...], w_ref[...], preferred_element_type=F32)
    yn = y * lax.rsqrt(jnp.mean(y * y, axis=-1, keepdims=True) + NORM_EPS) * g_ref[...]
    o_ref[...] = x_ref[...] + yn


def _proj_res(mrg, w, g, x, layer):
    m = x.shape[0]
    tm = _pick(m, (512, 256, 128))
    return pl.pallas_call(
        _proj_res_kernel,
        grid=(m // tm,),
        in_specs=[
            pl.BlockSpec((tm, D_MODEL), lambda i: (i, 0)),
            pl.BlockSpec((None, D_MODEL, D_MODEL), lambda i: (layer, 0, 0)),
            pl.BlockSpec((1, D_MODEL), lambda i: (0, 0)),
            pl.BlockSpec((tm, D_MODEL), lambda i: (i, 0)),
        ],
        out_specs=pl.BlockSpec((tm, D_MODEL), lambda i: (i, 0)),
        out_shape=jax.ShapeDtypeStruct((m, D_MODEL), F32),
        compiler_params=_cparams(("parallel",)),
        name="proj_res",
    )(mrg, w, g.reshape(1, D_MODEL), x)


def _gelu(x):
    return 0.5 * x * (1.0 + jnp.tanh(math.sqrt(2.0 / math.pi) * (x + 0.044715 * (x * x * x))))


def _up_act_kernel(*refs, tm, t, blocks_per_seq, has_state, has_alias):
    it = iter(refs)
    x_ref, g_ref, wa, wb, cw, cb = (next(it) for _ in range(6))
    st = next(it) if has_state else None
    if has_alias:
        next(it)
    o_ref, nc_ref, xn_ref = next(it), next(it), next(it)
    tail = None if has_state else next(it)
    i = pl.program_id(0)
    j = pl.program_id(1)
    tn = wa.shape[1]
    sub = UP_SUB

    @pl.when(j == 0)
    def _():
        rows = min(tm, LANES)

        def body(r, carry):
            r0 = pl.multiple_of(r * rows, rows)
            x = x_ref[pl.ds(r0, rows), :]
            ms = jnp.mean(x * x, axis=-1, keepdims=True)
            xn_ref[pl.ds(r0, rows), :] = (x * lax.rsqrt(ms + NORM_EPS) * g_ref[...]).astype(BF16)
            return carry
        lax.fori_loop(0, tm // rows, body, 0)

    if has_state:
        ns = tm // t
        tt = lax.broadcasted_iota(jnp.int32, (ns, t, sub), 1)
    else:
        seq_start = (i % blocks_per_seq) == 0
        rr = lax.broadcasted_iota(jnp.int32, (tm, sub), 0)
    for c in range(tn // sub):
        cols = slice(c * sub, (c + 1) * sub)
        ua = jnp.dot(xn_ref[...], wa[:, cols].astype(BF16), preferred_element_type=F32)
        ub = jnp.dot(xn_ref[...], wb[:, cols].astype(BF16), preferred_element_type=F32)
        if has_state:
            x3 = ua.reshape(ns, t, sub)
            s_old = st[:, 0:1, cols]
            s_new = st[:, 1:2, cols]
            prev1 = jnp.where(tt >= 1, pltpu.roll(x3, 1, axis=1), s_new)
            prev2 = jnp.where(tt >= 2, pltpu.roll(x3, 2, axis=1), jnp.where(tt == 1, s_new, s_old))
            prev1 = prev1.reshape(tm, sub)
            prev2 = prev2.reshape(tm, sub)
            nc_ref[:, :, cols] = x3[:, t - (CONV_W - 1):, :]
        else:
            h = jnp.where(seq_start, 0.0, tail[j, :, cols])
            h1 = h[SUBLANES - 1:SUBLANES, :]
            h2 = h[SUBLANES - 2:SUBLANES - 1, :]
            prev1 = jnp.where(rr == 0, h1, pltpu.roll(ua, 1, axis=0))
            prev2 = jnp.where(rr == 0, h2, jnp.where(rr == 1, h1, pltpu.roll(ua, 2, axis=0)))
            tail[j, :, cols] = ua[tm - SUBLANES:, :]
            nc_ref[0, :, cols] = ua[tm - (CONV_W - 1):, :]
        conv = cb[:, cols] + cw[0:1, cols] * prev2 + cw[1:2, cols] * prev1 + cw[2:3, cols] * ua
        o_ref[:, cols] = (_gelu(conv) * ub).astype(BF16)


def _up_act(x1, g, w_up, cw, cb, state, prev_out, layer, b, t, row_off):
    m = b * t
    total = x1.shape[0]
    has_state = state is not None
    has_alias = prev_out is not None
    tn = UP_TN
    nj = D_FF // tn
    if has_state:
        tm = _pick(m, (1024, 512, 256, 128))
        assert tm % t == 0 and t == SUBLANES
        blocks_per_seq = 1
        nc_spec = pl.BlockSpec((tm // t, CONV_W - 1, tn), lambda i, j: (i, 0, j))
    else:
        tm = _pick(t, (1024, 512, 256, 128))
        blocks_per_seq = t // tm
        nc_spec = pl.BlockSpec((1, CONV_W - 1, tn), lambda i, j: (i, 0, j))
    nc_rows = b * blocks_per_seq
    off = row_off // tm
    assert row_off % tm == 0
    in_specs = [
        pl.BlockSpec((tm, D_MODEL), lambda i, j: (i + off, 0)),
        pl.BlockSpec((1, D_MODEL), lambda i, j: (0, 0)),
        pl.BlockSpec((None, D_MODEL, tn), lambda i, j: (layer, 0, j)),
        pl.BlockSpec((None, D_MODEL, tn), lambda i, j: (layer, 0, nj + j)),
        pl.BlockSpec((CONV_W, tn), lambda i, j: (0, j)),
        pl.BlockSpec((1, tn), lambda i, j: (0, j)),
    ]
    args = [x1, g.reshape(1, D_MODEL), w_up, w_up, cw, cb.reshape(1, D_FF)]
    if has_state:
        in_specs.append(pl.BlockSpec((tm // t, CONV_W - 1, tn), lambda i, j: (i, 0, j)))
        args.append(state)
    aliases = {}
    if has_alias:
        aliases = {len(args): 0}
        in_specs.append(pl.BlockSpec(memory_space=pl.ANY))
        args.append(prev_out)
    scratch = [pltpu.VMEM((tm, D_MODEL), BF16)]
    if not has_state:
        scratch.append(pltpu.VMEM((nj, SUBLANES, tn), F32))
    act, nc = pl.pallas_call(
        functools.partial(_up_act_kernel, tm=tm, t=t, blocks_per_seq=blocks_per_seq,
                          has_state=has_state, has_alias=has_alias),
        grid=(m // tm, nj),
        in_specs=in_specs,
        out_specs=[pl.BlockSpec((tm, tn), lambda i, j: (i + off, j)), nc_spec],
        out_shape=[jax.ShapeDtypeStruct((total, D_FF), BF16),
                   jax.ShapeDtypeStruct((nc_rows, CONV_W - 1, D_FF), F32)],
        scratch_shapes=scratch,
        input_output_aliases=aliases,
        compiler_params=_cparams(("arbitrary", "arbitrary")),
        name="up_act",
    )(*args)
    return act, nc[blocks_per_seq - 1::blocks_per_seq]


def _down_res_kernel(a_ref, w_ref, g_ref, x_ref, o_ref, acc):
    kstep = pl.program_id(1)

    @pl.when(kstep == 0)
    def _():
        acc[...] = jnp.zeros_like(acc)

    acc[...] += jnp.dot(a_ref[...], w_ref[...], preferred_element_type=F32)

    @pl.when(kstep == pl.num_programs(1) - 1)
    def _():
        y = acc[...]
        yn = y * lax.rsqrt(jnp.mean(y * y, axis=-1, keepdims=True) + NORM_EPS) * g_ref[...]
        o_ref[...] = x_ref[...] + yn


def _down_res(act, wd, g, x1, layer, row_off, rows):
    tm = _pick(rows, (512, 256, 128))
    tk = DOWN_TK
    off = row_off // tm
    assert row_off % tm == 0
    return pl.pallas_call(
        _down_res_kernel,
        grid=(rows // tm, D_FF // tk),
        in_specs=[
            pl.BlockSpec((tm, tk), lambda i, k: (i + off, k)),
            pl.BlockSpec((None, tk, D_MODEL), lambda i, k: (layer, k, 0)),
            pl.BlockSpec((1, D_MODEL), lambda i, k: (0, 0)),
            pl.BlockSpec((tm, D_MODEL), lambda i, k: (i + off, 0)),
        ],
        out_specs=pl.BlockSpec((tm, D_MODEL), lambda i, k: (i, 0)),
        out_shape=jax.ShapeDtypeStruct((rows, D_MODEL), F32),
        scratch_shapes=[pltpu.VMEM((tm, D_MODEL), F32)],
        compiler_params=_cparams(("parallel", "arbitrary")),
        name="down_res",
    )(act, wd, g.reshape(1, D_MODEL), x1)


def _layer(x, groups, lb, p, layer, depth, prev, split_out):
    z = _rms_matmul(x, p['pre_mix_g'], p['w_in'], layer, tn=1280)
    states = []
    mrg = None
    row = 0
    for gi, (b, t, t0, st) in enumerate(groups):
        s_a, s_b, s_sh, s_c, _ = st if st is not None else (None,) * 5
        pv = prev[gi] if prev is not None else (None,) * 3
        o_a, n_a = _hgrn(z, lb, p['a_norm_g'], s_a, pv[0], layer, depth, b, t, row)
        o_b, n_b, n_sh = _rwkv(z, p, s_b, s_sh, pv[1], layer, depth, b, t, row)
        o_c, n_c = _retention(z, s_c, pv[2], layer, depth, b, t, t0, row)
        mrg = _merge(z, o_a, o_b, o_c, p['w_br_a'], p['w_br_b'], p['w_br_c'], layer, row, mrg)
        states.append([n_a, n_b, n_sh, n_c])
        row += b * t
    x1 = _proj_res(mrg, p['w_out'], p['post_mix_g'], x, layer)
    act = None
    row = 0
    for gi, (b, t, t0, st) in enumerate(groups):
        s_cv = st[4] if st is not None else None
        act, n_cv = _up_act(x1, p['pre_ffn_g'], p['w_up'], p['conv_w'], p['conv_b'], s_cv, act,
                            layer, b, t, row)
        states[gi].append(n_cv)
        row += b * t
    if split_out:
        x2, row = [], 0
        for (b, t, _, _) in groups:
            x2.append(_down_res(act, p['w_down'], p['post_ffn_g'], x1, layer, row, b * t))
            row += b * t
    else:
        x2 = _down_res(act, p['w_down'], p['post_ffn_g'], x1, layer, 0, x1.shape[0])
    return x2, states


def kernel(x_prompt, x_sample, state_hgrn, state_rwkv, state_rwkv_shift, state_ret, state_conv,
           lb_logits, pre_mix_g, w_in, a_norm_g, rwkv_mu, rwkv_w0, rwkv_w2, rwkv_a0, rwkv_a2,
           rwkv_g2, rwkv_kk, rwkv_ka, rwkv_rk, rwkv_gn_w, rwkv_gn_b, w_br_a, w_br_b, w_br_c,
           w_out, post_mix_g, pre_ffn_g, w_up, conv_w, conv_b, w_down, post_ffn_g):
    depth = w_in.shape[0]
    bp, tp, _ = x_prompt.shape
    bs, ts, _ = x_sample.shape
    past_len = 16384
    lb_soft = jax.nn.softmax(lb_logits.astype(F32), axis=0)
    lbs = jnp.cumsum(lb_soft, axis=0) - lb_soft[0]
    big = {'w_in': w_in, 'w_br_a': w_br_a, 'w_br_b': w_br_b,
           'w_br_c': w_br_c, 'w_out': w_out.astype(BF16), 'w_up': w_up,
           'w_down': w_down.astype(BF16)}
    x = jnp.concatenate([x_prompt.reshape(bp * tp, D_MODEL), x_sample.reshape(bs * ts, D_MODEL)], axis=0)
    small = [[[], []], [[], []]]
    prev = None
    for l in range(depth):
        p = dict(big)
        p.update({
            'pre_mix_g': pre_mix_g[l], 'a_norm_g': a_norm_g[l],
            'rwkv_mu': rwkv_mu[l], 'rwkv_w0': rwkv_w0[l], 'rwkv_w2': rwkv_w2[l],
            'rwkv_a0': rwkv_a0[l], 'rwkv_a2': rwkv_a2[l], 'rwkv_g2': rwkv_g2[l],
            'rwkv_kk': rwkv_kk[l], 'rwkv_ka': rwkv_ka[l], 'rwkv_rk': rwkv_rk[l],
            'rwkv_gn_w': rwkv_gn_w[l], 'rwkv_gn_b': rwkv_gn_b[l],
            'post_mix_g': post_mix_g[l], 'pre_ffn_g': pre_ffn_g[l],
            'conv_w': conv_w[l], 'conv_b': conv_b[l], 'post_ffn_g': post_ffn_g[l],
        })
        groups = [
            (bp, tp, 0, None),
            (bs, ts, past_len, (state_hgrn, state_rwkv, state_rwkv_shift[l], state_ret, state_conv[l])),
        ]
        x, states = _layer(x, groups, lbs[l], p, l, depth, prev, l == depth - 1)
        prev = [(st[0], st[1], st[3]) for st in states]
        for gi, st in enumerate(states):
            small[gi][0].append(st[2])
            small[gi][1].append(st[4])
    y_p = x[0].reshape(bp, tp, D_MODEL)
    y_s = x[1].reshape(bs, ts, D_MODEL)
    outs = []
    for gi in range(2):
        outs += [prev[gi][0], prev[gi][1], jnp.stack(small[gi][0]), prev[gi][2], jnp.stack(small[gi][1])]
    return (y_p, y_s, *outs)
```

```python
import functools
import itertools
import math

import jax
import jax.numpy as jnp
import numpy as np
from jax import lax
from jax.experimental import pallas as pl
from jax.experimental.pallas import tpu as pltpu

F32 = jnp.float32
BF16 = jnp.bfloat16

D_MODEL = 2048
A_HEADS, A_DK, A_DV = 8, 128, 128
A_QK = A_HEADS * A_DK
A_WIDTH = A_HEADS * A_DV
F_TINY = 1e-30
B_HEAD = 64
B_WIDTH = 1024
B_HEADS = B_WIDTH // B_HEAD
B_LORA_W, B_LORA_A, B_LORA_G = 64, 64, 128
RWKV_GN_EPS = 64e-5
C_HEADS, C_DK, C_DV = 4, 128, 256
C_QK = C_HEADS * C_DK
C_WIDTH = C_HEADS * C_DV
ROPE_BASE = 10000.0
A_COLS = 2 * A_QK + 2 * A_WIDTH
B_COLS = 3 * B_WIDTH + B_LORA_W + B_LORA_A + B_LORA_G
C_COLS = 2 * C_QK + 2 * C_WIDTH
N_BRANCH = 3
P_COLS = A_COLS + B_COLS + C_COLS + N_BRANCH * D_MODEL
B_OFF = A_COLS
C_OFF = A_COLS + B_COLS
G_OFF = A_COLS + B_COLS + C_COLS
D_FF = 5632
CONV_W = 3
NORM_EPS = 1e-6

LANES = 128
SUBLANES = 8
MXU_COLS = 256
MIX_ROWS = 128
RWKV_ROWS = 64
MIX_TIME_BLOCK = 512
MIX_STREAMS = 4
HGRN_STREAMS = 8
HGRN_MATMUL_LEVEL_ROWS = 8
RET_STREAMS = 8
RWKV_STREAMS = 16
STATE_WINDOW_BYTES = 4 * 1024 * 1024
UP_TN = 512
UP_SUB = 256
DOWN_TK = 2816
VMEM_LIMIT = 56 * 1024 * 1024


def _cparams(sem):
    return pltpu.CompilerParams(dimension_semantics=sem, vmem_limit_bytes=VMEM_LIMIT)


def _dot(a, b):
    return jnp.dot(a.astype(BF16), b.astype(BF16), preferred_element_type=F32)


def _dot_nt(a, b):
    return lax.dot_general(a.astype(BF16), b.astype(BF16), (((1,), (1,)), ((), ())),
                           preferred_element_type=F32)


def _dot_tn(a, b):
    return lax.dot_general(a.astype(BF16), b.astype(BF16), (((0,), (0,)), ((), ())),
                           preferred_element_type=F32)


def _split(x):
    hi = x.astype(BF16)
    lo = (x - hi.astype(F32)).astype(BF16)
    return hi, lo


def _sel_dot(m2, x):
    hi, lo = _split(x)
    return jnp.dot(m2, jnp.concatenate([hi, lo], axis=0), preferred_element_type=F32)


def _dot_sel(x, m2):
    hi, lo = _split(x)
    return jnp.dot(jnp.concatenate([hi, lo], axis=1), m2, preferred_element_type=F32)


def _sigmoid(x):
    return 0.5 * jnp.tanh(0.5 * x) + 0.5


def _silu(x):
    h = 0.5 * x
    return h * jnp.tanh(h) + h


def _round_robin(gens):
    for _ in itertools.zip_longest(*gens):
        pass


def _pick(n, cands):
    for c in cands:
        if n % c == 0:
            return c
    raise ValueError(f"no tile in {cands} divides {n}")


def _rms_matmul_kernel(x_ref, g_ref, w_ref, o_ref, xn_ref, *, tm, sub):
    @pl.when(pl.program_id(1) == 0)
    def _():
        def body(i, carry):
            r = pl.multiple_of(i * sub, sub)
            x = x_ref[pl.ds(r, sub), :]
            ms = jnp.mean(x * x, axis=-1, keepdims=True)
            xn_ref[pl.ds(r, sub), :] = (x * lax.rsqrt(ms + NORM_EPS) * g_ref[...]).astype(BF16)
            return carry
        lax.fori_loop(0, tm // sub, body, 0)

    tn = w_ref.shape[1]
    for c in range(tn // MXU_COLS):
        cols = slice(c * MXU_COLS, (c + 1) * MXU_COLS)
        o_ref[:, cols] = jnp.dot(xn_ref[...], w_ref[:, cols].astype(BF16), preferred_element_type=F32)


def _rms_matmul(x, g, w, layer, tn):
    m, k = x.shape
    n = w.shape[2]
    tm = _pick(m, (1024, 512, 256, 128))
    sub = min(tm, 128)
    assert tn % MXU_COLS == 0
    return pl.pallas_call(
        functools.partial(_rms_matmul_kernel, tm=tm, sub=sub),
        grid=(m // tm, n // tn),
        in_specs=[
            pl.BlockSpec((tm, k), lambda i, j: (i, 0), pipeline_mode=pl.Buffered(1)),
            pl.BlockSpec((1, k), lambda i, j: (0, 0)),
            pl.BlockSpec((None, k, tn), lambda i, j: (layer, 0, j)),
        ],
        out_specs=pl.BlockSpec((tm, tn), lambda i, j: (i, j)),
        out_shape=jax.ShapeDtypeStruct((m, n), F32),
        scratch_shapes=[pltpu.VMEM((tm, k), BF16)],
        compiler_params=_cparams(("parallel", "arbitrary")),
        name="rms_matmul",
    )(x, g.reshape(1, k), w)


class _Tiling:
    def __init__(self, b, t, rows, row_off, heads, state_bytes, streams_total):
        if t >= rows:
            self.nseq, self.c = 1, rows
            self.blk = _pick(t, (MIX_TIME_BLOCK, rows))
            self.nt = t // self.blk
            n_streams = b
        else:
            assert rows % t == 0 and b % (rows // t) == 0
            self.nseq, self.c = rows // t, t
            self.blk, self.nt = rows, 1
            n_streams = b // self.nseq
        self.g = _pick(n_streams, (MIX_STREAMS, 2, 1))
        self.hs = _pick(heads, (max(streams_total // self.g, 1), 2, 1))
        while self.g * self.nseq * self.hs * state_bytes > STATE_WINDOW_BYTES and self.hs > 1:
            self.hs //= 2
        while self.g * self.nseq * self.hs * state_bytes > STATE_WINDOW_BYTES and self.g > 1:
            self.g //= 2
        self.head_steps = heads // self.hs
        self.steps = n_streams // self.g
        self.n_streams = n_streams
        self.chunks = self.blk // rows
        assert row_off % self.blk == 0
        self.off = row_off // self.blk

    def streams(self):
        return [(j, g) for j in range(self.hs) for g in range(self.g)]

    def zspec(self, j, stream, width, col_blk, per_head=1):
        g, nt, off, hs = self.g, self.nt, self.off, self.hs
        return pl.BlockSpec(
            (self.blk, width),
            lambda i, h, tb: (off + (i * g + stream) * nt + tb, col_blk + per_head * (h * hs + j)))

    def hspec(self, rows, width, blk_off=0):
        assert blk_off % self.hs == 0
        off = blk_off // self.hs
        return pl.BlockSpec((rows, self.hs * width), lambda i, h, tb: (0, off + h))

    def ospec(self, width):
        return pl.BlockSpec((self.g, self.blk, self.hs * width), lambda i, h, tb: (i, tb, h))

    def oshape(self, width):
        return jax.ShapeDtypeStruct((self.n_streams, self.nt * self.blk, width), BF16)

    def sspec(self, layer, heads_per_step, d0, d1):
        return pl.BlockSpec((1, self.g * self.nseq, self.hs * heads_per_step, d0, d1),
                            lambda i, h, tb: (layer, i, h, 0, 0))


def _const_spec(shape):
    nd = len(shape)
    return pl.BlockSpec(shape, lambda i, h, tb: (0,) * nd)


def _state_io(til, state, prev_out, layer, depth, b, heads, heads_per_step, d0, d1,
              in_specs, args):
    has_state = state is not None
    if has_state:
        in_specs.append(til.sspec(layer, heads_per_step, d0, d1))
        args.append(state)
    aliases = {}
    if prev_out is not None:
        aliases = {len(args): 1}
        in_specs.append(pl.BlockSpec(memory_space=pl.ANY))
        args.append(prev_out)
    out_spec = til.sspec(layer, heads_per_step, d0, d1)
    out_shape = jax.ShapeDtypeStruct((depth, b, heads, d0, d1), F32)
    return has_state, prev_out is not None, aliases, out_spec, out_shape


@functools.lru_cache(maxsize=None)
def _hgrn_consts(c):
    n = MIX_ROWS
    nlev = int(math.log2(c))
    t = np.arange(n)
    u = np.arange(n)[None, :]
    blk = t // c
    same = blk[:, None] == blk[None, :]
    mats = [same & (u <= t[:, None]), same]
    masks = [np.eye(n, dtype=bool)]
    for lev in range(nlev):
        h = 1 << lev
        base = (t // (2 * h)) * (2 * h)
        mid = base + h
        upper = t >= mid
        e_up = (u >= mid[:, None]) & (u <= t[:, None])
        e_lo = (u >= t[:, None] + 1) & (u <= mid[:, None] - 1)
        if 2 * h <= HGRN_MATMUL_LEVEL_ROWS:
            mats.append(np.where(upper[:, None], e_up, e_lo))
        masks.append((base[:, None] == base[None, :]) & upper[:, None] & (~upper)[None, :])
    sel = np.tile(np.concatenate(mats, 0).astype(np.float32), (1, 2))
    msk = np.stack(masks).astype(np.float32)
    return sel, msk, nlev


def _hgrn_kernel(*refs, g_n, hs, chunks, nseq, c, nlev, has_state, has_alias):
    it = iter(refs)
    zq, zf, zi, zg = ([[next(it) for _ in range(g_n)] for _ in range(hs)] for _ in range(4))
    lb, gn, sel, msk = (next(it) for _ in range(4))
    s0 = next(it) if has_state else None
    if has_alias:
        next(it)
    o_ref, s_out, s_ref = next(it), next(it), next(it)
    n = MIX_ROWS
    tb = pl.program_id(2)

    @pl.when(tb == 0)
    def _():
        if has_state:
            for j in range(hs):
                s_ref[j] = s0[0, :, j]
        else:
            s_ref[...] = jnp.zeros_like(s_ref)

    row_id = lax.broadcasted_iota(jnp.int32, (n, 1), 0)

    def tile(j, g, r):
        hcols = slice(j * LANES, (j + 1) * LANES)
        lbv = lb[:, hcols]
        xq = zq[j][g][pl.ds(r, n), :]
        fa = zf[j][g][pl.ds(r, n), :]
        v = zi[j][g][pl.ds(r, n), :]
        xg = zg[j][g][pl.ds(r, n), :]
        q = _silu(xq)
        th = jnp.tanh(0.5 * fa)
        f_gate = lbv + (1.0 - lbv) * (0.5 + 0.5 * th)
        gl = jnp.log(jnp.maximum(f_gate, F_TINY))
        k = (1.0 - lbv) * (0.5 - 0.5 * th)
        e = _sel_dot(sel[...], gl)
        yield
        b = e[0:n]
        bl = e[n:2 * n]
        scores = msk[0] * _dot_nt(q, k)
        for lev in range(nlev):
            size = 2 << lev
            if size <= HGRN_MATMUL_LEVEL_ROWS:
                ex = e[(lev + 2) * n:(lev + 3) * n]
            else:
                ref = jnp.concatenate(
                    [jnp.broadcast_to(b[s0 + size // 2 - 1:s0 + size // 2, :], (size, A_DK))
                     for s0 in range(0, n, size)], axis=0)
                upper = (row_id % size) >= size // 2
                ex = jnp.where(upper, b - ref, ref - b)
            x = jnp.exp(ex)
            scores = scores + msk[lev + 1] * _dot_nt(q * x, k * x)
        yield
        o = _dot(scores, v)
        yield
        qe = q * jnp.exp(b)
        kt = k * jnp.exp(bl - b)
        dt = jnp.exp(bl).T
        outs = []
        for s in range(nseq):
            rows = slice(s * c, (s + 1) * c)
            ss = s_ref[j, g * nseq + s]
            outs.append(_dot(qe[rows], ss))
            dcol = jnp.broadcast_to(dt[:, s * c:s * c + 1], (A_DK, A_DV))
            s_ref[j, g * nseq + s] = ss * dcol + _dot_tn(kt[rows], v[rows])
        o = o + (outs[0] if nseq == 1 else jnp.concatenate(outs, axis=0))
        on = o * lax.rsqrt(jnp.mean(o * o, axis=-1, keepdims=True) + NORM_EPS) * gn[:, hcols]
        o_ref[g, pl.ds(r, n), hcols] = (on * _silu(xg)).astype(BF16)

    def chunk(ci, carry):
        r = pl.multiple_of(ci * n, n)
        _round_robin([tile(j, g, r) for j in range(hs) for g in range(g_n)])
        return carry

    lax.fori_loop(0, chunks, chunk, 0)

    @pl.when(tb == pl.num_programs(2) - 1)
    def _():
        for j in range(hs):
            s_out[0, :, j] = s_ref[j]


def _hgrn(z, lb, gn, state, prev_out, layer, depth, b, t, row_off):
    til = _Tiling(b, t, MIX_ROWS, row_off, A_HEADS, A_DK * A_DV * 4, HGRN_STREAMS)
    sel, msk, nlev = _hgrn_consts(til.c)
    qk_blocks = A_QK // LANES
    in_specs, args = [], []
    for col in range(4):
        for j, g in til.streams():
            in_specs.append(til.zspec(j, g, LANES, col * qk_blocks))
            args.append(z)
    in_specs += [
        til.hspec(1, LANES), til.hspec(1, LANES),
        _const_spec(sel.shape), _const_spec(msk.shape),
    ]
    args += [lb.reshape(1, A_QK), gn.reshape(1, A_WIDTH), jnp.asarray(sel, BF16), jnp.asarray(msk, F32)]
    has_state, has_alias, aliases, s_spec, s_shape = _state_io(
        til, state, prev_out, layer, depth, b, A_HEADS, 1, A_DK, A_DV, in_specs, args)
    o, s = pl.pallas_call(
        functools.partial(_hgrn_kernel, g_n=til.g, hs=til.hs, chunks=til.chunks, nseq=til.nseq, c=til.c,
                          nlev=nlev, has_state=has_state, has_alias=has_alias),
        grid=(til.steps, til.head_steps, til.nt),
        in_specs=in_specs,
        out_specs=[til.ospec(LANES), s_spec],
        out_shape=[til.oshape(A_WIDTH), s_shape],
        scratch_shapes=[pltpu.VMEM((til.hs, til.g * til.nseq, A_DK, A_DV), F32)],
        input_output_aliases=aliases,
        compiler_params=_cparams(("parallel", "parallel", "arbitrary")),
        name="hgrn2",
    )(*args)
    return o.reshape(b * t, A_WIDTH), s


def _ret_tables(c):
    n = MIX_ROWS
    log_g = jnp.log1p(-jnp.exp2(-5.0 - jnp.arange(C_HEADS, dtype=F32)))
    t = np.arange(n)
    tt = (t % c).astype(np.float32)
    blk = t // c
    rel = tt[:, None] - tt[None, :]
    same = (blk[:, None] == blk[None, :]) & (rel >= 0)
    dmat = jnp.where(same[None], jnp.exp(log_g[:, None, None] * np.maximum(rel, 0.0)[None]), 0.0)
    inner = jnp.exp(log_g[:, None] * (tt[None, :] + 1.0))
    tail = jnp.exp(log_g[:, None] * (c - 1.0 - tt[None, :]))
    total = jnp.exp(log_g * c)
    shape = (C_HEADS, n, n)
    tab = jnp.stack([dmat, jnp.broadcast_to(inner[:, :, None], shape),
                     jnp.broadcast_to(tail[:, :, None], shape)], axis=1)
    tot = jnp.broadcast_to(total[:, None, None], (C_HEADS, 1, C_DV))
    return tab.astype(F32), tot.astype(F32)


def _rope_tables(t0, t, reps):
    half = C_DK // 2
    inv = ROPE_BASE ** (-jnp.arange(half, dtype=F32) / half)
    pos = t0 + jnp.arange(t, dtype=F32)
    ang = pos[:, None] * inv[None, :]
    cos, sin = jnp.cos(ang), jnp.sin(ang)
    cosf = jnp.concatenate([cos, cos], axis=-1)
    sinf = jnp.concatenate([-sin, sin], axis=-1)
    return jnp.tile(cosf, (reps, 1)), jnp.tile(sinf, (reps, 1))


def _ret_kernel(*refs, g_n, hs, chunks, nseq, c, has_state, has_alias):
    it = iter(refs)
    zq, zk, zv, zg = ([[next(it) for _ in range(g_n)] for _ in range(hs)] for _ in range(4))
    cos, sin, tab, tot = (next(it) for _ in range(4))
    s0 = next(it) if has_state else None
    if has_alias:
        next(it)
    o_ref, s_out, s_ref = next(it), next(it), next(it)
    n = MIX_ROWS
    half = C_DK // 2
    tb = pl.program_id(2)

    @pl.when(tb == 0)
    def _():
        if has_state:
            for j in range(hs):
                s_ref[j] = s0[0, :, j]
        else:
            s_ref[...] = jnp.zeros_like(s_ref)

    def tile(j, g, r, cs, sn):
        xq = zq[j][g][pl.ds(r, n), :]
        xk = zk[j][g][pl.ds(r, n), :]
        v = zv[j][g][pl.ds(r, n), :]
        xg = zg[j][g][pl.ds(r, n), :]
        q = xq * cs + pltpu.roll(xq, half, axis=1) * sn
        k = (xk * cs + pltpu.roll(xk, half, axis=1) * sn) * (C_DK ** -0.5)
        scores = _dot_nt(q, k) * tab[j, 0]
        yield
        o = _dot(scores, v)
        yield
        qi = q * tab[j, 1]
        ktl = k * tab[j, 2]
        outs = []
        for s in range(nseq):
            rows = slice(s * c, (s + 1) * c)
            ss = s_ref[j, g * nseq + s]
            outs.append(_dot(qi[rows], ss))
            s_ref[j, g * nseq + s] = tot[j] * ss + _dot_tn(ktl[rows], v[rows])
        o = o + (outs[0] if nseq == 1 else jnp.concatenate(outs, axis=0))
        on = o * lax.rsqrt(jnp.mean(o * o, axis=-1, keepdims=True) + NORM_EPS)
        o_ref[g, pl.ds(r, n), j * C_DV:(j + 1) * C_DV] = (on * _silu(xg)).astype(BF16)

    def chunk(ci, carry):
        r = pl.multiple_of(ci * n, n)
        cs = cos[pl.ds(r, n), :]
        sn = sin[pl.ds(r, n), :]
        _round_robin([tile(j, g, r, cs, sn) for j in range(hs) for g in range(g_n)])
        return carry

    lax.fori_loop(0, chunks, chunk, 0)

    @pl.when(tb == pl.num_programs(2) - 1)
    def _():
        for j in range(hs):
            s_out[0, :, j] = s_ref[j]


def _retention(z, state, prev_out, layer, depth, b, t, t0, row_off):
    til = _Tiling(b, t, MIX_ROWS, row_off, C_HEADS, C_DK * C_DV * 4, RET_STREAMS)
    tab, tot = _ret_tables(til.c)
    cosf, sinf = _rope_tables(t0, t, max(MIX_ROWS // t, 1))
    qb = C_OFF // C_DK
    vb = (C_OFF + 2 * C_QK) // C_DV
    gb = (C_OFF + 2 * C_QK + C_WIDTH) // C_DV
    in_specs, args = [], []
    for width, col in ((C_DK, qb), (C_DK, qb + C_HEADS), (C_DV, vb), (C_DV, gb)):
        for j, g in til.streams():
            in_specs.append(til.zspec(j, g, width, col))
            args.append(z)
    in_specs += [
        pl.BlockSpec((til.blk, C_DK), lambda i, h, tb: (tb, 0)),
        pl.BlockSpec((til.blk, C_DK), lambda i, h, tb: (tb, 0)),
        pl.BlockSpec((til.hs, 3, MIX_ROWS, MIX_ROWS), lambda i, h, tb: (h, 0, 0, 0)),
        pl.BlockSpec((til.hs, 1, C_DV), lambda i, h, tb: (h, 0, 0)),
    ]
    args += [cosf, sinf, tab, tot]
    has_state, has_alias, aliases, s_spec, s_shape = _state_io(
        til, state, prev_out, layer, depth, b, C_HEADS, 1, C_DK, C_DV, in_specs, args)
    o, s = pl.pallas_call(
        functools.partial(_ret_kernel, g_n=til.g, hs=til.hs, chunks=til.chunks, nseq=til.nseq, c=til.c,
                          has_state=has_state, has_alias=has_alias),
        grid=(til.steps, til.head_steps, til.nt),
        in_specs=in_specs,
        out_specs=[til.ospec(C_DV), s_spec],
        out_shape=[til.oshape(C_WIDTH), s_shape],
        scratch_shapes=[pltpu.VMEM((til.hs, til.g * til.nseq, C_DK, C_DV), F32)],
        input_output_aliases=aliases,
        compiler_params=_cparams(("parallel", "parallel", "arbitrary")),
        name="retention",
    )(*args)
    return o.reshape(b * t, C_WIDTH), s


@functools.lru_cache(maxsize=None)
def _rwkv_consts(c):
    w = RWKV_ROWS
    t = np.arange(w)
    blk = t // c
    same = blk[:, None] == blk[None, :]
    tri = same & (t[None, :] <= t[:, None])
    cum = np.tile(np.concatenate([tri, same], 0).astype(np.float32), (1, 2))
    strict = same & (t[None, :] < t[:, None])
    masks = np.stack([np.tile(strict, (1, 2)), np.tile(tri, (1, 2)),
                      np.tile(np.eye(w, dtype=bool), (1, 2))]).astype(np.float32)
    rr = np.arange(2 * w)
    hh = rr // B_HEAD
    gmat = np.tile((hh[:, None] == hh[None, :]).astype(np.float32), (2, 1))
    return cum, masks, gmat


def _rwkv_kernel(*refs, g_n, hs, chunks, nseq, c, has_state, has_alias):
    it = iter(refs)
    zr, zk, zv = ([[next(it) for _ in range(g_n)] for _ in range(hs)] for _ in range(3))
    zl = [next(it) for _ in range(g_n)]
    (mu_r, mu_k, mu_v, mu_l, w0, a0, kkp, kap, rkp, gnw, gnb,
     w2, a2, g2, cum, msk, gmat) = (next(it) for _ in range(17))
    if has_state:
        sh_r, sh_k, sh_v, sh_l, s0 = (next(it) for _ in range(5))
    if has_alias:
        next(it)
    o_ref, s_out = next(it), next(it)
    shift_outs = [next(it) for _ in range(4)]
    s_ref = next(it)
    carries = None if has_state else [next(it) for _ in range(4)]
    w = RWKV_ROWS
    n = 2 * w
    blk = chunks * w
    nsq = int(math.log2(c)) - 1
    tb = pl.program_id(2)
    lane = lax.broadcasted_iota(jnp.int32, (w, LANES), 1)
    head0 = lane < B_HEAD
    row = lax.broadcasted_iota(jnp.int32, (w, 1), 0)
    first = (row % c) == 0
    gm = gmat[...]
    gm_f = gm[0:2 * w].astype(F32)

    @pl.when(tb == 0)
    def _():
        if has_state:
            zero_blk = jnp.zeros((B_HEAD, B_HEAD), F32)
            for j in range(hs):
                for s in range(g_n * nseq):
                    top = jnp.concatenate([s0[0, s, 2 * j], zero_blk], axis=1)
                    bot = jnp.concatenate([zero_blk, s0[0, s, 2 * j + 1]], axis=1)
                    s_ref[j, s] = jnp.concatenate([top, bot], axis=0)
        else:
            s_ref[...] = jnp.zeros_like(s_ref)
            for cr in carries:
                cr[...] = jnp.zeros_like(cr)

    def stack(x):
        return jnp.concatenate([jnp.where(head0, x, 0.0), jnp.where(head0, 0.0, x)], axis=0)

    def gsum(x):
        return _dot_sel(x, gm)

    def shifted(ref, sh, cols, carry, mu, g, ci, r):
        x = ref[pl.ds(r, w), :]
        width = x.shape[1]
        if has_state:
            src = jnp.concatenate(
                [jnp.broadcast_to(sh[g * nseq + s:g * nseq + s + 1, cols], (c, width))
                 for s in range(nseq)], axis=0)
        else:
            rp = pl.multiple_of(jnp.maximum(r - SUBLANES, 0), SUBLANES)
            prev8 = jnp.where(ci == 0, carry, ref[pl.ds(rp, SUBLANES), :])
            src = jnp.broadcast_to(prev8[SUBLANES - 1:SUBLANES, :], (w, width))
        prev = jnp.where(first, src, pltpu.roll(x, 1, axis=0))
        return x + mu * (prev - x)

    def tile(j, g, ci, r):
        hc = slice(j * LANES, (j + 1) * LANES)
        lc = slice(0, B_LORA_W + B_LORA_A + B_LORA_G)
        cr = [None] * 4 if carries is None else [carries[0][j, g], carries[1][j, g], carries[2][j, g],
                                                  carries[3][g]]
        xr = shifted(zr[j][g], sh_r if has_state else None, hc, cr[0], mu_r[:, hc], g, ci, r)
        xk = shifted(zk[j][g], sh_k if has_state else None, hc, cr[1], mu_k[:, hc], g, ci, r)
        xv = shifted(zv[j][g], sh_v if has_state else None, hc, cr[2], mu_v[:, hc], g, ci, r)
        xl = shifted(zl[g], sh_l if has_state else None, lc, cr[3], mu_l[...], g, ci, r)
        wd = xl[:, 0:B_LORA_W]
        ad = xl[:, B_LORA_W:B_LORA_W + B_LORA_A]
        gd = xl[:, B_LORA_W + B_LORA_A:]
        wx = -(w0[:, hc] + _dot(jnp.tanh(wd), w2[:, hc]))
        w_raw = -(jnp.maximum(wx, 0.0) + jnp.log1p(jnp.exp(-jnp.abs(wx)))) - 0.5
        lw = -jnp.exp(w_raw)
        aa = _sigmoid(a0[:, hc] + _dot(ad, a2[:, hc]))
        gb = _dot(_sigmoid(gd), g2[:, hc])
        yield
        kk = xk * kkp[:, hc]
        kk = kk / jnp.maximum(jnp.sqrt(gsum(kk * kk)), 1e-12)
        k2 = xk * (1.0 + (aa - 1.0) * kap[:, hc])
        a = -kk
        b = kk * aa
        yield
        e = _sel_dot(cum[...], lw)
        yield
        lwc = e[0:w]
        lwl = e[w:n]
        dec_in = jnp.exp(lwc)
        dec_ex = jnp.exp(lwc - lw)
        inv = jnp.exp(-lwc)
        rest = jnp.exp(lwl - lwc)
        a_t = a * dec_ex
        r_t = xr * dec_in
        b_t = b * inv
        k_t = k2 * inv
        gram = _dot_nt(jnp.concatenate([a_t, r_t], axis=0),
                       jnp.concatenate([stack(b_t), stack(k_t)], axis=0))
        yield
        m_ab = gram[0:w, 0:n] * msk[0]
        m_ak = gram[0:w, n:2 * n] * msk[0]
        m_rb = gram[w:n, 0:n] * msk[1]
        m_rk = gram[w:n, n:2 * n] * msk[1]
        p = _dot(m_ab, stack(m_ab))
        tinv = msk[2] + m_ab
        yield
        for lev in range(nsq):
            if lev + 1 < nsq:
                both = _dot(p, jnp.concatenate([stack(p), stack(tinv)], axis=1))
                p = both[:, 0:n]
                tinv = tinv + both[:, n:2 * n]
            else:
                tinv = tinv + _dot(p, stack(tinv))
            yield
        p0a, p0r = [], []
        for s in range(nseq):
            rows = slice(s * c, (s + 1) * c)
            pr = _dot_nt(jnp.concatenate([a_t[rows], r_t[rows]], axis=0), s_ref[j, g * nseq + s])
            p0a.append(pr[0:c])
            p0r.append(pr[c:2 * c])
        p0a = p0a[0] if nseq == 1 else jnp.concatenate(p0a, axis=0)
        p0r = p0r[0] if nseq == 1 else jnp.concatenate(p0r, axis=0)
        yield
        vs = stack(xv)
        rhs = p0a + _dot(m_ak, vs)
        yield
        u_w = _dot(tinv, stack(rhs))
        yield
        y = p0r + _dot(m_rb, stack(u_w)) + _dot(m_rk, vs)
        yield
        b_g = b * rest
        k_g = k2 * rest
        dec_l = jnp.exp(lwl)
        for s in range(nseq):
            rows = slice(s * c, (s + 1) * c)
            upd = _dot_tn(jnp.concatenate([u_w[rows], xv[rows]], axis=0),
                          jnp.concatenate([b_g[rows], k_g[rows]], axis=0))
            s_ref[j, g * nseq + s] = s_ref[j, g * nseq + s] * dec_l[s * c:s * c + 1, :] + gm_f * upd
        yield
        mean = gsum(y) * (1.0 / B_HEAD)
        yield
        d = y - mean
        var = gsum(d * d) * (1.0 / B_HEAD)
        yn = d * lax.rsqrt(var + RWKV_GN_EPS) * gnw[:, hc] + gnb[:, hc]
        yield
        bonus = gsum(xr * k2 * rkp[:, hc])
        o_ref[g, pl.ds(r, w), hc] = ((yn + bonus * xv) * gb).astype(BF16)

    def chunk(ci, carry):
        r = pl.multiple_of(ci * w, w)
        _round_robin([tile(j, g, ci, r) for j in range(hs) for g in range(g_n)])
        return carry

    lax.fori_loop(0, chunks, chunk, 0)

    base = blk - w
    for g in range(g_n):
        per_head = [(zr[j][g], shift_outs[0], j) for j in range(hs)]
        per_head += [(zk[j][g], shift_outs[1], j) for j in range(hs)]
        per_head += [(zv[j][g], shift_outs[2], j) for j in range(hs)]
        for ref, out, j in per_head + [(zl[g], shift_outs[3], 0)]:
            cols = slice(j * LANES, j * LANES + ref.shape[1])
            for s in range(nseq):
                last = base + (s + 1) * c - 1
                out[g * nseq + s:g * nseq + s + 1, cols] = ref[last:last + 1, :]
        if carries is not None:
            for j in range(hs):
                carries[0][j, g] = zr[j][g][blk - SUBLANES:blk, :]
                carries[1][j, g] = zk[j][g][blk - SUBLANES:blk, :]
                carries[2][j, g] = zv[j][g][blk - SUBLANES:blk, :]
            carries[3][g] = zl[g][blk - SUBLANES:blk, :]

    @pl.when(tb == pl.num_programs(2) - 1)
    def _():
        for j in range(hs):
            for s in range(g_n * nseq):
                ss = s_ref[j, s]
                s_out[0, s, 2 * j] = ss[0:B_HEAD, 0:B_HEAD]
                s_out[0, s, 2 * j + 1] = ss[B_HEAD:n, B_HEAD:n]


def _rwkv(z, p, state, shift, prev_out, layer, depth, b, t, row_off):
    pairs = B_HEADS // 2
    til = _Tiling(b, t, RWKV_ROWS, row_off, pairs, 2 * B_HEAD * B_HEAD * 4, RWKV_STREAMS)
    cum, msk, gmat = _rwkv_consts(til.c)
    cb = B_OFF // LANES
    wb = B_WIDTH // LANES
    lora_w = B_LORA_W + B_LORA_A + B_LORA_G
    lb_z = (B_OFF + 3 * B_WIDTH) // lora_w
    lb_s = (3 * B_WIDTH) // lora_w

    def vec(x):
        return x.reshape(1, -1)

    def pspec(rows_, col_off=0):
        return til.hspec(rows_, LANES, col_off)

    in_specs, args = [], []
    for col in (cb, cb + wb, cb + 2 * wb):
        for j, g in til.streams():
            in_specs.append(til.zspec(j, g, LANES, col))
            args.append(z)
    for g in range(til.g):
        in_specs.append(til.zspec(0, g, lora_w, lb_z, per_head=0))
        args.append(z)
    in_specs += [
        pspec(1), pspec(1, wb), pspec(1, 2 * wb),
        pl.BlockSpec((1, lora_w), lambda i, h, tb: (0, lb_s)),
        pspec(1), pspec(1), pspec(1), pspec(1), pspec(1), pspec(1), pspec(1),
        pspec(B_LORA_W), pspec(B_LORA_A), pspec(B_LORA_G),
        _const_spec(cum.shape), _const_spec(msk.shape), _const_spec(gmat.shape),
    ]
    mu = vec(p['rwkv_mu'])
    args += [mu, mu, mu, mu,
             vec(p['rwkv_w0']), vec(p['rwkv_a0']), vec(p['rwkv_kk']), vec(p['rwkv_ka']),
             vec(p['rwkv_rk']), vec(p['rwkv_gn_w']), vec(p['rwkv_gn_b']),
             p['rwkv_w2'].astype(BF16), p['rwkv_a2'].astype(BF16), p['rwkv_g2'].astype(BF16),
             jnp.asarray(cum, BF16), jnp.asarray(msk, F32), jnp.asarray(gmat, BF16)]
    ns = til.g * til.nseq
    if state is not None:
        hw = til.hs * LANES
        wbh = wb // til.hs
        in_specs += [
            pl.BlockSpec((ns, hw), lambda i, h, tb: (i, h)),
            pl.BlockSpec((ns, hw), lambda i, h, tb: (i, wbh + h)),
            pl.BlockSpec((ns, hw), lambda i, h, tb: (i, 2 * wbh + h)),
            pl.BlockSpec((ns, lora_w), lambda i, h, tb: (i, lb_s)),
        ]
        args += [shift, shift, shift, shift]
    has_state, has_alias, aliases, s_spec, s_shape = _state_io(
        til, state, prev_out, layer, depth, b, B_HEADS, 2, B_HEAD, B_HEAD, in_specs, args)
    scratch = [pltpu.VMEM((til.hs, til.g * til.nseq, 2 * B_HEAD, 2 * B_HEAD), F32)]
    if not has_state:
        scratch += [pltpu.VMEM((til.hs, til.g, SUBLANES, LANES), F32) for _ in range(3)]
        scratch += [pltpu.VMEM((til.g, SUBLANES, lora_w), F32)]
    o, s, sh_r, sh_k, sh_v, sh_l = pl.pallas_call(
        functools.partial(_rwkv_kernel, g_n=til.g, hs=til.hs, chunks=til.chunks, nseq=til.nseq, c=til.c,
                          has_state=has_state, has_alias=has_alias),
        grid=(til.steps, til.head_steps, til.nt),
        in_specs=in_specs,
        out_specs=[til.ospec(LANES), s_spec,
                   pl.BlockSpec((ns, til.hs * LANES), lambda i, h, tb: (i, h)),
                   pl.BlockSpec((ns, til.hs * LANES), lambda i, h, tb: (i, h)),
                   pl.BlockSpec((ns, til.hs * LANES), lambda i, h, tb: (i, h)),
                   pl.BlockSpec((ns, lora_w), lambda i, h, tb: (i, 0))],
        out_shape=[til.oshape(B_WIDTH), s_shape] + [
            jax.ShapeDtypeStruct((b, wd), F32) for wd in (B_WIDTH, B_WIDTH, B_WIDTH, lora_w)],
        scratch_shapes=scratch,
        input_output_aliases=aliases,
        compiler_params=_cparams(("parallel", "parallel", "arbitrary")),
        name="rwkv7",
    )(*args)
    return o.reshape(b * t, B_WIDTH), s, jnp.concatenate([sh_r, sh_k, sh_v, sh_l], axis=1)


def _merge_kernel(oa, ob, oc, wa, wb, wc, ga, gb, gc, *rest):
    o_ref = rest[-1]
    acc = _sigmoid(ga[...]) * jnp.dot(oa[...], wa[...].astype(BF16), preferred_element_type=F32)
    acc = acc + _sigmoid(gb[...]) * jnp.dot(ob[...], wb[...].astype(BF16), preferred_element_type=F32)
    acc = acc + _sigmoid(gc[...]) * jnp.dot(oc[...], wc[...].astype(BF16), preferred_element_type=F32)
    o_ref[...] = acc.astype(BF16)


def _merge(z, oa, ob, oc, wa, wb, wc, layer, row_off, prev_out):
    m = oa.shape[0]
    tm = _pick(m, (1024, 512, 256, 128))
    tn = 256
    gblk = G_OFF // tn
    nb = D_MODEL // tn
    assert G_OFF % tn == 0 and row_off % tm == 0
    off = row_off // tm

    def ospec():
        return pl.BlockSpec((tm, oa.shape[1]), lambda i, j: (i, 0))

    def wspec():
        return pl.BlockSpec((None, wa.shape[1], tn), lambda i, j: (layer, 0, j))

    def gspec(br):
        return pl.BlockSpec((tm, tn), lambda i, j: (i + off, gblk + br * nb + j))

    in_specs = [ospec(), ospec(), ospec(), wspec(), wspec(), wspec(), gspec(0), gspec(1), gspec(2)]
    args = [oa, ob, oc, wa, wb, wc, z, z, z]
    aliases = {}
    if prev_out is not None:
        aliases = {len(args): 0}
        in_specs.append(pl.BlockSpec(memory_space=pl.ANY))
        args.append(prev_out)
    return pl.pallas_call(
        _merge_kernel,
        grid=(m // tm, nb),
        in_specs=in_specs,
        out_specs=pl.BlockSpec((tm, tn), lambda i, j: (i + off, j)),
        out_shape=jax.ShapeDtypeStruct((z.shape[0], D_MODEL), BF16),
        input_output_aliases=aliases,
        compiler_params=_cparams(("parallel", "arbitrary")),
        name="merge",
    )(*args)


def _proj_res_kernel(m_ref, w_ref, g_ref, x_ref, o_ref):
    y = jnp.dot(m_ref[...], w_ref[...], preferred_element_type=F32)
    yn = y * lax.rsqrt(jnp.mean(y * y, axis=-1, keepdims=True) + NORM_EPS) * g_ref[...]
    o_ref[...] = x_ref[...] + yn


def _proj_res(mrg, w, g, x, layer):
    m = x.shape[0]
    tm = _pick(m, (512, 256, 128))
    return pl.pallas_call(
        _proj_res_kernel,
        grid=(m // tm,),
        in_specs=[
            pl.BlockSpec((tm, D_MODEL), lambda i: (i, 0)),
            pl.BlockSpec((None, D_MODEL, D_MODEL), lambda i: (layer, 0, 0)),
            pl.BlockSpec((1, D_MODEL), lambda i: (0, 0)),
            pl.BlockSpec((tm, D_MODEL), lambda i: (i, 0)),
        ],
        out_specs=pl.BlockSpec((tm, D_MODEL), lambda i: (i, 0)),
        out_shape=jax.ShapeDtypeStruct((m, D_MODEL), F32),
        compiler_params=_cparams(("parallel",)),
        name="proj_res",
    )(mrg, w, g.reshape(1, D_MODEL), x)


def _gelu(x):
    return 0.5 * x * (1.0 + jnp.tanh(math.sqrt(2.0 / math.pi) * (x + 0.044715 * (x * x * x))))


def _up_act_kernel(*refs, tm, t, blocks_per_seq, has_state, has_alias):
    it = iter(refs)
    x_ref, g_ref, wa, wb, cw, cb = (next(it) for _ in range(6))
    st = next(it) if has_state else None
    if has_alias:
        next(it)
    o_ref, nc_ref, xn_ref = next(it), next(it), next(it)
    tail = None if has_state else next(it)
    i = pl.program_id(0)
    j = pl.program_id(1)
    tn = wa.shape[1]
    sub = UP_SUB

    @pl.when(j == 0)
    def _():
        rows = min(tm, LANES)

        def body(r, carry):
            r0 = pl.multiple_of(r * rows, rows)
            x = x_ref[pl.ds(r0, rows), :]
            ms = jnp.mean(x * x, axis=-1, keepdims=True)
            xn_ref[pl.ds(r0, rows), :] = (x * lax.rsqrt(ms + NORM_EPS) * g_ref[...]).astype(BF16)
            return carry
        lax.fori_loop(0, tm // rows, body, 0)

    if has_state:
        ns = tm // t
        tt = lax.broadcasted_iota(jnp.int32, (ns, t, sub), 1)
    else:
        seq_start = (i % blocks_per_seq) == 0
        rr = lax.broadcasted_iota(jnp.int32, (tm, sub), 0)
    for c in range(tn // sub):
        cols = slice(c * sub, (c + 1) * sub)
        ua = jnp.dot(xn_ref[...], wa[:, cols].astype(BF16), preferred_element_type=F32)
        ub = jnp.dot(xn_ref[...], wb[:, cols].astype(BF16), preferred_element_type=F32)
        if has_state:
            x3 = ua.reshape(ns, t, sub)
            s_old = st[:, 0:1, cols]
            s_new = st[:, 1:2, cols]
            prev1 = jnp.where(tt >= 1, pltpu.roll(x3, 1, axis=1), s_new)
            prev2 = jnp.where(tt >= 2, pltpu.roll(x3, 2, axis=1), jnp.where(tt == 1, s_new, s_old))
            prev1 = prev1.reshape(tm, sub)
            prev2 = prev2.reshape(tm, sub)
            nc_ref[:, :, cols] = x3[:, t - (CONV_W - 1):, :]
        else:
            h = jnp.where(seq_start, 0.0, tail[j, :, cols])
            h1 = h[SUBLANES - 1:SUBLANES, :]
            h2 = h[SUBLANES - 2:SUBLANES - 1, :]
            prev1 = jnp.where(rr == 0, h1, pltpu.roll(ua, 1, axis=0))
            prev2 = jnp.where(rr == 0, h2, jnp.where(rr == 1, h1, pltpu.roll(ua, 2, axis=0)))
            tail[j, :, cols] = ua[tm - SUBLANES:, :]
            nc_ref[0, :, cols] = ua[tm - (CONV_W - 1):, :]
        conv = cb[:, cols] + cw[0:1, cols] * prev2 + cw[1:2, cols] * prev1 + cw[2:3, cols] * ua
        o_ref[:, cols] = (_gelu(conv) * ub).astype(BF16)


def _up_act(x1, g, w_up, cw, cb, state, prev_out, layer, b, t, row_off):
    m = b * t
    total = x1.shape[0]
    has_state = state is not None
    has_alias = prev_out is not None
    tn = UP_TN
    nj = D_FF // tn
    if has_state:
        tm = _pick(m, (1024, 512, 256, 128))
        assert tm % t == 0 and t == SUBLANES
        blocks_per_seq = 1
        nc_spec = pl.BlockSpec((tm // t, CONV_W - 1, tn), lambda i, j: (i, 0, j))
    else:
        tm = _pick(t, (1024, 512, 256, 128))
        blocks_per_seq = t // tm
        nc_spec = pl.BlockSpec((1, CONV_W - 1, tn), lambda i, j: (i, 0, j))
    nc_rows = b * blocks_per_seq
    off = row_off // tm
    assert row_off % tm == 0
    in_specs = [
        pl.BlockSpec((tm, D_MODEL), lambda i, j: (i + off, 0)),
        pl.BlockSpec((1, D_MODEL), lambda i, j: (0, 0)),
        pl.BlockSpec((None, D_MODEL, tn), lambda i, j: (layer, 0, j)),
        pl.BlockSpec((None, D_MODEL, tn), lambda i, j: (layer, 0, nj + j)),
        pl.BlockSpec((CONV_W, tn), lambda i, j: (0, j)),
        pl.BlockSpec((1, tn), lambda i, j: (0, j)),
    ]
    args = [x1, g.reshape(1, D_MODEL), w_up, w_up, cw, cb.reshape(1, D_FF)]
    if has_state:
        in_specs.append(pl.BlockSpec((tm // t, CONV_W - 1, tn), lambda i, j: (i, 0, j)))
        args.append(state)
    aliases = {}
    if has_alias:
        aliases = {len(args): 0}
        in_specs.append(pl.BlockSpec(memory_space=pl.ANY))
        args.append(prev_out)
    scratch = [pltpu.VMEM((tm, D_MODEL), BF16)]
    if not has_state:
        scratch.append(pltpu.VMEM((nj, SUBLANES, tn), F32))
    act, nc = pl.pallas_call(
        functools.partial(_up_act_kernel, tm=tm, t=t, blocks_per_seq=blocks_per_seq,
                          has_state=has_state, has_alias=has_alias),
        grid=(m // tm, nj),
        in_specs=in_specs,
        out_specs=[pl.BlockSpec((tm, tn), lambda i, j: (i + off, j)), nc_spec],
        out_shape=[jax.ShapeDtypeStruct((total, D_FF), BF16),
                   jax.ShapeDtypeStruct((nc_rows, CONV_W - 1, D_FF), F32)],
        scratch_shapes=scratch,
        input_output_aliases=aliases,
        compiler_params=_cparams(("arbitrary", "arbitrary")),
        name="up_act",
    )(*args)
    return act, nc[blocks_per_seq - 1::blocks_per_seq]


def _down_res_kernel(a_ref, w_ref, g_ref, x_ref, o_ref, acc):
    kstep = pl.program_id(1)

    @pl.when(kstep == 0)
    def _():
        acc[...] = jnp.zeros_like(acc)

    acc[...] += jnp.dot(a_ref[...], w_ref[...], preferred_element_type=F32)

    @pl.when(kstep == pl.num_programs(1) - 1)
    def _():
        y = acc[...]
        yn = y * lax.rsqrt(jnp.mean(y * y, axis=-1, keepdims=True) + NORM_EPS) * g_ref[...]
        o_ref[...] = x_ref[...] + yn


def _down_res(act, wd, g, x1, layer, row_off, rows):
    tm = _pick(rows, (512, 256, 128))
    tk = DOWN_TK
    off = row_off // tm
    assert row_off % tm == 0
    return pl.pallas_call(
        _down_res_kernel,
        grid=(rows // tm, D_FF // tk),
        in_specs=[
            pl.BlockSpec((tm, tk), lambda i, k: (i + off, k)),
            pl.BlockSpec((None, tk, D_MODEL), lambda i, k: (layer, k, 0)),
            pl.BlockSpec((1, D_MODEL), lambda i, k: (0, 0)),
            pl.BlockSpec((tm, D_MODEL), lambda i, k: (i + off, 0)),
        ],
        out_specs=pl.BlockSpec((tm, D_MODEL), lambda i, k: (i, 0)),
        out_shape=jax.ShapeDtypeStruct((rows, D_MODEL), F32),
        scratch_shapes=[pltpu.VMEM((tm, D_MODEL), F32)],
        compiler_params=_cparams(("parallel", "arbitrary")),
        name="down_res",
    )(act, wd, g.reshape(1, D_MODEL), x1)


def _layer(x, groups, lb, p, layer, depth, prev, split_out):
    z = _rms_matmul(x, p['pre_mix_g'], p['w_in'], layer, tn=1280)
    states = []
    mrg = None
    row = 0
    for gi, (b, t, t0, st) in enumerate(groups):
        s_a, s_b, s_sh, s_c, _ = st if st is not None else (None,) * 5
        pv = prev[gi] if prev is not None else (None,) * 3
        o_a, n_a = _hgrn(z, lb, p['a_norm_g'], s_a, pv[0], layer, depth, b, t, row)
        o_b, n_b, n_sh = _rwkv(z, p, s_b, s_sh, pv[1], layer, depth, b, t, row)
        o_c, n_c = _retention(z, s_c, pv[2], layer, depth, b, t, t0, row)
        mrg = _merge(z, o_a, o_b, o_c, p['w_br_a'], p['w_br_b'], p['w_br_c'], layer, row, mrg)
        states.append([n_a, n_b, n_sh, n_c])
        row += b * t
    x1 = _proj_res(mrg, p['w_out'], p['post_mix_g'], x, layer)
    act = None
    row = 0
    for gi, (b, t, t0, st) in enumerate(groups):
        s_cv = st[4] if st is not None else None
        act, n_cv = _up_act(x1, p['pre_ffn_g'], p['w_up'], p['conv_w'], p['conv_b'], s_cv, act,
                            layer, b, t, row)
        states[gi].append(n_cv)
        row += b * t
    if split_out:
        x2, row = [], 0
        for (b, t, _, _) in groups:
            x2.append(_down_res(act, p['w_down'], p['post_ffn_g'], x1, layer, row, b * t))
            row += b * t
    else:
        x2 = _down_res(act, p['w_down'], p['post_ffn_g'], x1, layer, 0, x1.shape[0])
    return x2, states


def kernel(x_prompt, x_sample, state_hgrn, state_rwkv, state_rwkv_shift, state_ret, state_conv,
           lb_logits, pre_mix_g, w_in, a_norm_g, rwkv_mu, rwkv_w0, rwkv_w2, rwkv_a0, rwkv_a2,
           rwkv_g2, rwkv_kk, rwkv_ka, rwkv_rk, rwkv_gn_w, rwkv_gn_b, w_br_a, w_br_b, w_br_c,
           w_out, post_mix_g, pre_ffn_g, w_up, conv_w, conv_b, w_down, post_ffn_g):
    depth = w_in.shape[0]
    bp, tp, _ = x_prompt.shape
    bs, ts, _ = x_sample.shape
    past_len = 16384
    lb_soft = jax.nn.softmax(lb_logits.astype(F32), axis=0)
    lbs = jnp.cumsum(lb_soft, axis=0) - lb_soft[0]
    big = {'w_in': w_in, 'w_br_a': w_br_a, 'w_br_b': w_br_b,
           'w_br_c': w_br_c, 'w_out': w_out.astype(BF16), 'w_up': w_up,
           'w_down': w_down.astype(BF16)}
    x = jnp.concatenate([x_prompt.reshape(bp * tp, D_MODEL), x_sample.reshape(bs * ts, D_MODEL)], axis=0)
    small = [[[], []], [[], []]]
    prev = None
    for l in range(depth):
        p = dict(big)
        p.update({
            'pre_mix_g': pre_mix_g[l], 'a_norm_g': a_norm_g[l],
            'rwkv_mu': rwkv_mu[l], 'rwkv_w0': rwkv_w0[l], 'rwkv_w2': rwkv_w2[l],
            'rwkv_a0': rwkv_a0[l], 'rwkv_a2': rwkv_a2[l], 'rwkv_g2': rwkv_g2[l],
            'rwkv_kk': rwkv_kk[l], 'rwkv_ka': rwkv_ka[l], 'rwkv_rk': rwkv_rk[l],
            'rwkv_gn_w': rwkv_gn_w[l], 'rwkv_gn_b': rwkv_gn_b[l],
            'post_mix_g': post_mix_g[l], 'pre_ffn_g': pre_ffn_g[l],
            'conv_w': conv_w[l], 'conv_b': conv_b[l], 'post_ffn_g': post_ffn_g[l],
        })
        groups = [
            (bp, tp, 0, None),
            (bs, ts, past_len, (state_hgrn, state_rwkv, state_rwkv_shift[l], state_ret, state_conv[l])),
        ]
        x, states = _layer(x, groups, lbs[l], p, l, depth, prev, l == depth - 1)
        prev = [(st[0], st[1], st[3]) for st in states]
        for gi, st in enumerate(states):
            small[gi][0].append(st[2])
            small[gi][1].append(st[4])
    y_p = x[0].reshape(bp, tp, D_MODEL)
    y_s = x[1].reshape(bs, ts, D_MODEL)
    outs = []
    for gi in range(2):
        outs += [prev[gi][0], prev[gi][1], jnp.stack(small[gi][0]), prev[gi][2], jnp.stack(small[gi][1])]
    return (y_p, y_s, *outs)
```

```python
import functools
import itertools
import math

import jax
import jax.numpy as jnp
import numpy as np
from jax import lax
from jax.experimental import pallas as pl
from jax.experimental.pallas import tpu as pltpu

F32 = jnp.float32
BF16 = jnp.bfloat16

D_MODEL = 2048
A_HEADS, A_DK, A_DV = 8, 128, 128
A_QK = A_HEADS * A_DK
A_WIDTH = A_HEADS * A_DV
F_TINY = 1e-30
B_HEAD = 64
B_WIDTH = 1024
B_HEADS = B_WIDTH // B_HEAD
B_LORA_W, B_LORA_A, B_LORA_G = 64, 64, 128
RWKV_GN_EPS = 64e-5
C_HEADS, C_DK, C_DV = 4, 128, 256
C_QK = C_HEADS * C_DK
C_WIDTH = C_HEADS * C_DV
ROPE_BASE = 10000.0
A_COLS = 2 * A_QK + 2 * A_WIDTH
B_COLS = 3 * B_WIDTH + B_LORA_W + B_LORA_A + B_LORA_G
C_COLS = 2 * C_QK + 2 * C_WIDTH
N_BRANCH = 3
P_COLS = A_COLS + B_COLS + C_COLS + N_BRANCH * D_MODEL
B_OFF = A_COLS
C_OFF = A_COLS + B_COLS
G_OFF = A_COLS + B_COLS + C_COLS
D_FF = 5632
CONV_W = 3
NORM_EPS = 1e-6

LANES = 128
SUBLANES = 8
MXU_COLS = 256
MIX_ROWS = 128
RWKV_ROWS = 64
MIX_TIME_BLOCK = 512
MIX_STREAMS = 4
HGRN_STREAMS = 8
HGRN_MATMUL_LEVEL_ROWS = 8
RET_STREAMS = 8
RWKV_STREAMS = 16
STATE_WINDOW_BYTES = 4 * 1024 * 1024
UP_TN = 512
UP_SUB = 256
DOWN_TK = 2816
VMEM_LIMIT = 56 * 1024 * 1024


def _cparams(sem):
    return pltpu.CompilerParams(dimension_semantics=sem, vmem_limit_bytes=VMEM_LIMIT)


def _dot(a, b):
    return jnp.dot(a.astype(BF16), b.astype(BF16), preferred_element_type=F32)


def _dot_nt(a, b):
    return lax.dot_general(a.astype(BF16), b.astype(BF16), (((1,), (1,)), ((), ())),
                           preferred_element_type=F32)


def _dot_tn(a, b):
    return lax.dot_general(a.astype(BF16), b.astype(BF16), (((0,), (0,)), ((), ())),
                           preferred_element_type=F32)


def _split(x):
    hi = x.astype(BF16)
    lo = (x - hi.astype(F32)).astype(BF16)
    return hi, lo


def _sel_dot(m2, x):
    hi, lo = _split(x)
    return jnp.dot(m2, jnp.concatenate([hi, lo], axis=0), preferred_element_type=F32)


def _dot_sel(x, m2):
    hi, lo = _split(x)
    return jnp.dot(jnp.concatenate([hi, lo], axis=1), m2, preferred_element_type=F32)


def _sigmoid(x):
    return 0.5 * jnp.tanh(0.5 * x) + 0.5


def _silu(x):
    h = 0.5 * x
    return h * jnp.tanh(h) + h


def _round_robin(gens):
    for _ in itertools.zip_longest(*gens):
        pass


def _pick(n, cands):
    for c in cands:
        if n % c == 0:
            return c
    raise ValueError(f"no tile in {cands} divides {n}")


def _rms_matmul_kernel(x_ref, g_ref, w_ref, o_ref, xn_ref, *, tm, sub):
    @pl.when(pl.program_id(1) == 0)
    def _():
        def body(i, carry):
            r = pl.multiple_of(i * sub, sub)
            x = x_ref[pl.ds(r, sub), :]
            ms = jnp.mean(x * x, axis=-1, keepdims=True)
            xn_ref[pl.ds(r, sub), :] = (x * lax.rsqrt(ms + NORM_EPS) * g_ref[...]).astype(BF16)
            return carry
        lax.fori_loop(0, tm // sub, body, 0)

    tn = w_ref.shape[1]
    for c in range(tn // MXU_COLS):
        cols = slice(c * MXU_COLS, (c + 1) * MXU_COLS)
        o_ref[:, cols] = jnp.dot(xn_ref[...], w_ref[:, cols].astype(BF16), preferred_element_type=F32)


def _rms_matmul(x, g, w, layer, tn):
    m, k = x.shape
    n = w.shape[2]
    tm = _pick(m, (1024, 512, 256, 128))
    sub = min(tm, 128)
    assert tn % MXU_COLS == 0
    return pl.pallas_call(
        functools.partial(_rms_matmul_kernel, tm=tm, sub=sub),
        grid=(m // tm, n // tn),
        in_specs=[
            pl.BlockSpec((tm, k), lambda i, j: (i, 0), pipeline_mode=pl.Buffered(1)),
            pl.BlockSpec((1, k), lambda i, j: (0, 0)),
            pl.BlockSpec((None, k, tn), lambda i, j: (layer, 0, j)),
        ],
        out_specs=pl.BlockSpec((tm, tn), lambda i, j: (i, j)),
        out_shape=jax.ShapeDtypeStruct((m, n), F32),
        scratch_shapes=[pltpu.VMEM((tm, k), BF16)],
        compiler_params=_cparams(("parallel", "arbitrary")),
        name="rms_matmul",
    )(x, g.reshape(1, k), w)


class _Tiling:
    def __init__(self, b, t, rows, row_off, heads, state_bytes, streams_total):
        if t >= rows:
            self.nseq, self.c = 1, rows
            self.blk = _pick(t, (MIX_TIME_BLOCK, rows))
            self.nt = t // self.blk
            n_streams = b
        else:
            assert rows % t == 0 and b % (rows // t) == 0
            self.nseq, self.c = rows // t, t
            self.blk, self.nt = rows, 1
            n_streams = b // self.nseq
        self.g = _pick(n_streams, (MIX_STREAMS, 2, 1))
        self.hs = _pick(heads, (max(streams_total // self.g, 1), 2, 1))
        while self.g * self.nseq * self.hs * state_bytes > STATE_WINDOW_BYTES and self.hs > 1:
            self.hs //= 2
        while self.g * self.nseq * self.hs * state_bytes > STATE_WINDOW_BYTES and self.g > 1:
            self.g //= 2
        self.head_steps = heads // self.hs
        self.steps = n_streams // self.g
        self.n_streams = n_streams
        self.chunks = self.blk // rows
        assert row_off % self.blk == 0
        self.off = row_off // self.blk

    def streams(self):
        return [(j, g) for j in range(self.hs) for g in range(self.g)]

    def zspec(self, j, stream, width, col_blk, per_head=1):
        g, nt, off, hs = self.g, self.nt, self.off, self.hs
        return pl.BlockSpec(
            (self.blk, width),
            lambda i, h, tb: (off + (i * g + stream) * nt + tb, col_blk + per_head * (h * hs + j)))

    def hspec(self, rows, width, blk_off=0):
        assert blk_off % self.hs == 0
        off = blk_off // self.hs
        return pl.BlockSpec((rows, self.hs * width), lambda i, h, tb: (0, off + h))

    def ospec(self, width):
        return pl.BlockSpec((self.g, self.blk, self.hs * width), lambda i, h, tb: (i, tb, h))

    def oshape(self, width):
        return jax.ShapeDtypeStruct((self.n_streams, self.nt * self.blk, width), BF16)

    def sspec(self, layer, heads_per_step, d0, d1):
        return pl.BlockSpec((1, self.g * self.nseq, self.hs * heads_per_step, d0, d1),
                            lambda i, h, tb: (layer, i, h, 0, 0))


def _const_spec(shape):
    nd = len(shape)
    return pl.BlockSpec(shape, lambda i, h, tb: (0,) * nd)


def _state_io(til, state, prev_out, layer, depth, b, heads, heads_per_step, d0, d1,
              in_specs, args):
    has_state = state is not None
    if has_state:
        in_specs.append(til.sspec(layer, heads_per_step, d0, d1))
        args.append(state)
    aliases = {}
    if prev_out is not None:
        aliases = {len(args): 1}
        in_specs.append(pl.BlockSpec(memory_space=pl.ANY))
        args.append(prev_out)
    out_spec = til.sspec(layer, heads_per_step, d0, d1)
    out_shape = jax.ShapeDtypeStruct((depth, b, heads, d0, d1), F32)
    return has_state, prev_out is not None, aliases, out_spec, out_shape


@functools.lru_cache(maxsize=None)
def _hgrn_consts(c):
    n = MIX_ROWS
    nlev = int(math.log2(c))
    t = np.arange(n)
    u = np.arange(n)[None, :]
    blk = t // c
    same = blk[:, None] == blk[None, :]
    mats = [same & (u <= t[:, None]), same]
    masks = [np.eye(n, dtype=bool)]
    for lev in range(nlev):
        h = 1 << lev
        base = (t // (2 * h)) * (2 * h)
        mid = base + h
        upper = t >= mid
        e_up = (u >= mid[:, None]) & (u <= t[:, None])
        e_lo = (u >= t[:, None] + 1) & (u <= mid[:, None] - 1)
        if 2 * h <= HGRN_MATMUL_LEVEL_ROWS:
            mats.append(np.where(upper[:, None], e_up, e_lo))
        masks.append((base[:, None] == base[None, :]) & upper[:, None] & (~upper)[None, :])
    sel = np.tile(np.concatenate(mats, 0).astype(np.float32), (1, 2))
    msk = np.stack(masks).astype(np.float32)
    return sel, msk, nlev


def _hgrn_kernel(*refs, g_n, hs, chunks, nseq, c, nlev, has_state, has_alias):
    it = iter(refs)
    zq, zf, zi, zg = ([[next(it) for _ in range(g_n)] for _ in range(hs)] for _ in range(4))
    lb, gn, sel, msk = (next(it) for _ in range(4))
    s0 = next(it) if has_state else None
    if has_alias:
        next(it)
    o_ref, s_out, s_ref = next(it), next(it), next(it)
    n = MIX_ROWS
    tb = pl.program_id(2)

    @pl.when(tb == 0)
    def _():
        if has_state:
            for j in range(hs):
                s_ref[j] = s0[0, :, j]
        else:
            s_ref[...] = jnp.zeros_like(s_ref)

    row_id = lax.broadcasted_iota(jnp.int32, (n, 1), 0)

    def tile(j, g, r):
        hcols = slice(j * LANES, (j + 1) * LANES)
        lbv = lb[:, hcols]
        xq = zq[j][g][pl.ds(r, n), :]
        fa = zf[j][g][pl.ds(r, n), :]
        v = zi[j][g][pl.ds(r, n), :]
        xg = zg[j][g][pl.ds(r, n), :]
        q = _silu(xq)
        th = jnp.tanh(0.5 * fa)
        f_gate = lbv + (1.0 - lbv) * (0.5 + 0.5 * th)
        gl = jnp.log(jnp.maximum(f_gate, F_TINY))
        k = (1.0 - lbv) * (0.5 - 0.5 * th)
        e = _sel_dot(sel[...], gl)
        yield
        b = e[0:n]
        bl = e[n:2 * n]
        scores = msk[0] * _dot_nt(q, k)
        for lev in range(nlev):
            size = 2 << lev
            if size <= HGRN_MATMUL_LEVEL_ROWS:
                ex = e[(lev + 2) * n:(lev + 3) * n]
            else:
                ref = jnp.concatenate(
                    [jnp.broadcast_to(b[s0 + size // 2 - 1:s0 + size // 2, :], (size, A_DK))
                     for s0 in range(0, n, size)], axis=0)
                upper = (row_id % size) >= size // 2
                ex = jnp.where(upper, b - ref, ref - b)
            x = jnp.exp(ex)
            scores = scores + msk[lev + 1] * _dot_nt(q * x, k * x)
        yield
        o = _dot(scores, v)
        yield
        qe = q * jnp.exp(b)
        kt = k * jnp.exp(bl - b)
        dt = jnp.exp(bl).T
        outs = []
        for s in range(nseq):
            rows = slice(s * c, (s + 1) * c)
            ss = s_ref[j, g * nseq + s]
            outs.append(_dot(qe[rows], ss))
            dcol = jnp.broadcast_to(dt[:, s * c:s * c + 1], (A_DK, A_DV))
            s_ref[j, g * nseq + s] = ss * dcol + _dot_tn(kt[rows], v[rows])
        o = o + (outs[0] if nseq == 1 else jnp.concatenate(outs, axis=0))
        on = o * lax.rsqrt(jnp.mean(o * o, axis=-1, keepdims=True) + NORM_EPS) * gn[:, hcols]
        o_ref[g, pl.ds(r, n), hcols] = (on * _silu(xg)).astype(BF16)

    def chunk(ci, carry):
        r = pl.multiple_of(ci * n, n)
        _round_robin([tile(j, g, r) for j in range(hs) for g in range(g_n)])
        return carry

    lax.fori_loop(0, chunks, chunk, 0)

    @pl.when(tb == pl.num_programs(2) - 1)
    def _():
        for j in range(hs):
            s_out[0, :, j] = s_ref[j]


def _hgrn(z, lb, gn, state, prev_out, layer, depth, b, t, row_off):
    til = _Tiling(b, t, MIX_ROWS, row_off, A_HEADS, A_DK * A_DV * 4, HGRN_STREAMS)
    sel, msk, nlev = _hgrn_consts(til.c)
    qk_blocks = A_QK // LANES
    in_specs, args = [], []
    for col in range(4):
        for j, g in til.streams():
            in_specs.append(til.zspec(j, g, LANES, col * qk_blocks))
            args.append(z)
    in_specs += [
        til.hspec(1, LANES), til.hspec(1, LANES),
        _const_spec(sel.shape), _const_spec(msk.shape),
    ]
    args += [lb.reshape(1, A_QK), gn.reshape(1, A_WIDTH), jnp.asarray(sel, BF16), jnp.asarray(msk, F32)]
    has_state, has_alias, aliases, s_spec, s_shape = _state_io(
        til, state, prev_out, layer, depth, b, A_HEADS, 1, A_DK, A_DV, in_specs, args)
    o, s = pl.pallas_call(
        functools.partial(_hgrn_kernel, g_n=til.g, hs=til.hs, chunks=til.chunks, nseq=til.nseq, c=til.c,
                          nlev=nlev, has_state=has_state, has_alias=has_alias),
        grid=(til.steps, til.head_steps, til.nt),
        in_specs=in_specs,
        out_specs=[til.ospec(LANES), s_spec],
        out_shape=[til.oshape(A_WIDTH), s_shape],
        scratch_shapes=[pltpu.VMEM((til.hs, til.g * til.nseq, A_DK, A_DV), F32)],
        input_output_aliases=aliases,
        compiler_params=_cparams(("parallel", "parallel", "arbitrary")),
        name="hgrn2",
    )(*args)
    return o.reshape(b * t, A_WIDTH), s


def _ret_tables(c):
    n = MIX_ROWS
    log_g = jnp.log1p(-jnp.exp2(-5.0 - jnp.arange(C_HEADS, dtype=F32)))
    t = np.arange(n)
    tt = (t % c).astype(np.float32)
    blk = t // c
    rel = tt[:, None] - tt[None, :]
    same = (blk[:, None] == blk[None, :]) & (rel >= 0)
    dmat = jnp.where(same[None], jnp.exp(log_g[:, None, None] * np.maximum(rel, 0.0)[None]), 0.0)
    inner = jnp.exp(log_g[:, None] * (tt[None, :] + 1.0))
    tail = jnp.exp(log_g[:, None] * (c - 1.0 - tt[None, :]))
    total = jnp.exp(log_g * c)
    shape = (C_HEADS, n, n)
    tab = jnp.stack([dmat, jnp.broadcast_to(inner[:, :, None], shape),
                     jnp.broadcast_to(tail[:, :, None], shape)], axis=1)
    tot = jnp.broadcast_to(total[:, None, None], (C_HEADS, 1, C_DV))
    return tab.astype(F32), tot.astype(F32)


def _rope_tables(t0, t, reps):
    half = C_DK // 2
    inv = ROPE_BASE ** (-jnp.arange(half, dtype=F32) / half)
    pos = t0 + jnp.arange(t, dtype=F32)
    ang = pos[:, None] * inv[None, :]
    cos, sin = jnp.cos(ang), jnp.sin(ang)
    cosf = jnp.concatenate([cos, cos], axis=-1)
    sinf = jnp.concatenate([-sin, sin], axis=-1)
    return jnp.tile(cosf, (reps, 1)), jnp.tile(sinf, (reps, 1))


def _ret_kernel(*refs, g_n, hs, chunks, nseq, c, has_state, has_alias):
    it = iter(refs)
    zq, zk, zv, zg = ([[next(it) for _ in range(g_n)] for _ in range(hs)] for _ in range(4))
    cos, sin, tab, tot = (next(it) for _ in range(4))
    s0 = next(it) if has_state else None
    if has_alias:
        next(it)
    o_ref, s_out, s_ref = next(it), next(it), next(it)
    n = MIX_ROWS
    half = C_DK // 2
    tb = pl.program_id(2)

    @pl.when(tb == 0)
    def _():
        if has_state:
            for j in range(hs):
                s_ref[j] = s0[0, :, j]
        else:
            s_ref[...] = jnp.zeros_like(s_ref)

    def tile(j, g, r, cs, sn):
        xq = zq[j][g][pl.ds(r, n), :]
        xk = zk[j][g][pl.ds(r, n), :]
        v = zv[j][g][pl.ds(r, n), :]
        xg = zg[j][g][pl.ds(r, n), :]
        q = xq * cs + pltpu.roll(xq, half, axis=1) * sn
        k = (xk * cs + pltpu.roll(xk, half, axis=1) * sn) * (C_DK ** -0.5)
        scores = _dot_nt(q, k) * tab[j, 0]
        yield
        o = _dot(scores, v)
        yield
        qi = q * tab[j, 1]
        ktl = k * tab[j, 2]
        outs = []
        for s in range(nseq):
            rows = slice(s * c, (s + 1) * c)
            ss = s_ref[j, g * nseq + s]
            outs.append(_dot(qi[rows], ss))
            s_ref[j, g * nseq + s] = tot[j] * ss + _dot_tn(ktl[rows], v[rows])
        o = o + (outs[0] if nseq == 1 else jnp.concatenate(outs, axis=0))
        on = o * lax.rsqrt(jnp.mean(o * o, axis=-1, keepdims=True) + NORM_EPS)
        o_ref[g, pl.ds(r, n), j * C_DV:(j + 1) * C_DV] = (on * _silu(xg)).astype(BF16)

    def chunk(ci, carry):
        r = pl.multiple_of(ci * n, n)
        cs = cos[pl.ds(r, n), :]
        sn = sin[pl.ds(r, n), :]
        _round_robin([tile(j, g, r, cs, sn) for j in range(hs) for g in range(g_n)])
        return carry

    lax.fori_loop(0, chunks, chunk, 0)

    @pl.when(tb == pl.num_programs(2) - 1)
    def _():
        for j in range(hs):
            s_out[0, :, j] = s_ref[j]


def _retention(z, state, prev_out, layer, depth, b, t, t0, row_off):
    til = _Tiling(b, t, MIX_ROWS, row_off, C_HEADS, C_DK * C_DV * 4, RET_STREAMS)
    tab, tot = _ret_tables(til.c)
    cosf, sinf = _rope_tables(t0, t, max(MIX_ROWS // t, 1))
    qb = C_OFF // C_DK
    vb = (C_OFF + 2 * C_QK) // C_DV
    gb = (C_OFF + 2 * C_QK + C_WIDTH) // C_DV
    in_specs, args = [], []
    for width, col in ((C_DK, qb), (C_DK, qb + C_HEADS), (C_DV, vb), (C_DV, gb)):
        for j, g in til.streams():
            in_specs.append(til.zspec(j, g, width, col))
            args.append(z)
    in_specs += [
        pl.BlockSpec((til.blk, C_DK), lambda i, h, tb: (tb, 0)),
        pl.BlockSpec((til.blk, C_DK), lambda i, h, tb: (tb, 0)),
        pl.BlockSpec((til.hs, 3, MIX_ROWS, MIX_ROWS), lambda i, h, tb: (h, 0, 0, 0)),
        pl.BlockSpec((til.hs, 1, C_DV), lambda i, h, tb: (h, 0, 0)),
    ]
    args += [cosf, sinf, tab, tot]
    has_state, has_alias, aliases, s_spec, s_shape = _state_io(
        til, state, prev_out, layer, depth, b, C_HEADS, 1, C_DK, C_DV, in_specs, args)
    o, s = pl.pallas_call(
        functools.partial(_ret_kernel, g_n=til.g, hs=til.hs, chunks=til.chunks, nseq=til.nseq, c=til.c,
                          has_state=has_state, has_alias=has_alias),
        grid=(til.steps, til.head_steps, til.nt),
        in_specs=in_specs,
        out_specs=[til.ospec(C_DV), s_spec],
        out_shape=[til.oshape(C_WIDTH), s_shape],
        scratch_shapes=[pltpu.VMEM((til.hs, til.g * til.nseq, C_DK, C_DV), F32)],
        input_output_aliases=aliases,
        compiler_params=_cparams(("parallel", "parallel", "arbitrary")),
        name="retention",
    )(*args)
    return o.reshape(b * t, C_WIDTH), s


@functools.lru_cache(maxsize=None)
def _rwkv_consts(c):
    w = RWKV_ROWS
    t = np.arange(w)
    blk = t // c
    same = blk[:, None] == blk[None, :]
    tri = same & (t[None, :] <= t[:, None])
    cum = np.tile(np.concatenate([tri, same], 0).astype(np.float32), (1, 2))
    strict = same & (t[None, :] < t[:, None])
    masks = np.stack([np.tile(strict, (1, 2)), np.tile(tri, (1, 2)),
                      np.tile(np.eye(w, dtype=bool), (1, 2))]).astype(np.float32)
    rr = np.arange(2 * w)
    hh = rr // B_HEAD
    gmat = np.tile((hh[:, None] == hh[None, :]).astype(np.float32), (2, 1))
    return cum, masks, gmat


def _rwkv_kernel(*refs, g_n, hs, chunks, nseq, c, has_state, has_alias):
    it = iter(refs)
    zr, zk, zv = ([[next(it) for _ in range(g_n)] for _ in range(hs)] for _ in range(3))
    zl = [next(it) for _ in range(g_n)]
    (mu_r, mu_k, mu_v, mu_l, w0, a0, kkp, kap, rkp, gnw, gnb,
     w2, a2, g2, cum, msk, gmat) = (next(it) for _ in range(17))
    if has_state:
        sh_r, sh_k, sh_v, sh_l, s0 = (next(it) for _ in range(5))
    if has_alias:
        next(it)
    o_ref, s_out = next(it), next(it)
    shift_outs = [next(it) for _ in range(4)]
    s_ref = next(it)
    carries = None if has_state else [next(it) for _ in range(4)]
    w = RWKV_ROWS
    n = 2 * w
    blk = chunks * w
    nsq = int(math.log2(c)) - 1
    tb = pl.program_id(2)
    lane = lax.broadcasted_iota(jnp.int32, (w, LANES), 1)
    head0 = lane < B_HEAD
    row = lax.broadcasted_iota(jnp.int32, (w, 1), 0)
    first = (row % c) == 0
    gm = gmat[...]
    gm_f = gm[0:2 * w].astype(F32)

    @pl.when(tb == 0)
    def _():
        if has_state:
            zero_blk = jnp.zeros((B_HEAD, B_HEAD), F32)
            for j in range(hs):
                for s in range(g_n * nseq):
                    top = jnp.concatenate([s0[0, s, 2 * j], zero_blk], axis=1)
                    bot = jnp.concatenate([zero_blk, s0[0, s, 2 * j + 1]], axis=1)
                    s_ref[j, s] = jnp.concatenate([top, bot], axis=0)
        else:
            s_ref[...] = jnp.zeros_like(s_ref)
            for cr in carries:
                cr[...] = jnp.zeros_like(cr)

    def stack(x):
        return jnp.concatenate([jnp.where(head0, x, 0.0), jnp.where(head0, 0.0, x)], axis=0)

    def gsum(x):
        return _dot_sel(x, gm)

    def shifted(ref, sh, cols, carry, mu, g, ci, r):
        x = ref[pl.ds(r, w), :]
        width = x.shape[1]
        if has_state:
            src = jnp.concatenate(
                [jnp.broadcast_to(sh[g * nseq + s:g * nseq + s + 1, cols], (c, width))
                 for s in range(nseq)], axis=0)
        else:
            rp = pl.multiple_of(jnp.maximum(r - SUBLANES, 0), SUBLANES)
            prev8 = jnp.where(ci == 0, carry, ref[pl.ds(rp, SUBLANES), :])
            src = jnp.broadcast_to(prev8[SUBLANES - 1:SUBLANES, :], (w, width))
        prev = jnp.where(first, src, pltpu.roll(x, 1, axis=0))
        return x + mu * (prev - x)

    def tile(j, g, ci, r):
        hc = slice(j * LANES, (j + 1) * LANES)
        lc = slice(0, B_LORA_W + B_LORA_A + B_LORA_G)
        cr = [None] * 4 if carries is None else [carries[0][j, g], carries[1][j, g], carries[2][j, g],
                                                  carries[3][g]]
        xr = shifted(zr[j][g], sh_r if has_state else None, hc, cr[0], mu_r[:, hc], g, ci, r)
        xk = shifted(zk[j][g], sh_k if has_state else None, hc, cr[1], mu_k[:, hc], g, ci, r)
        xv = shifted(zv[j][g], sh_v if has_state else None, hc, cr[2], mu_v[:, hc], g, ci, r)
        xl = shifted(zl[g], sh_l if has_state else None, lc, cr[3], mu_l[...], g, ci, r)
        wd = xl[:, 0:B_LORA_W]
        ad = xl[:, B_LORA_W:B_LORA_W + B_LORA_A]
        gd = xl[:, B_LORA_W + B_LORA_A:]
        wx = -(w0[:, hc] + _dot(jnp.tanh(wd), w2[:, hc]))
        w_raw = -(jnp.maximum(wx, 0.0) + jnp.log(1.0 + jnp.exp(-jnp.abs(wx)))) - 0.5
        lw = -jnp.exp(w_raw)
        aa = _sigmoid(a0[:, hc] + _dot(ad, a2[:, hc]))
        gb = _dot(_sigmoid(gd), g2[:, hc])
        yield
        kk = xk * kkp[:, hc]
        kk = kk * jnp.minimum(lax.rsqrt(gsum(kk * kk)), 1e12)
        k2 = xk * (1.0 + (aa - 1.0) * kap[:, hc])
        a = -kk
        b = kk * aa
        yield
        e = _sel_dot(cum[...], lw)
        yield
        lwc = e[0:w]
        lwl = e[w:n]
        dec_in = jnp.exp(lwc)
        dec_ex = jnp.exp(lwc - lw)
        inv = jnp.exp(-lwc)
        rest = jnp.exp(lwl - lwc)
        a_t = a * dec_ex
        r_t = xr * dec_in
        b_t = b * inv
        k_t = k2 * inv
        gram = _dot_nt(jnp.concatenate([a_t, r_t], axis=0),
                       jnp.concatenate([stack(b_t), stack(k_t)], axis=0))
        yield
        m_ab = gram[0:w, 0:n] * msk[0]
        m_ak = gram[0:w, n:2 * n] * msk[0]
        m_rb = gram[w:n, 0:n] * msk[1]
        m_rk = gram[w:n, n:2 * n] * msk[1]
        p = _dot(m_ab, stack(m_ab))
        tinv = msk[2] + m_ab
        yield
        for lev in range(nsq):
            if lev + 1 < nsq:
                both = _dot(p, jnp.concatenate([stack(p), stack(tinv)], axis=1))
                p = both[:, 0:n]
                tinv = tinv + both[:, n:2 * n]
            else:
                tinv = tinv + _dot(p, stack(tinv))
            yield
        p0a, p0r = [], []
        for s in range(nseq):
            rows = slice(s * c, (s + 1) * c)
            pr = _dot_nt(jnp.concatenate([a_t[rows], r_t[rows]], axis=0), s_ref[j, g * nseq + s])
            p0a.append(pr[0:c])
            p0r.append(pr[c:2 * c])
        p0a = p0a[0] if nseq == 1 else jnp.concatenate(p0a, axis=0)
        p0r = p0r[0] if nseq == 1 else jnp.concatenate(p0r, axis=0)
        yield
        vs = stack(xv)
        rhs = p0a + _dot(m_ak, vs)
        yield
        u_w = _dot(tinv, stack(rhs))
        yield
        y = p0r + _dot(m_rb, stack(u_w)) + _dot(m_rk, vs)
        yield
        b_g = b * rest
        k_g = k2 * rest
        dec_l = jnp.exp(lwl)
        for s in range(nseq):
            rows = slice(s * c, (s + 1) * c)
            upd = _dot_tn(jnp.concatenate([u_w[rows], xv[rows]], axis=0),
                          jnp.concatenate([b_g[rows], k_g[rows]], axis=0))
            s_ref[j, g * nseq + s] = s_ref[j, g * nseq + s] * dec_l[s * c:s * c + 1, :] + gm_f * upd
        yield
        mean = gsum(y) * (1.0 / B_HEAD)
        yield
        d = y - mean
        var = gsum(d * d) * (1.0 / B_HEAD)
        yn = d * lax.rsqrt(var + RWKV_GN_EPS) * gnw[:, hc] + gnb[:, hc]
        yield
        bonus = gsum(xr * k2 * rkp[:, hc])
        o_ref[g, pl.ds(r, w), hc] = ((yn + bonus * xv) * gb).astype(BF16)

    def chunk(ci, carry):
        r = pl.multiple_of(ci * w, w)
        _round_robin([tile(j, g, ci, r) for j in range(hs) for g in range(g_n)])
        return carry

    lax.fori_loop(0, chunks, chunk, 0)

    base = blk - w
    for g in range(g_n):
        per_head = [(zr[j][g], shift_outs[0], j) for j in range(hs)]
        per_head += [(zk[j][g], shift_outs[1], j) for j in range(hs)]
        per_head += [(zv[j][g], shift_outs[2], j) for j in range(hs)]
        for ref, out, j in per_head + [(zl[g], shift_outs[3], 0)]:
            cols = slice(j * LANES, j * LANES + ref.shape[1])
            for s in range(nseq):
                last = base + (s + 1) * c - 1
                out[g * nseq + s:g * nseq + s + 1, cols] = ref[last:last + 1, :]
        if carries is not None:
            for j in range(hs):
                carries[0][j, g] = zr[j][g][blk - SUBLANES:blk, :]
                carries[1][j, g] = zk[j][g][blk - SUBLANES:blk, :]
                carries[2][j, g] = zv[j][g][blk - SUBLANES:blk, :]
            carries[3][g] = zl[g][blk - SUBLANES:blk, :]

    @pl.when(tb == pl.num_programs(2) - 1)
    def _():
        for j in range(hs):
            for s in range(g_n * nseq):
                ss = s_ref[j, s]
                s_out[0, s, 2 * j] = ss[0:B_HEAD, 0:B_HEAD]
                s_out[0, s, 2 * j + 1] = ss[B_HEAD:n, B_HEAD:n]


def _rwkv(z, p, state, shift, prev_out, layer, depth, b, t, row_off):
    pairs = B_HEADS // 2
    til = _Tiling(b, t, RWKV_ROWS, row_off, pairs, 2 * B_HEAD * B_HEAD * 4, RWKV_STREAMS)
    cum, msk, gmat = _rwkv_consts(til.c)
    cb = B_OFF // LANES
    wb = B_WIDTH // LANES
    lora_w = B_LORA_W + B_LORA_A + B_LORA_G
    lb_z = (B_OFF + 3 * B_WIDTH) // lora_w
    lb_s = (3 * B_WIDTH) // lora_w

    def vec(x):
        return x.reshape(1, -1)

    def pspec(rows_, col_off=0):
        return til.hspec(rows_, LANES, col_off)

    in_specs, args = [], []
    for col in (cb, cb + wb, cb + 2 * wb):
        for j, g in til.streams():
            in_specs.append(til.zspec(j, g, LANES, col))
            args.append(z)
    for g in range(til.g):
        in_specs.append(til.zspec(0, g, lora_w, lb_z, per_head=0))
        args.append(z)
    in_specs += [
        pspec(1), pspec(1, wb), pspec(1, 2 * wb),
        pl.BlockSpec((1, lora_w), lambda i, h, tb: (0, lb_s)),
        pspec(1), pspec(1), pspec(1), pspec(1), pspec(1), pspec(1), pspec(1),
        pspec(B_LORA_W), pspec(B_LORA_A), pspec(B_LORA_G),
        _const_spec(cum.shape), _const_spec(msk.shape), _const_spec(gmat.shape),
    ]
    mu = vec(p['rwkv_mu'])
    args += [mu, mu, mu, mu,
             vec(p['rwkv_w0']), vec(p['rwkv_a0']), vec(p['rwkv_kk']), vec(p['rwkv_ka']),
             vec(p['rwkv_rk']), vec(p['rwkv_gn_w']), vec(p['rwkv_gn_b']),
             p['rwkv_w2'].astype(BF16), p['rwkv_a2'].astype(BF16), p['rwkv_g2'].astype(BF16),
             jnp.asarray(cum, BF16), jnp.asarray(msk, F32), jnp.asarray(gmat, BF16)]
    ns = til.g * til.nseq
    if state is not None:
        hw = til.hs * LANES
        wbh = wb // til.hs
        in_specs += [
            pl.BlockSpec((ns, hw), lambda i, h, tb: (i, h)),
            pl.BlockSpec((ns, hw), lambda i, h, tb: (i, wbh + h)),
            pl.BlockSpec((ns, hw), lambda i, h, tb: (i, 2 * wbh + h)),
            pl.BlockSpec((ns, lora_w), lambda i, h, tb: (i, lb_s)),
        ]
        args += [shift, shift, shift, shift]
    has_state, has_alias, aliases, s_spec, s_shape = _state_io(
        til, state, prev_out, layer, depth, b, B_HEADS, 2, B_HEAD, B_HEAD, in_specs, args)
    scratch = [pltpu.VMEM((til.hs, til.g * til.nseq, 2 * B_HEAD, 2 * B_HEAD), F32)]
    if not has_state:
        scratch += [pltpu.VMEM((til.hs, til.g, SUBLANES, LANES), F32) for _ in range(3)]
        scratch += [pltpu.VMEM((til.g, SUBLANES, lora_w), F32)]
    o, s, sh_r, sh_k, sh_v, sh_l = pl.pallas_call(
        functools.partial(_rwkv_kernel, g_n=til.g, hs=til.hs, chunks=til.chunks, nseq=til.nseq, c=til.c,
                          has_state=has_state, has_alias=has_alias),
        grid=(til.steps, til.head_steps, til.nt),
        in_specs=in_specs,
        out_specs=[til.ospec(LANES), s_spec,
                   pl.BlockSpec((ns, til.hs * LANES), lambda i, h, tb: (i, h)),
                   pl.BlockSpec((ns, til.hs * LANES), lambda i, h, tb: (i, h)),
                   pl.BlockSpec((ns, til.hs * LANES), lambda i, h, tb: (i, h)),
                   pl.BlockSpec((ns, lora_w), lambda i, h, tb: (i, 0))],
        out_shape=[til.oshape(B_WIDTH), s_shape] + [
            jax.ShapeDtypeStruct((b, wd), F32) for wd in (B_WIDTH, B_WIDTH, B_WIDTH, lora_w)],
        scratch_shapes=scratch,
        input_output_aliases=aliases,
        compiler_params=_cparams(("parallel", "parallel", "arbitrary")),
        name="rwkv7",
    )(*args)
    return o.reshape(b * t, B_WIDTH), s, jnp.concatenate([sh_r, sh_k, sh_v, sh_l], axis=1)


def _merge_kernel(oa, ob, oc, wa, wb, wc, ga, gb, gc, *rest):
    o_ref = rest[-1]
    acc = _sigmoid(ga[...]) * jnp.dot(oa[...], wa[...].astype(BF16), preferred_element_type=F32)
    acc = acc + _sigmoid(gb[...]) * jnp.dot(ob[...], wb[...].astype(BF16), preferred_element_type=F32)
    acc = acc + _sigmoid(gc[...]) * jnp.dot(oc[...], wc[...].astype(BF16), preferred_element_type=F32)
    o_ref[...] = acc.astype(BF16)


def _merge(z, oa, ob, oc, wa, wb, wc, layer, row_off, prev_out):
    m = oa.shape[0]
    tm = _pick(m, (1024, 512, 256, 128))
    tn = 256
    gblk = G_OFF // tn
    nb = D_MODEL // tn
    assert G_OFF % tn == 0 and row_off % tm == 0
    off = row_off // tm

    def ospec():
        return pl.BlockSpec((tm, oa.shape[1]), lambda i, j: (i, 0))

    def wspec():
        return pl.BlockSpec((None, wa.shape[1], tn), lambda i, j: (layer, 0, j))

    def gspec(br):
        return pl.BlockSpec((tm, tn), lambda i, j: (i + off, gblk + br * nb + j))

    in_specs = [ospec(), ospec(), ospec(), wspec(), wspec(), wspec(), gspec(0), gspec(1), gspec(2)]
    args = [oa, ob, oc, wa, wb, wc, z, z, z]
    aliases = {}
    if prev_out is not None:
        aliases = {len(args): 0}
        in_specs.append(pl.BlockSpec(memory_space=pl.ANY))
        args.append(prev_out)
    return pl.pallas_call(
        _merge_kernel,
        grid=(m // tm, nb),
        in_specs=in_specs,
        out_specs=pl.BlockSpec((tm, tn), lambda i, j: (i + off, j)),
        out_shape=jax.ShapeDtypeStruct((z.shape[0], D_MODEL), BF16),
        input_output_aliases=aliases,
        compiler_params=_cparams(("parallel", "arbitrary")),
        name="merge",
    )(*args)


def _proj_res_kernel(m_ref, w_ref, g_ref, x_ref, o_ref):
    y = jnp.dot(m_ref[...], w_ref[...], preferred_element_type=F32)
    yn = y * lax.rsqrt(jnp.mean(y * y, axis=-1, keepdims=True) + NORM_EPS) * g_ref[...]
    o_ref[...] = x_ref[...] + yn


def _proj_res(mrg, w, g, x, layer):
    m = x.shape[0]
    tm = _pick(m, (512, 256, 128))
    return pl.pallas_call(
        _proj_res_kernel,
        grid=(m // tm,),
        in_specs=[
            pl.BlockSpec((tm, D_MODEL), lambda i: (i, 0)),
            pl.BlockSpec((None, D_MODEL, D_MODEL), lambda i: (layer, 0, 0)),
            pl.BlockSpec((1, D_MODEL), lambda i: (0, 0)),
            pl.BlockSpec((tm, D_MODEL), lambda i: (i, 0)),
        ],
        out_specs=pl.BlockSpec((tm, D_MODEL), lambda i: (i, 0)),
        out_shape=jax.ShapeDtypeStruct((m, D_MODEL), F32),
        compiler_params=_cparams(("parallel",)),
        name="proj_res",
    )(mrg, w, g.reshape(1, D_MODEL), x)


def _gelu(x):
    return 0.5 * x * (1.0 + jnp.tanh(math.sqrt(2.0 / math.pi) * (x + 0.044715 * (x * x * x))))


def _up_act_kernel(*refs, tm, t, blocks_per_seq, has_state, has_alias):
    it = iter(refs)
    x_ref, g_ref, wa, wb, cw, cb = (next(it) for _ in range(6))
    st = next(it) if has_state else None
    if has_alias:
        next(it)
    o_ref, nc_ref, xn_ref = next(it), next(it), next(it)
    tail = None if has_state else next(it)
    i = pl.program_id(0)
    j = pl.program_id(1)
    tn = wa.shape[1]
    sub = UP_SUB

    @pl.when(j == 0)
    def _():
        rows = min(tm, LANES)

        def body(r, carry):
            r0 = pl.multiple_of(r * rows, rows)
            x = x_ref[pl.ds(r0, rows), :]
            ms = jnp.mean(x * x, axis=-1, keepdims=True)
            xn_ref[pl.ds(r0, rows), :] = (x * lax.rsqrt(ms + NORM_EPS) * g_ref[...]).astype(BF16)
            return carry
        lax.fori_loop(0, tm // rows, body, 0)

    if has_state:
        ns = tm // t
        tt = lax.broadcasted_iota(jnp.int32, (ns, t, sub), 1)
    else:
        seq_start = (i % blocks_per_seq) == 0
        rr = lax.broadcasted_iota(jnp.int32, (tm, sub), 0)
    for c in range(tn // sub):
        cols = slice(c * sub, (c + 1) * sub)
        ua = jnp.dot(xn_ref[...], wa[:, cols].astype(BF16), preferred_element_type=F32)
        ub = jnp.dot(xn_ref[...], wb[:, cols].astype(BF16), preferred_element_type=F32)
        if has_state:
            x3 = ua.reshape(ns, t, sub)
            s_old = st[:, 0:1, cols]
            s_new = st[:, 1:2, cols]
            prev1 = jnp.where(tt >= 1, pltpu.roll(x3, 1, axis=1), s_new)
            prev2 = jnp.where(tt >= 2, pltpu.roll(x3, 2, axis=1), jnp.where(tt == 1, s_new, s_old))
            prev1 = prev1.reshape(tm, sub)
            prev2 = prev2.reshape(tm, sub)
            nc_ref[:, :, cols] = x3[:, t - (CONV_W - 1):, :]
        else:
            h = jnp.where(seq_start, 0.0, tail[j, :, cols])
            h1 = h[SUBLANES - 1:SUBLANES, :]
            h2 = h[SUBLANES - 2:SUBLANES - 1, :]
            prev1 = jnp.where(rr == 0, h1, pltpu.roll(ua, 1, axis=0))
            prev2 = jnp.where(rr == 0, h2, jnp.where(rr == 1, h1, pltpu.roll(ua, 2, axis=0)))
            tail[j, :, cols] = ua[tm - SUBLANES:, :]
            nc_ref[0, :, cols] = ua[tm - (CONV_W - 1):, :]
        conv = cb[:, cols] + cw[0:1, cols] * prev2 + cw[1:2, cols] * prev1 + cw[2:3, cols] * ua
        o_ref[:, cols] = (_gelu(conv) * ub).astype(BF16)


def _up_act(x1, g, w_up, cw, cb, state, prev_out, layer, b, t, row_off):
    m = b * t
    total = x1.shape[0]
    has_state = state is not None
    has_alias = prev_out is not None
    tn = UP_TN
    nj = D_FF // tn
    if has_state:
        tm = _pick(m, (1024, 512, 256, 128))
        assert tm % t == 0 and t == SUBLANES
        blocks_per_seq = 1
        nc_spec = pl.BlockSpec((tm // t, CONV_W - 1, tn), lambda i, j: (i, 0, j))
    else:
        tm = _pick(t, (1024, 512, 256, 128))
        blocks_per_seq = t // tm
        nc_spec = pl.BlockSpec((1, CONV_W - 1, tn), lambda i, j: (i, 0, j))
    nc_rows = b * blocks_per_seq
    off = row_off // tm
    assert row_off % tm == 0
    in_specs = [
        pl.BlockSpec((tm, D_MODEL), lambda i, j: (i + off, 0)),
        pl.BlockSpec((1, D_MODEL), lambda i, j: (0, 0)),
        pl.BlockSpec((None, D_MODEL, tn), lambda i, j: (layer, 0, j)),
        pl.BlockSpec((None, D_MODEL, tn), lambda i, j: (layer, 0, nj + j)),
        pl.BlockSpec((CONV_W, tn), lambda i, j: (0, j)),
        pl.BlockSpec((1, tn), lambda i, j: (0, j)),
    ]
    args = [x1, g.reshape(1, D_MODEL), w_up, w_up, cw, cb.reshape(1, D_FF)]
    if has_state:
        in_specs.append(pl.BlockSpec((tm // t, CONV_W - 1, tn), lambda i, j: (i, 0, j)))
        args.append(state)
    aliases = {}
    if has_alias:
        aliases = {len(args): 0}
        in_specs.append(pl.BlockSpec(memory_space=pl.ANY))
        args.append(prev_out)
    scratch = [pltpu.VMEM((tm, D_MODEL), BF16)]
    if not has_state:
        scratch.append(pltpu.VMEM((nj, SUBLANES, tn), F32))
    act, nc = pl.pallas_call(
        functools.partial(_up_act_kernel, tm=tm, t=t, blocks_per_seq=blocks_per_seq,
                          has_state=has_state, has_alias=has_alias),
        grid=(m // tm, nj),
        in_specs=in_specs,
        out_specs=[pl.BlockSpec((tm, tn), lambda i, j: (i + off, j)), nc_spec],
        out_shape=[jax.ShapeDtypeStruct((total, D_FF), BF16),
                   jax.ShapeDtypeStruct((nc_rows, CONV_W - 1, D_FF), F32)],
        scratch_shapes=scratch,
        input_output_aliases=aliases,
        compiler_params=_cparams(("arbitrary", "arbitrary")),
        name="up_act",
    )(*args)
    return act, nc[blocks_per_seq - 1::blocks_per_seq]


def _down_res_kernel(a_ref, w_ref, g_ref, x_ref, o_ref, acc):
    kstep = pl.program_id(1)

    @pl.when(kstep == 0)
    def _():
        acc[...] = jnp.zeros_like(acc)

    acc[...] += jnp.dot(a_ref[...], w_ref[...], preferred_element_type=F32)

    @pl.when(kstep == pl.num_programs(1) - 1)
    def _():
        y = acc[...]
        yn = y * lax.rsqrt(jnp.mean(y * y, axis=-1, keepdims=True) + NORM_EPS) * g_ref[...]
        o_ref[...] = x_ref[...] + yn


def _down_res(act, wd, g, x1, layer, row_off, rows):
    tm = _pick(rows, (512, 256, 128))
    tk = DOWN_TK
    off = row_off // tm
    assert row_off % tm == 0
    return pl.pallas_call(
        _down_res_kernel,
        grid=(rows // tm, D_FF // tk),
        in_specs=[
            pl.BlockSpec((tm, tk), lambda i, k: (i + off, k)),
            pl.BlockSpec((None, tk, D_MODEL), lambda i, k: (layer, k, 0)),
            pl.BlockSpec((1, D_MODEL), lambda i, k: (0, 0)),
            pl.BlockSpec((tm, D_MODEL), lambda i, k: (i + off, 0)),
        ],
        out_specs=pl.BlockSpec((tm, D_MODEL), lambda i, k: (i, 0)),
        out_shape=jax.ShapeDtypeStruct((rows, D_MODEL), F32),
        scratch_shapes=[pltpu.VMEM((tm, D_MODEL), F32)],
        compiler_params=_cparams(("parallel", "arbitrary")),
        name="down_res",
    )(act, wd, g.reshape(1, D_MODEL), x1)


def _layer(x, groups, lb, p, layer, depth, prev, split_out):
    z = _rms_matmul(x, p['pre_mix_g'], p['w_in'], layer, tn=1280)
    states = []
    mrg = None
    row = 0
    for gi, (b, t, t0, st) in enumerate(groups):
        s_a, s_b, s_sh, s_c, _ = st if st is not None else (None,) * 5
        pv = prev[gi] if prev is not None else (None,) * 3
        o_a, n_a = _hgrn(z, lb, p['a_norm_g'], s_a, pv[0], layer, depth, b, t, row)
        o_b, n_b, n_sh = _rwkv(z, p, s_b, s_sh, pv[1], layer, depth, b, t, row)
        o_c, n_c = _retention(z, s_c, pv[2], layer, depth, b, t, t0, row)
        mrg = _merge(z, o_a, o_b, o_c, p['w_br_a'], p['w_br_b'], p['w_br_c'], layer, row, mrg)
        states.append([n_a, n_b, n_sh, n_c])
        row += b * t
    x1 = _proj_res(mrg, p['w_out'], p['post_mix_g'], x, layer)
    act = None
    row = 0
    for gi, (b, t, t0, st) in enumerate(groups):
        s_cv = st[4] if st is not None else None
        act, n_cv = _up_act(x1, p['pre_ffn_g'], p['w_up'], p['conv_w'], p['conv_b'], s_cv, act,
                            layer, b, t, row)
        states[gi].append(n_cv)
        row += b * t
    if split_out:
        x2, row = [], 0
        for (b, t, _, _) in groups:
            x2.append(_down_res(act, p['w_down'], p['post_ffn_g'], x1, layer, row, b * t))
            row += b * t
    else:
        x2 = _down_res(act, p['w_down'], p['post_ffn_g'], x1, layer, 0, x1.shape[0])
    return x2, states


def kernel(x_prompt, x_sample, state_hgrn, state_rwkv, state_rwkv_shift, state_ret, state_conv,
           lb_logits, pre_mix_g, w_in, a_norm_g, rwkv_mu, rwkv_w0, rwkv_w2, rwkv_a0, rwkv_a2,
           rwkv_g2, rwkv_kk, rwkv_ka, rwkv_rk, rwkv_gn_w, rwkv_gn_b, w_br_a, w_br_b, w_br_c,
           w_out, post_mix_g, pre_ffn_g, w_up, conv_w, conv_b, w_down, post_ffn_g):
    depth = w_in.shape[0]
    bp, tp, _ = x_prompt.shape
    bs, ts, _ = x_sample.shape
    past_len = 16384
    lb_soft = jax.nn.softmax(lb_logits.astype(F32), axis=0)
    lbs = jnp.cumsum(lb_soft, axis=0) - lb_soft[0]
    big = {'w_in': w_in, 'w_br_a': w_br_a, 'w_br_b': w_br_b,
           'w_br_c': w_br_c, 'w_out': w_out.astype(BF16), 'w_up': w_up,
           'w_down': w_down.astype(BF16)}
    x = jnp.concatenate([x_prompt.reshape(bp * tp, D_MODEL), x_sample.reshape(bs * ts, D_MODEL)], axis=0)
    small = [[[], []], [[], []]]
    prev = None
    for l in range(depth):
        p = dict(big)
        p.update({
            'pre_mix_g': pre_mix_g[l], 'a_norm_g': a_norm_g[l],
            'rwkv_mu': rwkv_mu[l], 'rwkv_w0': rwkv_w0[l], 'rwkv_w2': rwkv_w2[l],
            'rwkv_a0': rwkv_a0[l], 'rwkv_a2': rwkv_a2[l], 'rwkv_g2': rwkv_g2[l],
            'rwkv_kk': rwkv_kk[l], 'rwkv_ka': rwkv_ka[l], 'rwkv_rk': rwkv_rk[l],
            'rwkv_gn_w': rwkv_gn_w[l], 'rwkv_gn_b': rwkv_gn_b[l],
            'post_mix_g': post_mix_g[l], 'pre_ffn_g': pre_ffn_g[l],
            'conv_w': conv_w[l], 'conv_b': conv_b[l], 'post_ffn_g': post_ffn_g[l],
        })
        groups = [
            (bp, tp, 0, None),
            (bs, ts, past_len, (state_hgrn, state_rwkv, state_rwkv_shift[l], state_ret, state_conv[l])),
        ]
        x, states = _layer(x, groups, lbs[l], p, l, depth, prev, l == depth - 1)
        prev = [(st[0], st[1], st[3]) for st in states]
        for gi, st in enumerate(states):
            small[gi][0].append(st[2])
            small[gi][1].append(st[4])
    y_p = x[0].reshape(bp, tp, D_MODEL)
    y_s = x[1].reshape(bs, ts, D_MODEL)
    outs = []
    for gi in range(2):
        outs += [prev[gi][0], prev[gi][1], jnp.stack(small[gi][0]), prev[gi][2], jnp.stack(small[gi][1])]
    return (y_p, y_s, *outs)
```

```python
import functools
import itertools
import math

import jax
import jax.numpy as jnp
import numpy as np
from jax import lax
from jax.experimental import pallas as pl
from jax.experimental.pallas import tpu as pltpu

F32 = jnp.float32
BF16 = jnp.bfloat16

D_MODEL = 2048
A_HEADS, A_DK, A_DV = 8, 128, 128
A_QK = A_HEADS * A_DK
A_WIDTH = A_HEADS * A_DV
F_TINY = 1e-30
B_HEAD = 64
B_WIDTH = 1024
B_HEADS = B_WIDTH // B_HEAD
B_LORA_W, B_LORA_A, B_LORA_G = 64, 64, 128
RWKV_GN_EPS = 64e-5
C_HEADS, C_DK, C_DV = 4, 128, 256
C_QK = C_HEADS * C_DK
C_WIDTH = C_HEADS * C_DV
ROPE_BASE = 10000.0
A_COLS = 2 * A_QK + 2 * A_WIDTH
B_COLS = 3 * B_WIDTH + B_LORA_W + B_LORA_A + B_LORA_G
C_COLS = 2 * C_QK + 2 * C_WIDTH
N_BRANCH = 3
P_COLS = A_COLS + B_COLS + C_COLS + N_BRANCH * D_MODEL
B_OFF = A_COLS
C_OFF = A_COLS + B_COLS
G_OFF = A_COLS + B_COLS + C_COLS
D_FF = 5632
CONV_W = 3
NORM_EPS = 1e-6

LANES = 128
SUBLANES = 8
MXU_COLS = 256
MIX_ROWS = 128
RWKV_ROWS = 64
MIX_STREAMS = 4
HGRN_STREAMS = 16
HGRN_TIME_BLOCK = 256
HGRN_MATMUL_LEVEL_ROWS = 8
RET_STREAMS = 16
RET_TIME_BLOCK = 256
RWKV_STREAMS = 16
RWKV_TIME_BLOCK = 512
STATE_WINDOW_BYTES = 8 * 1024 * 1024
UP_TN = 512
UP_SUB = 256
DOWN_TK = 2816
VMEM_LIMIT = 56 * 1024 * 1024


def _cparams(sem):
    return pltpu.CompilerParams(dimension_semantics=sem, vmem_limit_bytes=VMEM_LIMIT)


def _dot(a, b):
    return jnp.dot(a.astype(BF16), b.astype(BF16), preferred_element_type=F32)


def _dot_nt(a, b):
    return lax.dot_general(a.astype(BF16), b.astype(BF16), (((1,), (1,)), ((), ())),
                           preferred_element_type=F32)


def _dot_tn(a, b):
    return lax.dot_general(a.astype(BF16), b.astype(BF16), (((0,), (0,)), ((), ())),
                           preferred_element_type=F32)


def _split(x):
    hi = x.astype(BF16)
    lo = (x - hi.astype(F32)).astype(BF16)
    return hi, lo


def _sel_dot(m2, x):
    hi, lo = _split(x)
    return jnp.dot(m2, jnp.concatenate([hi, lo], axis=0), preferred_element_type=F32)


def _dot_sel(x, m2):
    hi, lo = _split(x)
    return jnp.dot(jnp.concatenate([hi, lo], axis=1), m2, preferred_element_type=F32)


def _sigmoid(x):
    return 0.5 * jnp.tanh(0.5 * x) + 0.5


def _silu(x):
    h = 0.5 * x
    return h * jnp.tanh(h) + h


def _round_robin(gens):
    for _ in itertools.zip_longest(*gens):
        pass


def _pick(n, cands):
    for c in cands:
        if n % c == 0:
            return c
    raise ValueError(f"no tile in {cands} divides {n}")


def _rms_matmul_kernel(x_ref, g_ref, w_ref, o_ref, xn_ref, *, tm, sub):
    @pl.when(pl.program_id(1) == 0)
    def _():
        def body(i, carry):
            r = pl.multiple_of(i * sub, sub)
            x = x_ref[pl.ds(r, sub), :]
            ms = jnp.mean(x * x, axis=-1, keepdims=True)
            xn_ref[pl.ds(r, sub), :] = (x * lax.rsqrt(ms + NORM_EPS) * g_ref[...]).astype(BF16)
            return carry
        lax.fori_loop(0, tm // sub, body, 0)

    tn = w_ref.shape[1]
    for c in range(tn // MXU_COLS):
        cols = slice(c * MXU_COLS, (c + 1) * MXU_COLS)
        o_ref[:, cols] = jnp.dot(xn_ref[...], w_ref[:, cols].astype(BF16), preferred_element_type=F32)


def _rms_matmul(x, g, w, layer, tn):
    m, k = x.shape
    n = w.shape[2]
    tm = _pick(m, (1024, 512, 256, 128))
    sub = min(tm, 128)
    assert tn % MXU_COLS == 0
    return pl.pallas_call(
        functools.partial(_rms_matmul_kernel, tm=tm, sub=sub),
        grid=(m // tm, n // tn),
        in_specs=[
            pl.BlockSpec((tm, k), lambda i, j: (i, 0), pipeline_mode=pl.Buffered(1)),
            pl.BlockSpec((1, k), lambda i, j: (0, 0)),
            pl.BlockSpec((None, k, tn), lambda i, j: (layer, 0, j)),
        ],
        out_specs=pl.BlockSpec((tm, tn), lambda i, j: (i, j)),
        out_shape=jax.ShapeDtypeStruct((m, n), F32),
        scratch_shapes=[pltpu.VMEM((tm, k), BF16)],
        compiler_params=_cparams(("parallel", "arbitrary")),
        name="rms_matmul",
    )(x, g.reshape(1, k), w)


class _Tiling:
    def __init__(self, b, t, rows, row_off, heads, state_bytes, streams_total, time_block):
        if t >= rows:
            self.nseq, self.c = 1, rows
            self.blk = _pick(t, (time_block, rows))
            self.nt = t // self.blk
            n_streams = b
        else:
            assert rows % t == 0 and b % (rows // t) == 0
            self.nseq, self.c = rows // t, t
            self.blk, self.nt = rows, 1
            n_streams = b // self.nseq
        self.g = _pick(n_streams, (MIX_STREAMS, 2, 1))
        self.hs = _pick(heads, (max(streams_total // self.g, 1), 2, 1))
        while self.g * self.nseq * self.hs * state_bytes > STATE_WINDOW_BYTES and self.hs > 1:
            self.hs //= 2
        while self.g * self.nseq * self.hs * state_bytes > STATE_WINDOW_BYTES and self.g > 1:
            self.g //= 2
        self.head_steps = heads // self.hs
        self.steps = n_streams // self.g
        self.n_streams = n_streams
        self.chunks = self.blk // rows
        assert row_off % self.blk == 0
        self.off = row_off // self.blk

    def streams(self):
        return [(j, g) for j in range(self.hs) for g in range(self.g)]

    def zspec(self, j, stream, width, col_blk, per_head=1):
        g, nt, off, hs = self.g, self.nt, self.off, self.hs
        return pl.BlockSpec(
            (self.blk, width),
            lambda i, h, tb: (off + (i * g + stream) * nt + tb, col_blk + per_head * (h * hs + j)))

    def hspec(self, rows, width, blk_off=0):
        assert blk_off % self.hs == 0
        off = blk_off // self.hs
        return pl.BlockSpec((rows, self.hs * width), lambda i, h, tb: (0, off + h))

    def ospec(self, width):
        return pl.BlockSpec((self.g, self.blk, self.hs * width), lambda i, h, tb: (i, tb, h))

    def oshape(self, width):
        return jax.ShapeDtypeStruct((self.n_streams, self.nt * self.blk, width), BF16)

    def sspec(self, layer, heads_per_step, d0, d1):
        return pl.BlockSpec((1, self.g * self.nseq, self.hs * heads_per_step, d0, d1),
                            lambda i, h, tb: (layer, i, h, 0, 0))


def _const_spec(shape):
    nd = len(shape)
    return pl.BlockSpec(shape, lambda i, h, tb: (0,) * nd)


def _state_io(til, state, prev_out, layer, depth, b, heads, heads_per_step, d0, d1,
              in_specs, args):
    has_state = state is not None
    if has_state:
        in_specs.append(til.sspec(layer, heads_per_step, d0, d1))
        args.append(state)
    aliases = {}
    if prev_out is not None:
        aliases = {len(args): 1}
        in_specs.append(pl.BlockSpec(memory_space=pl.ANY))
        args.append(prev_out)
    out_spec = til.sspec(layer, heads_per_step, d0, d1)
    out_shape = jax.ShapeDtypeStruct((depth, b, heads, d0, d1), F32)
    return has_state, prev_out is not None, aliases, out_spec, out_shape


@functools.lru_cache(maxsize=None)
def _hgrn_consts(c):
    n = MIX_ROWS
    nlev = int(math.log2(c))
    t = np.arange(n)
    u = np.arange(n)[None, :]
    blk = t // c
    same = blk[:, None] == blk[None, :]
    mats = [same & (u <= t[:, None]), same]
    masks = [np.eye(n, dtype=bool)]
    for lev in range(nlev):
        h = 1 << lev
        base = (t // (2 * h)) * (2 * h)
        mid = base + h
        upper = t >= mid
        e_up = (u >= mid[:, None]) & (u <= t[:, None])
        e_lo = (u >= t[:, None] + 1) & (u <= mid[:, None] - 1)
        if 2 * h <= HGRN_MATMUL_LEVEL_ROWS:
            mats.append(np.where(upper[:, None], e_up, e_lo))
        masks.append((base[:, None] == base[None, :]) & upper[:, None] & (~upper)[None, :])
    sel = np.tile(np.concatenate(mats, 0).astype(np.float32), (1, 2))
    msk = np.stack(masks).astype(np.float32)
    return sel, msk, nlev


def _hgrn_kernel(*refs, g_n, hs, chunks, nseq, c, nlev, has_state, has_alias):
    it = iter(refs)
    zq, zf, zi, zg = ([[next(it) for _ in range(g_n)] for _ in range(hs)] for _ in range(4))
    lb, gn, sel, msk = (next(it) for _ in range(4))
    s0 = next(it) if has_state else None
    if has_alias:
        next(it)
    o_ref, s_out, s_ref = next(it), next(it), next(it)
    n = MIX_ROWS
    tb = pl.program_id(2)

    @pl.when(tb == 0)
    def _():
        if has_state:
            for j in range(hs):
                s_ref[j] = s0[0, :, j]
        else:
            s_ref[...] = jnp.zeros_like(s_ref)

    row_id = lax.broadcasted_iota(jnp.int32, (n, 1), 0)

    def tile(j, g, r):
        hcols = slice(j * LANES, (j + 1) * LANES)
        lbv = lb[:, hcols]
        xq = zq[j][g][pl.ds(r, n), :]
        fa = zf[j][g][pl.ds(r, n), :]
        v = zi[j][g][pl.ds(r, n), :]
        xg = zg[j][g][pl.ds(r, n), :]
        q = _silu(xq)
        th = jnp.tanh(0.5 * fa)
        f_gate = lbv + (1.0 - lbv) * (0.5 + 0.5 * th)
        gl = jnp.log(jnp.maximum(f_gate, F_TINY))
        k = (1.0 - lbv) * (0.5 - 0.5 * th)
        e = _sel_dot(sel[...], gl)
        yield
        b = e[0:n]
        bl = e[n:2 * n]
        scores = msk[0] * _dot_nt(q, k)
        for lev in range(nlev):
            size = 2 << lev
            if size <= HGRN_MATMUL_LEVEL_ROWS:
                ex = e[(lev + 2) * n:(lev + 3) * n]
            else:
                ref = jnp.concatenate(
                    [jnp.broadcast_to(b[s0 + size // 2 - 1:s0 + size // 2, :], (size, A_DK))
                     for s0 in range(0, n, size)], axis=0)
                upper = (row_id % size) >= size // 2
                ex = jnp.where(upper, b - ref, ref - b)
            x = jnp.exp(ex)
            scores = scores + msk[lev + 1] * _dot_nt(q * x, k * x)
        yield
        o = _dot(scores, v)
        yield
        qe = q * jnp.exp(b)
        kt = k * jnp.exp(bl - b)
        dt = jnp.exp(bl).T
        outs = []
        for s in range(nseq):
            rows = slice(s * c, (s + 1) * c)
            ss = s_ref[j, g * nseq + s]
            outs.append(_dot(qe[rows], ss))
            dcol = jnp.broadcast_to(dt[:, s * c:s * c + 1], (A_DK, A_DV))
            s_ref[j, g * nseq + s] = ss * dcol + _dot_tn(kt[rows], v[rows])
        o = o + (outs[0] if nseq == 1 else jnp.concatenate(outs, axis=0))
        on = o * lax.rsqrt(jnp.mean(o * o, axis=-1, keepdims=True) + NORM_EPS) * gn[:, hcols]
        o_ref[g, pl.ds(r, n), hcols] = (on * _silu(xg)).astype(BF16)

    def chunk(ci, carry):
        r = pl.multiple_of(ci * n, n)
        _round_robin([tile(j, g, r) for j in range(hs) for g in range(g_n)])
        return carry

    lax.fori_loop(0, chunks, chunk, 0)

    @pl.when(tb == pl.num_programs(2) - 1)
    def _():
        for j in range(hs):
            s_out[0, :, j] = s_ref[j]


def _hgrn(z, lb, gn, state, prev_out, layer, depth, b, t, row_off):
    til = _Tiling(b, t, MIX_ROWS, row_off, A_HEADS, A_DK * A_DV * 4, HGRN_STREAMS, HGRN_TIME_BLOCK)
    sel, msk, nlev = _hgrn_consts(til.c)
    qk_blocks = A_QK // LANES
    in_specs, args = [], []
    for col in range(4):
        for j, g in til.streams():
            in_specs.append(til.zspec(j, g, LANES, col * qk_blocks))
            args.append(z)
    in_specs += [
        til.hspec(1, LANES), til.hspec(1, LANES),
        _const_spec(sel.shape), _const_spec(msk.shape),
    ]
    args += [lb.reshape(1, A_QK), gn.reshape(1, A_WIDTH), jnp.asarray(sel, BF16), jnp.asarray(msk, F32)]
    has_state, has_alias, aliases, s_spec, s_shape = _state_io(
        til, state, prev_out, layer, depth, b, A_HEADS, 1, A_DK, A_DV, in_specs, args)
    o, s = pl.pallas_call(
        functools.partial(_hgrn_kernel, g_n=til.g, hs=til.hs, chunks=til.chunks, nseq=til.nseq, c=til.c,
                          nlev=nlev, has_state=has_state, has_alias=has_alias),
        grid=(til.steps, til.head_steps, til.nt),
        in_specs=in_specs,
        out_specs=[til.ospec(LANES), s_spec],
        out_shape=[til.oshape(A_WIDTH), s_shape],
        scratch_shapes=[pltpu.VMEM((til.hs, til.g * til.nseq, A_DK, A_DV), F32)],
        input_output_aliases=aliases,
        compiler_params=_cparams(("parallel", "parallel", "arbitrary")),
        name="hgrn2",
    )(*args)
    return o.reshape(b * t, A_WIDTH), s


def _ret_tables(c):
    n = MIX_ROWS
    log_g = jnp.log1p(-jnp.exp2(-5.0 - jnp.arange(C_HEADS, dtype=F32)))
    t = np.arange(n)
    tt = (t % c).astype(np.float32)
    blk = t // c
    rel = tt[:, None] - tt[None, :]
    same = (blk[:, None] == blk[None, :]) & (rel >= 0)
    dmat = jnp.where(same[None], jnp.exp(log_g[:, None, None] * np.maximum(rel, 0.0)[None]), 0.0)
    inner = jnp.exp(log_g[:, None] * (tt[None, :] + 1.0))
    tail = jnp.exp(log_g[:, None] * (c - 1.0 - tt[None, :]))
    total = jnp.exp(log_g * c)
    shape = (C_HEADS, n, n)
    tab = jnp.stack([dmat, jnp.broadcast_to(inner[:, :, None], shape),
                     jnp.broadcast_to(tail[:, :, None], shape)], axis=1)
    tot = jnp.broadcast_to(total[:, None, None], (C_HEADS, 1, C_DV))
    return tab.astype(F32), tot.astype(F32)


def _rope_tables(t0, t, reps):
    half = C_DK // 2
    inv = ROPE_BASE ** (-jnp.arange(half, dtype=F32) / half)
    pos = t0 + jnp.arange(t, dtype=F32)
    ang = pos[:, None] * inv[None, :]
    cos, sin = jnp.cos(ang), jnp.sin(ang)
    cosf = jnp.concatenate([cos, cos], axis=-1)
    sinf = jnp.concatenate([-sin, sin], axis=-1)
    return jnp.tile(cosf, (reps, 1)), jnp.tile(sinf, (reps, 1))


def _ret_kernel(*refs, g_n, hs, chunks, nseq, c, has_state, has_alias):
    it = iter(refs)
    zq, zk, zv, zg = ([[next(it) for _ in range(g_n)] for _ in range(hs)] for _ in range(4))
    cos, sin, tab, tot = (next(it) for _ in range(4))
    s0 = next(it) if has_state else None
    if has_alias:
        next(it)
    o_ref, s_out, s_ref = next(it), next(it), next(it)
    n = MIX_ROWS
    half = C_DK // 2
    tb = pl.program_id(2)

    @pl.when(tb == 0)
    def _():
        if has_state:
            for j in range(hs):
                s_ref[j] = s0[0, :, j]
        else:
            s_ref[...] = jnp.zeros_like(s_ref)

    def tile(j, g, r, cs, sn):
        xq = zq[j][g][pl.ds(r, n), :]
        xk = zk[j][g][pl.ds(r, n), :]
        v = zv[j][g][pl.ds(r, n), :]
        xg = zg[j][g][pl.ds(r, n), :]
        q = xq * cs + pltpu.roll(xq, half, axis=1) * sn
        k = (xk * cs + pltpu.roll(xk, half, axis=1) * sn) * (C_DK ** -0.5)
        scores = _dot_nt(q, k) * tab[j, 0]
        yield
        o = _dot(scores, v)
        yield
        qi = q * tab[j, 1]
        ktl = k * tab[j, 2]
        outs = []
        for s in range(nseq):
            rows = slice(s * c, (s + 1) * c)
            ss = s_ref[j, g * nseq + s]
            outs.append(_dot(qi[rows], ss))
            s_ref[j, g * nseq + s] = tot[j] * ss + _dot_tn(ktl[rows], v[rows])
        o = o + (outs[0] if nseq == 1 else jnp.concatenate(outs, axis=0))
        on = o * lax.rsqrt(jnp.mean(o * o, axis=-1, keepdims=True) + NORM_EPS)
        o_ref[g, pl.ds(r, n), j * C_DV:(j + 1) * C_DV] = (on * _silu(xg)).astype(BF16)

    def chunk(ci, carry):
        r = pl.multiple_of(ci * n, n)
        cs = cos[pl.ds(r, n), :]
        sn = sin[pl.ds(r, n), :]
        _round_robin([tile(j, g, r, cs, sn) for j in range(hs) for g in range(g_n)])
        return carry

    lax.fori_loop(0, chunks, chunk, 0)

    @pl.when(tb == pl.num_programs(2) - 1)
    def _():
        for j in range(hs):
            s_out[0, :, j] = s_ref[j]


def _retention(z, state, prev_out, layer, depth, b, t, t0, row_off):
    til = _Tiling(b, t, MIX_ROWS, row_off, C_HEADS, C_DK * C_DV * 4, RET_STREAMS, RET_TIME_BLOCK)
    tab, tot = _ret_tables(til.c)
    cosf, sinf = _rope_tables(t0, t, max(MIX_ROWS // t, 1))
    qb = C_OFF // C_DK
    vb = (C_OFF + 2 * C_QK) // C_DV
    gb = (C_OFF + 2 * C_QK + C_WIDTH) // C_DV
    in_specs, args = [], []
    for width, col in ((C_DK, qb), (C_DK, qb + C_HEADS), (C_DV, vb), (C_DV, gb)):
        for j, g in til.streams():
            in_specs.append(til.zspec(j, g, width, col))
            args.append(z)
    in_specs += [
        pl.BlockSpec((til.blk, C_DK), lambda i, h, tb: (tb, 0)),
        pl.BlockSpec((til.blk, C_DK), lambda i, h, tb: (tb, 0)),
        pl.BlockSpec((til.hs, 3, MIX_ROWS, MIX_ROWS), lambda i, h, tb: (h, 0, 0, 0)),
        pl.BlockSpec((til.hs, 1, C_DV), lambda i, h, tb: (h, 0, 0)),
    ]
    args += [cosf, sinf, tab, tot]
    has_state, has_alias, aliases, s_spec, s_shape = _state_io(
        til, state, prev_out, layer, depth, b, C_HEADS, 1, C_DK, C_DV, in_specs, args)
    o, s = pl.pallas_call(
        functools.partial(_ret_kernel, g_n=til.g, hs=til.hs, chunks=til.chunks, nseq=til.nseq, c=til.c,
                          has_state=has_state, has_alias=has_alias),
        grid=(til.steps, til.head_steps, til.nt),
        in_specs=in_specs,
        out_specs=[til.ospec(C_DV), s_spec],
        out_shape=[til.oshape(C_WIDTH), s_shape],
        scratch_shapes=[pltpu.VMEM((til.hs, til.g * til.nseq, C_DK, C_DV), F32)],
        input_output_aliases=aliases,
        compiler_params=_cparams(("parallel", "parallel", "arbitrary")),
        name="retention",
    )(*args)
    return o.reshape(b * t, C_WIDTH), s


@functools.lru_cache(maxsize=None)
def _rwkv_consts(c):
    w = RWKV_ROWS
    t = np.arange(w)
    blk = t // c
    same = blk[:, None] == blk[None, :]
    tri = same & (t[None, :] <= t[:, None])
    cum = np.tile(np.concatenate([tri, same], 0).astype(np.float32), (1, 2))
    strict = same & (t[None, :] < t[:, None])
    masks = np.stack([np.tile(strict, (1, 2)), np.tile(tri, (1, 2)),
                      np.tile(np.eye(w, dtype=bool), (1, 2))]).astype(np.float32)
    rr = np.arange(2 * w)
    hh = rr // B_HEAD
    gmat = np.tile((hh[:, None] == hh[None, :]).astype(np.float32), (2, 1))
    return cum, masks, gmat


def _rwkv_kernel(*refs, g_n, hs, chunks, nseq, c, has_state, has_alias):
    it = iter(refs)
    zr, zk, zv = ([[next(it) for _ in range(g_n)] for _ in range(hs)] for _ in range(3))
    zl = [next(it) for _ in range(g_n)]
    (mu_r, mu_k, mu_v, mu_l, w0, a0, kkp, kap, rkp, gnw, gnb,
     w2, a2, g2, cum, msk, gmat) = (next(it) for _ in range(17))
    if has_state:
        sh_r, sh_k, sh_v, sh_l, s0 = (next(it) for _ in range(5))
    if has_alias:
        next(it)
    o_ref, s_out = next(it), next(it)
    shift_outs = [next(it) for _ in range(4)]
    s_ref = next(it)
    carries = None if has_state else [next(it) for _ in range(4)]
    w = RWKV_ROWS
    n = 2 * w
    blk = chunks * w
    nsq = int(math.log2(c)) - 1
    tb = pl.program_id(2)
    lane = lax.broadcasted_iota(jnp.int32, (w, LANES), 1)
    head0 = lane < B_HEAD
    row = lax.broadcasted_iota(jnp.int32, (w, 1), 0)
    first = (row % c) == 0
    gm = gmat[...]
    gm_f = gm[0:2 * w].astype(F32)

    @pl.when(tb == 0)
    def _():
        if has_state:
            zero_blk = jnp.zeros((B_HEAD, B_HEAD), F32)
            for j in range(hs):
                for s in range(g_n * nseq):
                    top = jnp.concatenate([s0[0, s, 2 * j], zero_blk], axis=1)
                    bot = jnp.concatenate([zero_blk, s0[0, s, 2 * j + 1]], axis=1)
                    s_ref[j, s] = jnp.concatenate([top, bot], axis=0)
        else:
            s_ref[...] = jnp.zeros_like(s_ref)
            for cr in carries:
                cr[...] = jnp.zeros_like(cr)

    def stack(x):
        return jnp.concatenate([jnp.where(head0, x, 0.0), jnp.where(head0, 0.0, x)], axis=0)

    def gsum(x):
        return _dot_sel(x, gm)

    def shifted(ref, sh, cols, carry, mu, g, ci, r):
        x = ref[pl.ds(r, w), :]
        width = x.shape[1]
        if has_state:
            src = jnp.concatenate(
                [jnp.broadcast_to(sh[g * nseq + s:g * nseq + s + 1, cols], (c, width))
                 for s in range(nseq)], axis=0)
        else:
            rp = pl.multiple_of(jnp.maximum(r - SUBLANES, 0), SUBLANES)
            prev8 = jnp.where(ci == 0, carry, ref[pl.ds(rp, SUBLANES), :])
            src = jnp.broadcast_to(prev8[SUBLANES - 1:SUBLANES, :], (w, width))
        prev = jnp.where(first, src, pltpu.roll(x, 1, axis=0))
        return x + mu * (prev - x)

    def tile(j, g, ci, r):
        hc = slice(j * LANES, (j + 1) * LANES)
        lc = slice(0, B_LORA_W + B_LORA_A + B_LORA_G)
        cr = [None] * 4 if carries is None else [carries[0][j, g], carries[1][j, g], carries[2][j, g],
                                                  carries[3][g]]
        xr = shifted(zr[j][g], sh_r if has_state else None, hc, cr[0], mu_r[:, hc], g, ci, r)
        xk = shifted(zk[j][g], sh_k if has_state else None, hc, cr[1], mu_k[:, hc], g, ci, r)
        xv = shifted(zv[j][g], sh_v if has_state else None, hc, cr[2], mu_v[:, hc], g, ci, r)
        xl = shifted(zl[g], sh_l if has_state else None, lc, cr[3], mu_l[...], g, ci, r)
        wd = xl[:, 0:B_LORA_W]
        ad = xl[:, B_LORA_W:B_LORA_W + B_LORA_A]
        gd = xl[:, B_LORA_W + B_LORA_A:]
        wx = -(w0[:, hc] + _dot(jnp.tanh(wd), w2[:, hc]))
        w_raw = -(jnp.maximum(wx, 0.0) + jnp.log(1.0 + jnp.exp(-jnp.abs(wx)))) - 0.5
        lw = -jnp.exp(w_raw)
        aa = _sigmoid(a0[:, hc] + _dot(ad, a2[:, hc]))
        gb = _dot(_sigmoid(gd), g2[:, hc])
        yield
        kk = xk * kkp[:, hc]
        kk = kk * jnp.minimum(lax.rsqrt(gsum(kk * kk)), 1e12)
        k2 = xk * (1.0 + (aa - 1.0) * kap[:, hc])
        a = -kk
        b = kk * aa
        yield
        e = _sel_dot(cum[...], lw)
        yield
        lwc = e[0:w]
        lwl = e[w:n]
        dec_in = jnp.exp(lwc)
        dec_ex = jnp.exp(lwc - lw)
        inv = jnp.exp(-lwc)
        rest = jnp.exp(lwl - lwc)
        a_t = a * dec_ex
        r_t = xr * dec_in
        b_t = b * inv
        k_t = k2 * inv
        gram = _dot_nt(jnp.concatenate([a_t, r_t], axis=0),
                       jnp.concatenate([stack(b_t), stack(k_t)], axis=0))
        yield
        m_ab = gram[0:w, 0:n] * msk[0]
        m_ak = gram[0:w, n:2 * n] * msk[0]
        m_rb = gram[w:n, 0:n] * msk[1]
        m_rk = gram[w:n, n:2 * n] * msk[1]
        p = _dot(m_ab, stack(m_ab))
        tinv = msk[2] + m_ab
        yield
        for lev in range(nsq):
            if lev + 1 < nsq:
                both = _dot(p, jnp.concatenate([stack(p), stack(tinv)], axis=1))
                p = both[:, 0:n]
                tinv = tinv + both[:, n:2 * n]
            else:
                tinv = tinv + _dot(p, stack(tinv))
            yield
        p0a, p0r = [], []
        for s in range(nseq):
            rows = slice(s * c, (s + 1) * c)
            pr = _dot_nt(jnp.concatenate([a_t[rows], r_t[rows]], axis=0), s_ref[j, g * nseq + s])
            p0a.append(pr[0:c])
            p0r.append(pr[c:2 * c])
        p0a = p0a[0] if nseq == 1 else jnp.concatenate(p0a, axis=0)
        p0r = p0r[0] if nseq == 1 else jnp.concatenate(p0r, axis=0)
        yield
        vs = stack(xv)
        rhs = p0a + _dot(m_ak, vs)
        yield
        u_w = _dot(tinv, stack(rhs))
        yield
        y = p0r + _dot(m_rb, stack(u_w)) + _dot(m_rk, vs)
        yield
        b_g = b * rest
        k_g = k2 * rest
        dec_l = jnp.exp(lwl)
        for s in range(nseq):
            rows = slice(s * c, (s + 1) * c)
            upd = _dot_tn(jnp.concatenate([u_w[rows], xv[rows]], axis=0),
                          jnp.concatenate([b_g[rows], k_g[rows]], axis=0))
            s_ref[j, g * nseq + s] = s_ref[j, g * nseq + s] * dec_l[s * c:s * c + 1, :] + gm_f * upd
        yield
        mean = gsum(y) * (1.0 / B_HEAD)
        yield
        d = y - mean
        var = gsum(d * d) * (1.0 / B_HEAD)
        yn = d * lax.rsqrt(var + RWKV_GN_EPS) * gnw[:, hc] + gnb[:, hc]
        yield
        bonus = gsum(xr * k2 * rkp[:, hc])
        o_ref[g, pl.ds(r, w), hc] = ((yn + bonus * xv) * gb).astype(BF16)

    def chunk(ci, carry):
        r = pl.multiple_of(ci * w, w)
        _round_robin([tile(j, g, ci, r) for j in range(hs) for g in range(g_n)])
        return carry

    lax.fori_loop(0, chunks, chunk, 0)

    base = blk - w
    for g in range(g_n):
        per_head = [(zr[j][g], shift_outs[0], j) for j in range(hs)]
        per_head += [(zk[j][g], shift_outs[1], j) for j in range(hs)]
        per_head += [(zv[j][g], shift_outs[2], j) for j in range(hs)]
        for ref, out, j in per_head + [(zl[g], shift_outs[3], 0)]:
            cols = slice(j * LANES, j * LANES + ref.shape[1])
            for s in range(nseq):
                last = base + (s + 1) * c - 1
                out[g * nseq + s:g * nseq + s + 1, cols] = ref[last:last + 1, :]
        if carries is not None:
            for j in range(hs):
                carries[0][j, g] = zr[j][g][blk - SUBLANES:blk, :]
                carries[1][j, g] = zk[j][g][blk - SUBLANES:blk, :]
                carries[2][j, g] = zv[j][g][blk - SUBLANES:blk, :]
            carries[3][g] = zl[g][blk - SUBLANES:blk, :]

    @pl.when(tb == pl.num_programs(2) - 1)
    def _():
        for j in range(hs):
            for s in range(g_n * nseq):
                ss = s_ref[j, s]
                s_out[0, s, 2 * j] = ss[0:B_HEAD, 0:B_HEAD]
                s_out[0, s, 2 * j + 1] = ss[B_HEAD:n, B_HEAD:n]


def _rwkv(z, p, state, shift, prev_out, layer, depth, b, t, row_off):
    pairs = B_HEADS // 2
    til = _Tiling(b, t, RWKV_ROWS, row_off, pairs, 2 * B_HEAD * B_HEAD * 4, RWKV_STREAMS, RWKV_TIME_BLOCK)
    cum, msk, gmat = _rwkv_consts(til.c)
    cb = B_OFF // LANES
    wb = B_WIDTH // LANES
    lora_w = B_LORA_W + B_LORA_A + B_LORA_G
    lb_z = (B_OFF + 3 * B_WIDTH) // lora_w
    lb_s = (3 * B_WIDTH) // lora_w

    def vec(x):
        return x.reshape(1, -1)

    def pspec(rows_, col_off=0):
        return til.hspec(rows_, LANES, col_off)

    in_specs, args = [], []
    for col in (cb, cb + wb, cb + 2 * wb):
        for j, g in til.streams():
            in_specs.append(til.zspec(j, g, LANES, col))
            args.append(z)
    for g in range(til.g):
        in_specs.append(til.zspec(0, g, lora_w, lb_z, per_head=0))
        args.append(z)
    in_specs += [
        pspec(1), pspec(1, wb), pspec(1, 2 * wb),
        pl.BlockSpec((1, lora_w), lambda i, h, tb: (0, lb_s)),
        pspec(1), pspec(1), pspec(1), pspec(1), pspec(1), pspec(1), pspec(1),
        pspec(B_LORA_W), pspec(B_LORA_A), pspec(B_LORA_G),
        _const_spec(cum.shape), _const_spec(msk.shape), _const_spec(gmat.shape),
    ]
    mu = vec(p['rwkv_mu'])
    args += [mu, mu, mu, mu,
             vec(p['rwkv_w0']), vec(p['rwkv_a0']), vec(p['rwkv_kk']), vec(p['rwkv_ka']),
             vec(p['rwkv_rk']), vec(p['rwkv_gn_w']), vec(p['rwkv_gn_b']),
             p['rwkv_w2'].astype(BF16), p['rwkv_a2'].astype(BF16), p['rwkv_g2'].astype(BF16),
             jnp.asarray(cum, BF16), jnp.asarray(msk, F32), jnp.asarray(gmat, BF16)]
    ns = til.g * til.nseq
    if state is not None:
        hw = til.hs * LANES
        wbh = wb // til.hs
        in_specs += [
            pl.BlockSpec((ns, hw), lambda i, h, tb: (i, h)),
            pl.BlockSpec((ns, hw), lambda i, h, tb: (i, wbh + h)),
            pl.BlockSpec((ns, hw), lambda i, h, tb: (i, 2 * wbh + h)),
            pl.BlockSpec((ns, lora_w), lambda i, h, tb: (i, lb_s)),
        ]
        args += [shift, shift, shift, shift]
    has_state, has_alias, aliases, s_spec, s_shape = _state_io(
        til, state, prev_out, layer, depth, b, B_HEADS, 2, B_HEAD, B_HEAD, in_specs, args)
    scratch = [pltpu.VMEM((til.hs, til.g * til.nseq, 2 * B_HEAD, 2 * B_HEAD), F32)]
    if not has_state:
        scratch += [pltpu.VMEM((til.hs, til.g, SUBLANES, LANES), F32) for _ in range(3)]
        scratch += [pltpu.VMEM((til.g, SUBLANES, lora_w), F32)]
    o, s, sh_r, sh_k, sh_v, sh_l = pl.pallas_call(
        functools.partial(_rwkv_kernel, g_n=til.g, hs=til.hs, chunks=til.chunks, nseq=til.nseq, c=til.c,
                          has_state=has_state, has_alias=has_alias),
        grid=(til.steps, til.head_steps, til.nt),
        in_specs=in_specs,
        out_specs=[til.ospec(LANES), s_spec,
                   pl.BlockSpec((ns, til.hs * LANES), lambda i, h, tb: (i, h)),
                   pl.BlockSpec((ns, til.hs * LANES), lambda i, h, tb: (i, h)),
                   pl.BlockSpec((ns, til.hs * LANES), lambda i, h, tb: (i, h)),
                   pl.BlockSpec((ns, lora_w), lambda i, h, tb: (i, 0))],
        out_shape=[til.oshape(B_WIDTH), s_shape] + [
            jax.ShapeDtypeStruct((b, wd), F32) for wd in (B_WIDTH, B_WIDTH, B_WIDTH, lora_w)],
        scratch_shapes=scratch,
        input_output_aliases=aliases,
        compiler_params=_cparams(("parallel", "parallel", "arbitrary")),
        name="rwkv7",
    )(*args)
    return o.reshape(b * t, B_WIDTH), s, jnp.concatenate([sh_r, sh_k, sh_v, sh_l], axis=1)


def _merge_kernel(oa, ob, oc, wa, wb, wc, ga, gb, gc, *rest):
    o_ref = rest[-1]
    acc = _sigmoid(ga[...]) * jnp.dot(oa[...], wa[...].astype(BF16), preferred_element_type=F32)
    acc = acc + _sigmoid(gb[...]) * jnp.dot(ob[...], wb[...].astype(BF16), preferred_element_type=F32)
    acc = acc + _sigmoid(gc[...]) * jnp.dot(oc[...], wc[...].astype(BF16), preferred_element_type=F32)
    o_ref[...] = acc.astype(BF16)


def _merge(z, oa, ob, oc, wa, wb, wc, layer, row_off, prev_out):
    m = oa.shape[0]
    tm = _pick(m, (1024, 512, 256, 128))
    tn = 256
    gblk = G_OFF // tn
    nb = D_MODEL // tn
    assert G_OFF % tn == 0 and row_off % tm == 0
    off = row_off // tm

    def ospec():
        return pl.BlockSpec((tm, oa.shape[1]), lambda i, j: (i, 0))

    def wspec():
        return pl.BlockSpec((None, wa.shape[1], tn), lambda i, j: (layer, 0, j))

    def gspec(br):
        return pl.BlockSpec((tm, tn), lambda i, j: (i + off, gblk + br * nb + j))

    in_specs = [ospec(), ospec(), ospec(), wspec(), wspec(), wspec(), gspec(0), gspec(1), gspec(2)]
    args = [oa, ob, oc, wa, wb, wc, z, z, z]
    aliases = {}
    if prev_out is not None:
        aliases = {len(args): 0}
        in_specs.append(pl.BlockSpec(memory_space=pl.ANY))
        args.append(prev_out)
    return pl.pallas_call(
        _merge_kernel,
        grid=(m // tm, nb),
        in_specs=in_specs,
        out_specs=pl.BlockSpec((tm, tn), lambda i, j: (i + off, j)),
        out_shape=jax.ShapeDtypeStruct((z.shape[0], D_MODEL), BF16),
        input_output_aliases=aliases,
        compiler_params=_cparams(("parallel", "arbitrary")),
        name="merge",
    )(*args)


def _proj_res_kernel(m_ref, w_ref, g_ref, x_ref, o_ref):
    y = jnp.dot(m_ref[...], w_ref[...], preferred_element_type=F32)
    yn = y * lax.rsqrt(jnp.mean(y * y, axis=-1, keepdims=True) + NORM_EPS) * g_ref[...]
    o_ref[...] = x_ref[...] + yn


def _proj_res(mrg, w, g, x, layer):
    m = x.shape[0]
    tm = _pick(m, (512, 256, 128))
    return pl.pallas_call(
        _proj_res_kernel,
        grid=(m // tm,),
        in_specs=[
            pl.BlockSpec((tm, D_MODEL), lambda i: (i, 0)),
            pl.BlockSpec((None, D_MODEL, D_MODEL), lambda i: (layer, 0, 0)),
            pl.BlockSpec((1, D_MODEL), lambda i: (0, 0)),
            pl.BlockSpec((tm, D_MODEL), lambda i: (i, 0)),
        ],
        out_specs=pl.BlockSpec((tm, D_MODEL), lambda i: (i, 0)),
        out_shape=jax.ShapeDtypeStruct((m, D_MODEL), F32),
        compiler_params=_cparams(("parallel",)),
        name="proj_res",
    )(mrg, w, g.reshape(1, D_MODEL), x)


def _gelu(x):
    return 0.5 * x * (1.0 + jnp.tanh(math.sqrt(2.0 / math.pi) * (x + 0.044715 * (x * x * x))))


def _up_act_kernel(*refs, tm, t, blocks_per_seq, has_state, has_alias):
    it = iter(refs)
    x_ref, g_ref, wa, wb, cw, cb = (next(it) for _ in range(6))
    st = next(it) if has_state else None
    if has_alias:
        next(it)
    o_ref, nc_ref, xn_ref = next(it), next(it), next(it)
    tail = None if has_state else next(it)
    i = pl.program_id(0)
    j = pl.program_id(1)
    tn = wa.shape[1]
    sub = UP_SUB

    @pl.when(j == 0)
    def _():
        rows = min(tm, LANES)

        def body(r, carry):
            r0 = pl.multiple_of(r * rows, rows)
            x = x_ref[pl.ds(r0, rows), :]
            ms = jnp.mean(x * x, axis=-1, keepdims=True)
            xn_ref[pl.ds(r0, rows), :] = (x * lax.rsqrt(ms + NORM_EPS) * g_ref[...]).astype(BF16)
            return carry
        lax.fori_loop(0, tm // rows, body, 0)

    if has_state:
        ns = tm // t
        tt = lax.broadcasted_iota(jnp.int32, (ns, t, sub), 1)
    else:
        seq_start = (i % blocks_per_seq) == 0
        rr = lax.broadcasted_iota(jnp.int32, (tm, sub), 0)
    for c in range(tn // sub):
        cols = slice(c * sub, (c + 1) * sub)
        ua = jnp.dot(xn_ref[...], wa[:, cols].astype(BF16), preferred_element_type=F32)
        ub = jnp.dot(xn_ref[...], wb[:, cols].astype(BF16), preferred_element_type=F32)
        if has_state:
            x3 = ua.reshape(ns, t, sub)
            s_old = st[:, 0:1, cols]
            s_new = st[:, 1:2, cols]
            prev1 = jnp.where(tt >= 1, pltpu.roll(x3, 1, axis=1), s_new)
            prev2 = jnp.where(tt >= 2, pltpu.roll(x3, 2, axis=1), jnp.where(tt == 1, s_new, s_old))
            prev1 = prev1.reshape(tm, sub)
            prev2 = prev2.reshape(tm, sub)
            nc_ref[:, :, cols] = x3[:, t - (CONV_W - 1):, :]
        else:
            h = jnp.where(seq_start, 0.0, tail[j, :, cols])
            h1 = h[SUBLANES - 1:SUBLANES, :]
            h2 = h[SUBLANES - 2:SUBLANES - 1, :]
            prev1 = jnp.where(rr == 0, h1, pltpu.roll(ua, 1, axis=0))
            prev2 = jnp.where(rr == 0, h2, jnp.where(rr == 1, h1, pltpu.roll(ua, 2, axis=0)))
            tail[j, :, cols] = ua[tm - SUBLANES:, :]
            nc_ref[0, :, cols] = ua[tm - (CONV_W - 1):, :]
        conv = cb[:, cols] + cw[0:1, cols] * prev2 + cw[1:2, cols] * prev1 + cw[2:3, cols] * ua
        o_ref[:, cols] = (_gelu(conv) * ub).astype(BF16)


def _up_act(x1, g, w_up, cw, cb, state, prev_out, layer, b, t, row_off):
    m = b * t
    total = x1.shape[0]
    has_state = state is not None
    has_alias = prev_out is not None
    tn = UP_TN
    nj = D_FF // tn
    if has_state:
        tm = _pick(m, (1024, 512, 256, 128))
        assert tm % t == 0 and t == SUBLANES
        blocks_per_seq = 1
        nc_spec = pl.BlockSpec((tm // t, CONV_W - 1, tn), lambda i, j: (i, 0, j))
    else:
        tm = _pick(t, (1024, 512, 256, 128))
        blocks_per_seq = t // tm
        nc_spec = pl.BlockSpec((1, CONV_W - 1, tn), lambda i, j: (i, 0, j))
    nc_rows = b * blocks_per_seq
    off = row_off // tm
    assert row_off % tm == 0
    in_specs = [
        pl.BlockSpec((tm, D_MODEL), lambda i, j: (i + off, 0)),
        pl.BlockSpec((1, D_MODEL), lambda i, j: (0, 0)),
        pl.BlockSpec((None, D_MODEL, tn), lambda i, j: (layer, 0, j)),
        pl.BlockSpec((None, D_MODEL, tn), lambda i, j: (layer, 0, nj + j)),
        pl.BlockSpec((CONV_W, tn), lambda i, j: (0, j)),
        pl.BlockSpec((1, tn), lambda i, j: (0, j)),
    ]
    args = [x1, g.reshape(1, D_MODEL), w_up, w_up, cw, cb.reshape(1, D_FF)]
    if has_state:
        in_specs.append(pl.BlockSpec((tm // t, CONV_W - 1, tn), lambda i, j: (i, 0, j)))
        args.append(state)
    aliases = {}
    if has_alias:
        aliases = {len(args): 0}
        in_specs.append(pl.BlockSpec(memory_space=pl.ANY))
        args.append(prev_out)
    scratch = [pltpu.VMEM((tm, D_MODEL), BF16)]
    if not has_state:
        scratch.append(pltpu.VMEM((nj, SUBLANES, tn), F32))
    act, nc = pl.pallas_call(
        functools.partial(_up_act_kernel, tm=tm, t=t, blocks_per_seq=blocks_per_seq,
                          has_state=has_state, has_alias=has_alias),
        grid=(m // tm, nj),
        in_specs=in_specs,
        out_specs=[pl.BlockSpec((tm, tn), lambda i, j: (i + off, j)), nc_spec],
        out_shape=[jax.ShapeDtypeStruct((total, D_FF), BF16),
                   jax.ShapeDtypeStruct((nc_rows, CONV_W - 1, D_FF), F32)],
        scratch_shapes=scratch,
        input_output_aliases=aliases,
        compiler_params=_cparams(("arbitrary", "arbitrary")),
        name="up_act",
    )(*args)
    return act, nc[blocks_per_seq - 1::blocks_per_seq]


def _down_res_kernel(a_ref, w_ref, g_ref, x_ref, o_ref, acc):
    kstep = pl.program_id(1)

    @pl.when(kstep == 0)
    def _():
        acc[...] = jnp.zeros_like(acc)

    acc[...] += jnp.dot(a_ref[...], w_ref[...], preferred_element_type=F32)

    @pl.when(kstep == pl.num_programs(1) - 1)
    def _():
        y = acc[...]
        yn = y * lax.rsqrt(jnp.mean(y * y, axis=-1, keepdims=True) + NORM_EPS) * g_ref[...]
        o_ref[...] = x_ref[...] + yn


def _down_res(act, wd, g, x1, layer, row_off, rows):
    tm = _pick(rows, (512, 256, 128))
    tk = DOWN_TK
    off = row_off // tm
    assert row_off % tm == 0
    return pl.pallas_call(
        _down_res_kernel,
        grid=(rows // tm, D_FF // tk),
        in_specs=[
            pl.BlockSpec((tm, tk), lambda i, k: (i + off, k)),
            pl.BlockSpec((None, tk, D_MODEL), lambda i, k: (layer, k, 0)),
            pl.BlockSpec((1, D_MODEL), lambda i, k: (0, 0)),
            pl.BlockSpec((tm, D_MODEL), lambda i, k: (i + off, 0)),
        ],
        out_specs=pl.BlockSpec((tm, D_MODEL), lambda i, k: (i, 0)),
        out_shape=jax.ShapeDtypeStruct((rows, D_MODEL), F32),
        scratch_shapes=[pltpu.VMEM((tm, D_MODEL), F32)],
        compiler_params=_cparams(("parallel", "arbitrary")),
        name="down_res",
    )(act, wd, g.reshape(1, D_MODEL), x1)


def _layer(x, groups, lb, p, layer, depth, prev, split_out):
    z = _rms_matmul(x, p['pre_mix_g'], p['w_in'], layer, tn=1280)
    states = []
    mrg = None
    row = 0
    for gi, (b, t, t0, st) in enumerate(groups):
        s_a, s_b, s_sh, s_c, _ = st if st is not None else (None,) * 5
        pv = prev[gi] if prev is not None else (None,) * 3
        o_a, n_a = _hgrn(z, lb, p['a_norm_g'], s_a, pv[0], layer, depth, b, t, row)
        o_b, n_b, n_sh = _rwkv(z, p, s_b, s_sh, pv[1], layer, depth, b, t, row)
        o_c, n_c = _retention(z, s_c, pv[2], layer, depth, b, t, t0, row)
        mrg = _merge(z, o_a, o_b, o_c, p['w_br_a'], p['w_br_b'], p['w_br_c'], layer, row, mrg)
        states.append([n_a, n_b, n_sh, n_c])
        row += b * t
    x1 = _proj_res(mrg, p['w_out'], p['post_mix_g'], x, layer)
    act = None
    row = 0
    for gi, (b, t, t0, st) in enumerate(groups):
        s_cv = st[4] if st is not None else None
        act, n_cv = _up_act(x1, p['pre_ffn_g'], p['w_up'], p['conv_w'], p['conv_b'], s_cv, act,
                            layer, b, t, row)
        states[gi].append(n_cv)
        row += b * t
    if split_out:
        x2, row = [], 0
        for (b, t, _, _) in groups:
            x2.append(_down_res(act, p['w_down'], p['post_ffn_g'], x1, layer, row, b * t))
            row += b * t
    else:
        x2 = _down_res(act, p['w_down'], p['post_ffn_g'], x1, layer, 0, x1.shape[0])
    return x2, states


def kernel(x_prompt, x_sample, state_hgrn, state_rwkv, state_rwkv_shift, state_ret, state_conv,
           lb_logits, pre_mix_g, w_in, a_norm_g, rwkv_mu, rwkv_w0, rwkv_w2, rwkv_a0, rwkv_a2,
           rwkv_g2, rwkv_kk, rwkv_ka, rwkv_rk, rwkv_gn_w, rwkv_gn_b, w_br_a, w_br_b, w_br_c,
           w_out, post_mix_g, pre_ffn_g, w_up, conv_w, conv_b, w_down, post_ffn_g):
    depth = w_in.shape[0]
    bp, tp, _ = x_prompt.shape
    bs, ts, _ = x_sample.shape
    past_len = 16384
    lb_soft = jax.nn.softmax(lb_logits.astype(F32), axis=0)
    lbs = jnp.cumsum(lb_soft, axis=0) - lb_soft[0]
    big = {'w_in': w_in, 'w_br_a': w_br_a, 'w_br_b': w_br_b,
           'w_br_c': w_br_c, 'w_out': w_out.astype(BF16), 'w_up': w_up,
           'w_down': w_down.astype(BF16)}
    x = jnp.concatenate([x_prompt.reshape(bp * tp, D_MODEL), x_sample.reshape(bs * ts, D_MODEL)], axis=0)
    small = [[[], []], [[], []]]
    prev = None
    for l in range(depth):
        p = dict(big)
        p.update({
            'pre_mix_g': pre_mix_g[l], 'a_norm_g': a_norm_g[l],
            'rwkv_mu': rwkv_mu[l], 'rwkv_w0': rwkv_w0[l], 'rwkv_w2': rwkv_w2[l],
            'rwkv_a0': rwkv_a0[l], 'rwkv_a2': rwkv_a2[l], 'rwkv_g2': rwkv_g2[l],
            'rwkv_kk': rwkv_kk[l], 'rwkv_ka': rwkv_ka[l], 'rwkv_rk': rwkv_rk[l],
            'rwkv_gn_w': rwkv_gn_w[l], 'rwkv_gn_b': rwkv_gn_b[l],
            'post_mix_g': post_mix_g[l], 'pre_ffn_g': pre_ffn_g[l],
            'conv_w': conv_w[l], 'conv_b': conv_b[l], 'post_ffn_g': post_ffn_g[l],
        })
        groups = [
            (bp, tp, 0, None),
            (bs, ts, past_len, (state_hgrn, state_rwkv, state_rwkv_shift[l], state_ret, state_conv[l])),
        ]
        x, states = _layer(x, groups, lbs[l], p, l, depth, prev, l == depth - 1)
        prev = [(st[0], st[1], st[3]) for st in states]
        for gi, st in enumerate(states):
            small[gi][0].append(st[2])
            small[gi][1].append(st[4])
    y_p = x[0].reshape(bp, tp, D_MODEL)
    y_s = x[1].reshape(bs, ts, D_MODEL)
    outs = []
    for gi in range(2):
        outs += [prev[gi][0], prev[gi][1], jnp.stack(small[gi][0]), prev[gi][2], jnp.stack(small[gi][1])]
    return (y_p, y_s, *outs)
```
